```python
import jax, jax.numpy as jnp
from jax import lax
import numpy as np

D_MODEL = 1024
BATCH = 8
SEQ = 2048
DEPTH = 1
DEC_BATCH = 8
DEC_SEQ = 16
PAST_LEN = 1024

CHUNK = 64
Q_BLOCK = 128
D_MIX = D_MODEL
SB_WIDTH = D_MIX // 2
SB_HEAD_DIM = 64
SB_HEADS = SB_WIDTH // SB_HEAD_DIM
HG_WIDTH = D_MIX - SB_WIDTH
HG_HEAD_DIM = 128
HG_HEADS = HG_WIDTH // HG_HEAD_DIM
SPLIT_WIDTHS = [SB_WIDTH] * 4 + [HG_WIDTH] * 4
D_IN = sum(SPLIT_WIDTHS)
DEEPNORM_ALPHA = (2 * DEPTH) ** 0.25
DEEPNORM_BETA = (8 * DEPTH) ** -0.25
LN_EPS = 1e-5
RMS_EPS = 1e-6

kernel_name = "stickbreak_hgrn2_deepnorm_stream_step"


def layer_norm(x, g, b):
    x = x.astype(jnp.float32)
    mu = jnp.mean(x, axis=-1, keepdims=True)
    var = jnp.mean(jnp.square(x - mu), axis=-1, keepdims=True)
    return (x - mu) * lax.rsqrt(var + LN_EPS) * g.astype(jnp.float32) + b.astype(jnp.float32)


def stick_breaking_attention(q, k, v, q_offset):
    B, H, Tq, dh = q.shape
    Tk = k.shape[2]
    blk = min(Q_BLOCK, Tq)
    nb = Tq // blk
    scale = dh ** -0.5
    kf = k.astype(jnp.float32)
    vf = v.astype(jnp.float32)
    kpos = jnp.arange(Tk)
    qb = q.astype(jnp.float32).reshape(B, H, nb, blk, dh).transpose(2, 0, 1, 3, 4)
    starts = q_offset + blk * jnp.arange(nb)

    def one_block(args):
        qblk, start = args
        qpos = start + jnp.arange(blk)
        z = jnp.einsum('bhqd,bhkd->bhqk', qblk, kf) * scale
        mask = kpos[None, :] < qpos[:, None]
        log_keep = jnp.where(mask, jax.nn.log_sigmoid(-z), 0.0)
        later = lax.cumsum(log_keep, axis=3, reverse=True) - log_keep
        w = jnp.where(mask, jnp.exp(jax.nn.log_sigmoid(z) + later), 0.0)
        return jnp.einsum('bhqk,bhkd->bhqd', w, vf)

    o = lax.map(one_block, (qb, starts))
    return o.transpose(1, 2, 0, 3, 4).reshape(B, H, Tq, dh)


def hgrn2_recurrence(q, k, g, i, s0):
    B, H, T, dk = q.shape
    dv = i.shape[-1]
    c = min(CHUNK, T)
    n = T // c

    def to_chunks(a):
        return a.reshape(B, H, n, c, a.shape[-1]).transpose(2, 0, 1, 3, 4)

    causal = jnp.tril(jnp.ones((c, c), dtype=bool))[:, :, None]

    def step(S, xs):
        qc, kc, gc, ic = xs
        b = jnp.cumsum(gc, axis=2)
        diff = b[:, :, :, None, :] - b[:, :, None, :, :]
        decay = jnp.where(causal, jnp.exp(jnp.minimum(diff, 0.0)), 0.0)
        scores = jnp.einsum('bhtd,bhsd,bhtsd->bhts', qc, kc, decay)
        o = (jnp.einsum('bhts,bhsv->bhtv', scores, ic)
             + jnp.einsum('bhtd,bhdv->bhtv', qc * jnp.exp(b), S))
        b_last = b[:, :, -1:, :]
        S_new = (jnp.exp(b_last)[:, :, 0, :, None] * S
                 + jnp.einsum('bhsd,bhsv->bhdv', kc * jnp.exp(b_last - b), ic))
        return S_new, o

    S_fin, o = lax.scan(step, s0, (to_chunks(q), to_chunks(k), to_chunks(g), to_chunks(i)))
    return o.transpose(1, 2, 0, 3, 4).reshape(B, H, T, dv), S_fin


def encoder_layer(x, past_k, past_v, s0, w_in, w_out, lb, norm_g, ln_g, ln_b, q_offset):
    Bn, T, _ = x.shape
    proj = jnp.einsum('btd,de->bte', x, w_in)
    idx = [int(v) for v in np.cumsum(SPLIT_WIDTHS)[:-1]]
    qa, ka, va, ga, qh, fh, ih, gh = jnp.split(proj, idx, axis=-1)

    def heads(a, h):
        return a.reshape(Bn, T, h, -1).transpose(0, 2, 1, 3)

    qa, ka, va = heads(qa, SB_HEADS), heads(ka, SB_HEADS), heads(va, SB_HEADS)
    if past_k is None:
        k_all, v_all = ka, va
    else:
        k_all = jnp.concatenate([past_k.astype(ka.dtype), ka], axis=2)
        v_all = jnp.concatenate([past_v.astype(va.dtype), va], axis=2)
    o_sb = stick_breaking_attention(qa, k_all, v_all, q_offset)
    o_sb = o_sb.transpose(0, 2, 1, 3).reshape(Bn, T, SB_WIDTH) * jax.nn.silu(ga.astype(jnp.float32))

    f = lb + (1.0 - lb) * jax.nn.sigmoid(fh.astype(jnp.float32))
    g_log = jnp.log(f)
    k_in = 1.0 - f
    q_h = jax.nn.silu(qh.astype(jnp.float32))
    o_hg, s_new = hgrn2_recurrence(heads(q_h, HG_HEADS), heads(k_in, HG_HEADS),
                                   heads(g_log, HG_HEADS), heads(ih.astype(jnp.float32), HG_HEADS),
                                   s0.astype(jnp.float32))
    o_hg = o_hg * lax.rsqrt(jnp.mean(jnp.square(o_hg), axis=-1, keepdims=True) + RMS_EPS)
    o_hg = (o_hg.transpose(0, 2, 1, 3).reshape(Bn, T, HG_WIDTH) * norm_g.astype(jnp.float32)
            * jax.nn.silu(gh.astype(jnp.float32)))

    mixed = jnp.concatenate([o_sb, o_hg], axis=-1).astype(x.dtype)
    out = jnp.einsum('bte,ed->btd', mixed, w_out)
    y = layer_norm(DEEPNORM_ALPHA * x.astype(jnp.float32) + out.astype(jnp.float32), ln_g, ln_b)
    return y.astype(x.dtype), ka, va, s_new


def setup_inputs(seed: int = 0) -> dict:
    key = jax.random.key(seed)
    ks = jax.random.split(key, 12)
    f32 = jnp.float32
    x_prompt = jax.random.normal(ks[0], (BATCH, SEQ, D_MODEL), f32)
    x_sample = jax.random.normal(ks[1], (DEC_BATCH, DEC_SEQ, D_MODEL), f32)
    cache_k = jax.random.normal(ks[2], (DEPTH, DEC_BATCH, SB_HEADS, PAST_LEN, SB_HEAD_DIM), f32)
    cache_v = DEEPNORM_BETA * jax.random.normal(ks[3], (DEPTH, DEC_BATCH, SB_HEADS, PAST_LEN, SB_HEAD_DIM), f32)
    state_s = 0.5 * jax.random.normal(ks[4], (DEPTH, DEC_BATCH, HG_HEADS, HG_HEAD_DIM, HG_HEAD_DIM), f32)
    col_scale = jnp.concatenate([
        jnp.ones((SB_WIDTH * 2,), f32), jnp.full((SB_WIDTH,), DEEPNORM_BETA, f32), jnp.ones((SB_WIDTH,), f32),
        jnp.ones((HG_WIDTH * 2,), f32), jnp.full((HG_WIDTH,), DEEPNORM_BETA, f32), jnp.ones((HG_WIDTH,), f32)])
    w_in = jax.random.normal(ks[5], (DEPTH, D_MODEL, D_IN), f32) * (D_MODEL ** -0.5) * col_scale
    w_out = jax.random.normal(ks[6], (DEPTH, D_MIX, D_MODEL), f32) * (D_MIX ** -0.5) * DEEPNORM_BETA
    lb_logits = 0.1 * jax.random.normal(ks[7], (DEPTH + 1, HG_WIDTH), f32)
    hgrn_norm_g = 1.0 + 0.01 * jax.random.normal(ks[8], (DEPTH, HG_WIDTH), f32)
    ln_g = 1.0 + 0.01 * jax.random.normal(ks[9], (DEPTH, D_MODEL), f32)
    ln_b = 0.01 * jax.random.normal(ks[10], (DEPTH, D_MODEL), f32)
    return {"x_prompt": x_prompt, "x_sample": x_sample, "cache_k": cache_k, "cache_v": cache_v,
            "state_s": state_s, "w_in": w_in, "w_out": w_out, "lb_logits": lb_logits,
            "hgrn_norm_g": hgrn_norm_g, "ln_g": ln_g, "ln_b": ln_b}


def reference(x_prompt, x_sample, cache_k, cache_v, state_s, w_in, w_out, lb_logits,
              hgrn_norm_g, ln_g, ln_b):
    lower_bounds = jnp.cumsum(jax.nn.softmax(lb_logits.astype(jnp.float32), axis=0), axis=0)
    past_len = cache_k.shape[3]
    yp, ys = x_prompt, x_sample
    kp_l, vp_l, sp_l, ks_l, vs_l, ss_l = [], [], [], [], [], []
    for l in range(DEPTH):
        s0p = jnp.zeros((x_prompt.shape[0], HG_HEADS, HG_HEAD_DIM, HG_HEAD_DIM), jnp.float32)
        yp, kp, vp, sp = encoder_layer(yp, None, None, s0p, w_in[l], w_out[l], lower_bounds[l],
                                       hgrn_norm_g[l], ln_g[l], ln_b[l], 0)
        ys, kn, vn, sn = encoder_layer(ys, cache_k[l], cache_v[l], state_s[l], w_in[l], w_out[l],
                                       lower_bounds[l], hgrn_norm_g[l], ln_g[l], ln_b[l], past_len)
        kp_l.append(kp); vp_l.append(vp); sp_l.append(sp)
        ks_l.append(kn); vs_l.append(vn); ss_l.append(sn)
    return (yp, ys, jnp.stack(kp_l), jnp.stack(vp_l), jnp.stack(sp_l),
            jnp.stack(ks_l), jnp.stack(vs_l), jnp.stack(ss_l))
```

```python
import functools
import math

import numpy as np
import jax
import jax.numpy as jnp
from jax import lax
from jax.experimental import pallas as pl
from jax.experimental.pallas import tpu as pltpu

F32 = jnp.float32
BF16 = jnp.bfloat16

D_MODEL = 1024
GROUP_W = 512
N_GROUPS = 8
SB_HEADS = 8
SB_HEAD_DIM = 64
HG_HEADS = 4
HG_HEAD_DIM = 128
LN_EPS = 1e-5
RMS_EPS = 1e-6

KEY_TILE = 256
HEADS_PER_STEP = 2
VMEM_LIMIT = 56 * 1024 * 1024


def _sigmoid(x):
    return 1.0 / (1.0 + jnp.exp(-x))


def _dot(a, b):
    return jnp.dot(a, b, preferred_element_type=F32)


def _dot_nt(a, b):
    return lax.dot_general(a, b, (((1,), (1,)), ((), ())), preferred_element_type=F32)


def _dot_tn(a, b):
    return lax.dot_general(a, b, (((0,), (0,)), ((), ())), preferred_element_type=F32)


def _split_bf16(x, parts):
    out = []
    for _ in range(parts - 1):
        p = x.astype(BF16)
        out.append(p)
        x = x - p.astype(F32)
    out.append(x.astype(BF16))
    return out


def _proj_kernel(x_ref, w_ref, qa_ref, k_ref, va_ref, sga_ref, qh_ref, fh_ref, ih_ref, gh_ref,
                 ko_ref, vo_ref, *, nb, tm, k_transposed):
    m = nb * tm
    x = x_ref[...].reshape(m, D_MODEL).astype(BF16)

    def col(c):
        return _dot(x, w_ref[:, c * GROUP_W:(c + 1) * GROUP_W])

    def put(ref, val):
        ref[...] = val.reshape(nb, tm, GROUP_W).astype(ref.dtype)

    def put_heads(ref, val):
        for b in range(nb):
            for h in range(SB_HEADS):
                ref[0, b, h] = val[b * tm:(b + 1) * tm, h * SB_HEAD_DIM:(h + 1) * SB_HEAD_DIM]

    put(qa_ref, col(0) * (SB_HEAD_DIM ** -0.5))
    k = col(1)
    put_heads(ko_ref, k)
    if k_transposed:
        kt = k.T.astype(BF16)
        for j in range(m // KEY_TILE):
            k_ref[0, j] = kt[:, j * KEY_TILE:(j + 1) * KEY_TILE]
    else:
        put(k_ref, k)
    v = col(2)
    put_heads(vo_ref, v)
    put(va_ref, v)
    ga = col(3)
    put(sga_ref, ga * _sigmoid(ga))
    put(qh_ref, col(4))
    put(fh_ref, col(5))
    put(ih_ref, col(6))
    put(gh_ref, col(7))


def _project(x, w_bf, *, nb, tm, k_transposed):
    B, T, _ = x.shape
    grid = (B // nb, T // tm)
    act = lambda dt: jax.ShapeDtypeStruct((B, T, GROUP_W), dt)
    act_spec = pl.BlockSpec((nb, tm, GROUP_W), lambda b, t: (b, t, 0))
    if k_transposed:
        assert nb == 1 and tm % KEY_TILE == 0
        k_shape = jax.ShapeDtypeStruct((B, T // KEY_TILE, GROUP_W, KEY_TILE), BF16)
        k_spec = pl.BlockSpec((1, tm // KEY_TILE, GROUP_W, KEY_TILE), lambda b, t: (b, t, 0, 0))
    else:
        k_shape, k_spec = act(BF16), act_spec
    kv_shape = jax.ShapeDtypeStruct((1, B, SB_HEADS, T, SB_HEAD_DIM), F32)
    kv_spec = pl.BlockSpec((1, nb, SB_HEADS, tm, SB_HEAD_DIM), lambda b, t: (0, b, 0, t, 0))
    return pl.pallas_call(
        functools.partial(_proj_kernel, nb=nb, tm=tm, k_transposed=k_transposed),
        grid=grid,
        in_specs=[pl.BlockSpec((nb, tm, D_MODEL), lambda b, t: (b, t, 0)),
                  pl.BlockSpec((D_MODEL, N_GROUPS * GROUP_W), lambda b, t: (0, 0))],
        out_specs=[act_spec, k_spec, act_spec, act_spec, act_spec, act_spec, act_spec, act_spec,
                   kv_spec, kv_spec],
        out_shape=[act(BF16), k_shape, act(BF16), act(BF16), act(BF16), act(F32), act(BF16),
                   act(BF16), kv_shape, kv_shape],
        compiler_params=pltpu.CompilerParams(
            dimension_semantics=("arbitrary", "arbitrary"), vmem_limit_bytes=VMEM_LIMIT),
    )(x, w_bf)


def _suffix_matrix(n):
    r = lax.broadcasted_iota(jnp.int32, (n, n), 0)
    c = lax.broadcasted_iota(jnp.int32, (n, n), 1)
    return jnp.where(r >= c, 1.0, 0.0).astype(BF16)


def _sb_tile(z, v, suffix, run, acc, causal):
    log_keep = -(jnp.maximum(z, 0.0) + jnp.log(1.0 + jnp.exp(-jnp.abs(z))))
    if causal is not None:
        log_keep = jnp.where(causal, log_keep, 0.0)
    hi, lo = _split_bf16(log_keep, 2)
    incl = _dot(hi, suffix) + _dot(lo, suffix)
    w = jnp.exp(z + incl + run)
    if causal is not None:
        w = jnp.where(causal, w, 0.0)
    acc = acc + _dot(w.astype(BF16), v)
    run = run + incl[:, 0:1]
    return run, acc


def _sb_prompt_kernel(q_ref, kt_ref, v_ref, g_ref, o_ref):
    tq = q_ref.shape[1]
    i = pl.program_id(2)
    suffix = _suffix_matrix(KEY_TILE)
    r = lax.broadcasted_iota(jnp.int32, (tq, KEY_TILE), 0)
    c = lax.broadcasted_iota(jnp.int32, (tq, KEY_TILE), 1)
    causal = c < r
    outs = []
    for hh in range(HEADS_PER_STEP):
        lanes = slice(hh * SB_HEAD_DIM, (hh + 1) * SB_HEAD_DIM)
        q = q_ref[0, :, lanes]

        def tile(j, run, acc, mask):
            z = _dot(q, kt_ref[0, j, lanes, :])
            return _sb_tile(z, v_ref[0, j, :, lanes], suffix, run, acc, mask)

        run = jnp.zeros((tq, 1), F32)
        acc = jnp.zeros((tq, SB_HEAD_DIM), F32)
        run, acc = tile(i, run, acc, causal)
        run, acc = lax.fori_loop(0, i, lambda n, ra: tile(i - 1 - n, ra[0], ra[1], None), (run, acc))
        outs.append(acc)
    o = jnp.concatenate(outs, axis=1) * g_ref[0].astype(F32)
    o_ref[0] = o.astype(o_ref.dtype)


def _sb_prompt(qa, kt, va, sga):
    B, T, _ = qa.shape
    tq = KEY_TILE
    nkt = T // KEY_TILE
    lanes = HEADS_PER_STEP * SB_HEAD_DIM
    v4 = va.reshape(B, nkt, KEY_TILE, GROUP_W)
    row_spec = pl.BlockSpec((1, tq, lanes), lambda b, h, i: (b, i, h))
    return pl.pallas_call(
        _sb_prompt_kernel,
        grid=(B, GROUP_W // lanes, T // tq),
        in_specs=[row_spec,
                  pl.BlockSpec((1, nkt, lanes, KEY_TILE), lambda b, h, i: (b, 0, h, 0)),
                  pl.BlockSpec((1, nkt, KEY_TILE, lanes), lambda b, h, i: (b, 0, 0, h)),
                  row_spec],
        out_specs=row_spec,
        out_shape=jax.ShapeDtypeStruct((B, T, GROUP_W), BF16),
        compiler_params=pltpu.CompilerParams(
            dimension_semantics=("arbitrary", "arbitrary", "arbitrary"),
            vmem_limit_bytes=VMEM_LIMIT),
    )(qa, kt, v4, sga)


def _sb_decode_kernel(q_ref, kn_ref, vn_ref, ck_ref, cv_ref, g_ref, o_ref):
    tq = q_ref.shape[1]
    past = ck_ref.shape[3]
    suffix = _suffix_matrix(KEY_TILE)
    suffix_new = _suffix_matrix(tq)
    r = lax.broadcasted_iota(jnp.int32, (tq, tq), 0)
    c = lax.broadcasted_iota(jnp.int32, (tq, tq), 1)
    causal = c < r
    outs = []
    for hh in range(HEADS_PER_STEP):
        lanes = slice(hh * SB_HEAD_DIM, (hh + 1) * SB_HEAD_DIM)
        q = q_ref[0, :, lanes]
        run = jnp.zeros((tq, 1), F32)
        acc = jnp.zeros((tq, SB_HEAD_DIM), F32)
        run, acc = _sb_tile(_dot_nt(q, kn_ref[0, :, lanes]), vn_ref[0, :, lanes], suffix_new,
                            run, acc, causal)
        for j in reversed(range(past // KEY_TILE)):
            rows = slice(j * KEY_TILE, (j + 1) * KEY_TILE)
            z = _dot_nt(q, ck_ref[0, 0, hh, rows, :].astype(BF16))
            run, acc = _sb_tile(z, cv_ref[0, 0, hh, rows, :].astype(BF16), suffix, run, acc, None)
        outs.append(acc)
    o = jnp.concatenate(outs, axis=1) * g_ref[0].astype(F32)
    o_ref[0] = o.astype(o_ref.dtype)


def _sb_decode(qa, kn, vn, cache_k, cache_v, layer, sga):
    B, T, _ = qa.shape
    past = cache_k.shape[3]
    assert past % KEY_TILE == 0
    lanes = HEADS_PER_STEP * SB_HEAD_DIM
    row_spec = pl.BlockSpec((1, T, lanes), lambda b, h: (b, 0, h))
    cache_spec = pl.BlockSpec((1, 1, HEADS_PER_STEP, past, SB_HEAD_DIM),
                              lambda b, h: (layer, b, h, 0, 0))
    return pl.pallas_call(
        _sb_decode_kernel,
        grid=(B, GROUP_W // lanes),
        in_specs=[row_spec, row_spec, row_spec, cache_spec, cache_spec, row_spec],
        out_specs=row_spec,
        out_shape=jax.ShapeDtypeStruct((B, T, GROUP_W), BF16),
        compiler_params=pltpu.CompilerParams(
            dimension_semantics=("arbitrary", "arbitrary"), vmem_limit_bytes=VMEM_LIMIT),
    )(qa, kn, vn, cache_k, cache_v, sga)


def _hgrn_tables(C):
    nl = int(math.log2(C))
    assert 1 << nl == C
    t = np.arange(C)[:, None]
    j = np.arange(C)[None, :]
    blocks = []
    for l in range(nl):
        half = 1 << l
        mid = (t >> (l + 1) << (l + 1)) + half
        right = (t & half) != 0
        blocks.append(np.where(right, (j >= mid) & (j <= t), (j > t) & (j < mid)))
    blocks.append(j <= t)
    blocks.append(j > t)
    prefix = np.concatenate(blocks, axis=0).astype(np.float32)
    x = t ^ j
    msb = np.floor(np.log2(np.maximum(x, 1))).astype(np.int32)
    level_of = np.where(t > j, msb, np.where(t == j, nl, -1)).astype(np.int32)
    return jnp.asarray(prefix, BF16), jnp.asarray(level_of), nl


def _hgrn_kernel(*refs, C, nl, layer, has_s0):
    if has_s0:
        (qh_ref, fh_ref, ih_ref, gh_ref, pm_ref, lv_ref, lbl_ref, ng_ref, s0_ref,
         o_ref, so_ref, st_ref) = refs
    else:
        (qh_ref, fh_ref, ih_ref, gh_ref, pm_ref, lv_ref, lbl_ref, ng_ref,
         o_ref, so_ref, st_ref) = refs
    t = pl.program_id(1)

    @pl.when(t == 0)
    def _():
        if has_s0:
            for h in range(HG_HEADS):
                st_ref[h] = s0_ref[0, 0, h].T
        else:
            st_ref[...] = jnp.zeros_like(st_ref)

    logits = lbl_ref[...]
    e = jnp.exp(logits - jnp.max(logits, axis=0, keepdims=True))
    lb = jnp.sum(e[:layer + 1], axis=0, keepdims=True) / jnp.sum(e, axis=0, keepdims=True)

    f = lb + (1.0 - lb) * _sigmoid(fh_ref[0])
    g = jnp.log(f)
    kin = 1.0 - f
    qh = qh_ref[0].astype(F32)
    qs = qh * _sigmoid(qh)
    g3 = jnp.concatenate(_split_bf16(g, 3), axis=1)

    def prefix(l):
        p = _dot(pm_ref[l * C:(l + 1) * C, :], g3)
        return p[:, :GROUP_W] + p[:, GROUP_W:2 * GROUP_W] + p[:, 2 * GROUP_W:]

    def head(a, h):
        return a[:, h * HG_HEAD_DIM:(h + 1) * HG_HEAD_DIM]

    level_of = lv_ref[...]
    qs_b = qs.astype(BF16)
    kin_b = kin.astype(BF16)
    scores = [jnp.where(level_of == nl, _dot_nt(head(qs_b, h), head(kin_b, h)), 0.0)
              for h in range(HG_HEADS)]
    for l in range(nl):
        e_l = jnp.exp(prefix(l))
        q_l = (qs * e_l).astype(BF16)
        k_l = (kin * e_l).astype(BF16)
        for h in range(HG_HEADS):
            scores[h] = jnp.where(level_of == l, _dot_nt(head(q_l, h), head(k_l, h)), scores[h])

    b = prefix(nl)
    q_in = (qs * jnp.exp(b)).astype(BF16)
    k_out = (kin * jnp.exp(prefix(nl + 1))).astype(BF16)
    chunk_decay = jnp.exp(b[C - 1:C, :])
    i_b = ih_ref[0]
    gh = gh_ref[0].astype(F32)
    gate = ng_ref[...] * (gh * _sigmoid(gh))
    outs = []
    for h in range(HG_HEADS):
        st = st_ref[h]
        o = _dot(scores[h].astype(BF16), head(i_b, h)) + _dot_nt(head(q_in, h), st.astype(BF16))
        st_ref[h] = st * head(chunk_decay, h) + _dot_tn(head(i_b, h), head(k_out, h))
        outs.append(o * lax.rsqrt(jnp.mean(o * o, axis=-1, keepdims=True) + RMS_EPS))
    o_ref[0] = (jnp.concatenate(outs, axis=1) * gate).astype(o_ref.dtype)

    @pl.when(t == pl.num_programs(1) - 1)
    def _():
        for h in range(HG_HEADS):
            so_ref[0, 0, h] = st_ref[h].T


def _hgrn(qh, fh, ih, gh, lb_logits, norm_g, s0, layer, *, C):
    B, T, _ = qh.shape
    prefix, level_of, nl = _hgrn_tables(C)
    has_s0 = s0 is not None
    row_spec = pl.BlockSpec((1, C, GROUP_W), lambda b, t: (b, t, 0))
    const = lambda shape: pl.BlockSpec(shape, lambda b, t: (0,) * len(shape))
    state_shape = (1, 1, HG_HEADS, HG_HEAD_DIM, HG_HEAD_DIM)
    in_specs = [row_spec, row_spec, row_spec, row_spec, const(prefix.shape), const(level_of.shape),
                const(lb_logits.shape), const((1, GROUP_W))]
    args = [qh, fh, ih, gh, prefix, level_of, lb_logits.astype(F32),
            norm_g.reshape(1, GROUP_W).astype(F32)]
    if has_s0:
        in_specs.append(pl.BlockSpec(state_shape, lambda b, t: (layer, b, 0, 0, 0)))
        args.append(s0)
    return pl.pallas_call(
        functools.partial(_hgrn_kernel, C=C, nl=nl, layer=layer, has_s0=has_s0),
        grid=(B, T // C),
        in_specs=in_specs,
        out_specs=[row_spec, pl.BlockSpec(state_shape, lambda b, t: (0, b, 0, 0, 0))],
        out_shape=[jax.ShapeDtypeStruct((B, T, GROUP_W), BF16),
                   jax.ShapeDtypeStruct((1, B, HG_HEADS, HG_HEAD_DIM, HG_HEAD_DIM), F32)],
        scratch_shapes=[pltpu.VMEM((HG_HEADS, HG_HEAD_DIM, HG_HEAD_DIM), F32)],
        compiler_params=pltpu.CompilerParams(
            dimension_semantics=("arbitrary", "arbitrary"), vmem_limit_bytes=VMEM_LIMIT),
    )(*args)


def _out_kernel(ma_ref, mb_ref, x_ref, w_ref, g_ref, b_ref, y_ref, *, alpha):
    out = _dot(ma_ref[...], w_ref[:GROUP_W, :]) + _dot(mb_ref[...], w_ref[GROUP_W:, :])
    hid = alpha * x_ref[...] + out
    mu = jnp.mean(hid, axis=-1, keepdims=True)
    cen = hid - mu
    var = jnp.mean(cen * cen, axis=-1, keepdims=True)
    y_ref[...] = cen * lax.rsqrt(var + LN_EPS) * g_ref[...] + b_ref[...]


def _out_project(ma, mb, x, w_bf, ln_g, ln_b, *, tm, alpha):
    B, T, _ = x.shape
    n = B * T
    mix_spec = pl.BlockSpec((tm, GROUP_W), lambda i: (i, 0))
    x_spec = pl.BlockSpec((tm, D_MODEL), lambda i: (i, 0))
    const = lambda shape: pl.BlockSpec(shape, lambda i: (0, 0))
    y = pl.pallas_call(
        functools.partial(_out_kernel, alpha=alpha),
        grid=(n // tm,),
        in_specs=[mix_spec, mix_spec, x_spec, const((2 * GROUP_W, D_MODEL)), const((1, D_MODEL)),
                  const((1, D_MODEL))],
        out_specs=x_spec,
        out_shape=jax.ShapeDtypeStruct((n, D_MODEL), F32),
        compiler_params=pltpu.CompilerParams(
            dimension_semantics=("arbitrary",), vmem_limit_bytes=VMEM_LIMIT),
    )(ma.reshape(n, GROUP_W), mb.reshape(n, GROUP_W), x.reshape(n, D_MODEL), w_bf,
      ln_g.reshape(1, D_MODEL).astype(F32), ln_b.reshape(1, D_MODEL).astype(F32))
    return y.reshape(B, T, D_MODEL)


PROMPT_ROWS = 512
PROMPT_CHUNK = 256


def _layer(x, cache_k, cache_v, state_s, layer, w_in_bf, w_out_bf, lb_logits, norm_g, ln_g, ln_b,
           alpha):
    B, T, _ = x.shape
    decode = cache_k is not None
    if decode:
        qa, k, va, sga, qh, fh, ih, gh, k_out, v_out = _project(
            x, w_in_bf, nb=B, tm=T, k_transposed=False)
        ma = _sb_decode(qa, k, va, cache_k, cache_v, layer, sga)
        mb, s_out = _hgrn(qh, fh, ih, gh, lb_logits, norm_g, state_s, layer, C=T)
        y = _out_project(ma, mb, x, w_out_bf, ln_g, ln_b, tm=B * T, alpha=alpha)
    else:
        qa, kt, va, sga, qh, fh, ih, gh, k_out, v_out = _project(
            x, w_in_bf, nb=1, tm=min(PROMPT_ROWS, T), k_transposed=True)
        ma = _sb_prompt(qa, kt, va, sga)
        mb, s_out = _hgrn(qh, fh, ih, gh, lb_logits, norm_g, None, layer, C=min(PROMPT_CHUNK, T))
        y = _out_project(ma, mb, x, w_out_bf, ln_g, ln_b, tm=min(PROMPT_ROWS, T), alpha=alpha)
    return y, k_out, v_out, s_out


def kernel(x_prompt, x_sample, cache_k, cache_v, state_s, w_in, w_out, lb_logits, hgrn_norm_g,
           ln_g, ln_b):
    depth = w_in.shape[0]
    alpha = (2 * depth) ** 0.25
    yp, ys = x_prompt, x_sample
    per_layer = []
    for l in range(depth):
        w_in_bf = w_in[l].astype(BF16)
        w_out_bf = w_out[l].astype(BF16)
        common = (l, w_in_bf, w_out_bf, lb_logits, hgrn_norm_g[l], ln_g[l], ln_b[l], alpha)
        yp, kp, vp, sp = _layer(yp, None, None, None, *common)
        ys, kn, vn, sn = _layer(ys, cache_k, cache_v, state_s, *common)
        per_layer.append((kp, vp, sp, kn, vn, sn))
    stack = lambda i: (per_layer[0][i] if depth == 1
                       else jnp.concatenate([p[i] for p in per_layer], axis=0))
    return (yp, ys, stack(0), stack(1), stack(2), stack(3), stack(4), stack(5))
```

```python
import functools
import math

import numpy as np
import jax
import jax.numpy as jnp
from jax import lax
from jax.experimental import pallas as pl
from jax.experimental.pallas import tpu as pltpu

F32 = jnp.float32
BF16 = jnp.bfloat16

D_MODEL = 1024
GROUP_W = 512
N_GROUPS = 8
SB_HEADS = 8
SB_HEAD_DIM = 64
HG_HEADS = 4
HG_HEAD_DIM = 128
LN_EPS = 1e-5
RMS_EPS = 1e-6

KEY_TILE = 256
HEADS_PER_STEP = 2
PROMPT_HEADS_PER_STEP = 4
LOG2E = 1.4426950408889634
VMEM_LIMIT = 56 * 1024 * 1024


def _sigmoid(x):
    return 1.0 / (1.0 + jnp.exp(-x))


def _dot(a, b):
    return jnp.dot(a, b, preferred_element_type=F32)


def _dot_nt(a, b):
    return lax.dot_general(a, b, (((1,), (1,)), ((), ())), preferred_element_type=F32)


def _dot_tn(a, b):
    return lax.dot_general(a, b, (((0,), (0,)), ((), ())), preferred_element_type=F32)


def _split_bf16(x, parts):
    out = []
    for _ in range(parts - 1):
        p = x.astype(BF16)
        out.append(p)
        x = x - p.astype(F32)
    out.append(x.astype(BF16))
    return out


def _proj_kernel(x_ref, w_ref, qa_ref, k_ref, va_ref, sga_ref, qh_ref, fh_ref, ih_ref, gh_ref,
                 ko_ref, vo_ref, *, nb, tm, k_transposed):
    m = nb * tm
    x = x_ref[...].reshape(m, D_MODEL).astype(BF16)

    def col(c):
        return _dot(x, w_ref[:, c * GROUP_W:(c + 1) * GROUP_W])

    def put(ref, val):
        ref[...] = val.reshape(nb, tm, GROUP_W).astype(ref.dtype)

    def put_heads(ref, val):
        for b in range(nb):
            for h in range(SB_HEADS):
                ref[0, b, h] = val[b * tm:(b + 1) * tm, h * SB_HEAD_DIM:(h + 1) * SB_HEAD_DIM]

    put(qa_ref, col(0) * (SB_HEAD_DIM ** -0.5 * LOG2E))
    k = col(1)
    put_heads(ko_ref, k)
    if k_transposed:
        kt = k.T.astype(BF16)
        for j in range(m // KEY_TILE):
            k_ref[0, j] = kt[:, j * KEY_TILE:(j + 1) * KEY_TILE]
    else:
        put(k_ref, k)
    v = col(2)
    put_heads(vo_ref, v)
    put(va_ref, v)
    ga = col(3)
    put(sga_ref, ga * _sigmoid(ga))
    put(qh_ref, col(4))
    put(fh_ref, col(5))
    put(ih_ref, col(6))
    put(gh_ref, col(7))


def _project(x, w_bf, *, nb, tm, k_transposed):
    B, T, _ = x.shape
    grid = (B // nb, T // tm)
    act = lambda dt: jax.ShapeDtypeStruct((B, T, GROUP_W), dt)
    act_spec = pl.BlockSpec((nb, tm, GROUP_W), lambda b, t: (b, t, 0))
    if k_transposed:
        assert nb == 1 and tm % KEY_TILE == 0
        k_shape = jax.ShapeDtypeStruct((B, T // KEY_TILE, GROUP_W, KEY_TILE), BF16)
        k_spec = pl.BlockSpec((1, tm // KEY_TILE, GROUP_W, KEY_TILE), lambda b, t: (b, t, 0, 0))
    else:
        k_shape, k_spec = act(BF16), act_spec
    kv_shape = jax.ShapeDtypeStruct((1, B, SB_HEADS, T, SB_HEAD_DIM), F32)
    kv_spec = pl.BlockSpec((1, nb, SB_HEADS, tm, SB_HEAD_DIM), lambda b, t: (0, b, 0, t, 0))
    return pl.pallas_call(
        functools.partial(_proj_kernel, nb=nb, tm=tm, k_transposed=k_transposed),
        grid=grid,
        in_specs=[pl.BlockSpec((nb, tm, D_MODEL), lambda b, t: (b, t, 0)),
                  pl.BlockSpec((D_MODEL, N_GROUPS * GROUP_W), lambda b, t: (0, 0))],
        out_specs=[act_spec, k_spec, act_spec, act_spec, act_spec, act_spec, act_spec, act_spec,
                   kv_spec, kv_spec],
        out_shape=[act(BF16), k_shape, act(BF16), act(BF16), act(BF16), act(F32), act(BF16),
                   act(BF16), kv_shape, kv_shape],
        compiler_params=pltpu.CompilerParams(
            dimension_semantics=("arbitrary", "arbitrary"), vmem_limit_bytes=VMEM_LIMIT),
    )(x, w_bf)


def _suffix_matrix(n):
    r = lax.broadcasted_iota(jnp.int32, (n, n), 0)
    c = lax.broadcasted_iota(jnp.int32, (n, n), 1)
    return jnp.where(r >= c, 1.0, 0.0).astype(BF16)


def _sb_tiles(zs, vs, suffix, carry, causal):
    drops = []
    for z in zs:
        drop = jnp.maximum(z, 0.0) + jnp.log2(1.0 + jnp.exp2(-jnp.abs(z)))
        if causal is not None:
            drop = jnp.where(causal, drop, 0.0)
        drops.append(drop.astype(BF16))
    incls = [_dot(drop, suffix) for drop in drops]
    ws = []
    for z, incl, (run, _) in zip(zs, incls, carry):
        w = jnp.exp2(z - incl - run)
        if causal is not None:
            w = jnp.where(causal, w, 0.0)
        ws.append(w.astype(BF16))
    return tuple((run + incl[:, 0:1], acc + _dot(w, v))
                 for w, v, incl, (run, acc) in zip(ws, vs, incls, carry))


def _sb_prompt_kernel(q_ref, kt_ref, v_ref, g_ref, o_ref, *, heads):
    tq = q_ref.shape[1]
    i = pl.program_id(2)
    suffix = _suffix_matrix(KEY_TILE)
    r = lax.broadcasted_iota(jnp.int32, (tq, KEY_TILE), 0)
    c = lax.broadcasted_iota(jnp.int32, (tq, KEY_TILE), 1)
    causal = c < r

    head_lanes = [slice(hh * SB_HEAD_DIM, (hh + 1) * SB_HEAD_DIM) for hh in range(heads)]

    def tiles(j, carry, mask):
        zs = [_dot(q_ref[0, :, lanes], kt_ref[0, j, lanes, :]) for lanes in head_lanes]
        vs = [v_ref[0, j, :, lanes] for lanes in head_lanes]
        return _sb_tiles(zs, vs, suffix, carry, mask)

    zero = (jnp.zeros((tq, 1), F32), jnp.zeros((tq, SB_HEAD_DIM), F32))
    carry = tiles(i, (zero,) * heads, causal)
    carry = lax.fori_loop(0, i, lambda n, cr: tiles(i - 1 - n, cr, None), carry)
    o = jnp.concatenate([acc for _, acc in carry], axis=1) * g_ref[0].astype(F32)
    o_ref[0] = o.astype(o_ref.dtype)


def _sb_prompt(qa, kt, va, sga, *, heads):
    B, T, _ = qa.shape
    tq = KEY_TILE
    nkt = T // KEY_TILE
    lanes = heads * SB_HEAD_DIM
    v4 = va.reshape(B, nkt, KEY_TILE, GROUP_W)
    row_spec = pl.BlockSpec((1, tq, lanes), lambda b, h, i: (b, i, h))
    return pl.pallas_call(
        functools.partial(_sb_prompt_kernel, heads=heads),
        grid=(B, GROUP_W // lanes, T // tq),
        in_specs=[row_spec,
                  pl.BlockSpec((1, nkt, lanes, KEY_TILE), lambda b, h, i: (b, 0, h, 0)),
                  pl.BlockSpec((1, nkt, KEY_TILE, lanes), lambda b, h, i: (b, 0, 0, h)),
                  row_spec],
        out_specs=row_spec,
        out_shape=jax.ShapeDtypeStruct((B, T, GROUP_W), BF16),
        compiler_params=pltpu.CompilerParams(
            dimension_semantics=("arbitrary", "arbitrary", "arbitrary"),
            vmem_limit_bytes=VMEM_LIMIT),
    )(qa, kt, v4, sga)


def _sb_decode_kernel(q_ref, kn_ref, vn_ref, ck_ref, cv_ref, g_ref, o_ref):
    tq = q_ref.shape[1]
    past = ck_ref.shape[3]
    suffix = _suffix_matrix(KEY_TILE)
    suffix_new = _suffix_matrix(tq)
    r = lax.broadcasted_iota(jnp.int32, (tq, tq), 0)
    c = lax.broadcasted_iota(jnp.int32, (tq, tq), 1)
    causal = c < r
    heads = range(HEADS_PER_STEP)
    head_lanes = [slice(hh * SB_HEAD_DIM, (hh + 1) * SB_HEAD_DIM) for hh in heads]
    qs = [q_ref[0, :, lanes] for lanes in head_lanes]
    zero = (jnp.zeros((tq, 1), F32), jnp.zeros((tq, SB_HEAD_DIM), F32))
    carry = _sb_tiles([_dot_nt(q, kn_ref[0, :, lanes]) for q, lanes in zip(qs, head_lanes)],
                      [vn_ref[0, :, lanes] for lanes in head_lanes], suffix_new,
                      (zero,) * HEADS_PER_STEP, causal)
    for j in reversed(range(past // KEY_TILE)):
        rows = slice(j * KEY_TILE, (j + 1) * KEY_TILE)
        zs = [_dot_nt(q, ck_ref[0, 0, hh, rows, :].astype(BF16)) for q, hh in zip(qs, heads)]
        vs = [cv_ref[0, 0, hh, rows, :].astype(BF16) for hh in heads]
        carry = _sb_tiles(zs, vs, suffix, carry, None)
    o = jnp.concatenate([acc for _, acc in carry], axis=1) * g_ref[0].astype(F32)
    o_ref[0] = o.astype(o_ref.dtype)


def _sb_decode(qa, kn, vn, cache_k, cache_v, layer, sga):
    B, T, _ = qa.shape
    past = cache_k.shape[3]
    assert past % KEY_TILE == 0
    lanes = HEADS_PER_STEP * SB_HEAD_DIM
    row_spec = pl.BlockSpec((1, T, lanes), lambda b, h: (b, 0, h))
    cache_spec = pl.BlockSpec((1, 1, HEADS_PER_STEP, past, SB_HEAD_DIM),
                              lambda b, h: (layer, b, h, 0, 0))
    return pl.pallas_call(
        _sb_decode_kernel,
        grid=(B, GROUP_W // lanes),
        in_specs=[row_spec, row_spec, row_spec, cache_spec, cache_spec, row_spec],
        out_specs=row_spec,
        out_shape=jax.ShapeDtypeStruct((B, T, GROUP_W), BF16),
        compiler_params=pltpu.CompilerParams(
            dimension_semantics=("arbitrary", "arbitrary"), vmem_limit_bytes=VMEM_LIMIT),
    )(qa, kn, vn, cache_k, cache_v, sga)


def _hgrn_tables(C):
    nl = int(math.log2(C))
    assert 1 << nl == C
    t = np.arange(C)[:, None]
    j = np.arange(C)[None, :]
    blocks = []
    for l in range(nl):
        half = 1 << l
        mid = (t >> (l + 1) << (l + 1)) + half
        right = (t & half) != 0
        blocks.append(np.where(right, (j >= mid) & (j <= t), (j > t) & (j < mid)))
    blocks.append(j <= t)
    blocks.append(j > t)
    prefix = np.concatenate(blocks, axis=0).astype(np.float32)
    x = t ^ j
    msb = np.floor(np.log2(np.maximum(x, 1))).astype(np.int32)
    level_of = np.where(t > j, msb, np.where(t == j, nl, -1)).astype(np.int32)
    return jnp.asarray(prefix, BF16), jnp.asarray(level_of), nl


def _hgrn_kernel(*refs, C, nl, layer, has_s0):
    if has_s0:
        (qh_ref, fh_ref, ih_ref, gh_ref, pm_ref, lv_ref, lbl_ref, ng_ref, s0_ref,
         o_ref, so_ref, st_ref) = refs
    else:
        (qh_ref, fh_ref, ih_ref, gh_ref, pm_ref, lv_ref, lbl_ref, ng_ref,
         o_ref, so_ref, st_ref) = refs
    t = pl.program_id(1)

    @pl.when(t == 0)
    def _():
        if has_s0:
            for h in range(HG_HEADS):
                st_ref[h] = s0_ref[0, 0, h].T
        else:
            st_ref[...] = jnp.zeros_like(st_ref)

    logits = lbl_ref[...]
    e = jnp.exp(logits - jnp.max(logits, axis=0, keepdims=True))
    lb = jnp.sum(e[:layer + 1], axis=0, keepdims=True) / jnp.sum(e, axis=0, keepdims=True)

    f = lb + (1.0 - lb) * _sigmoid(fh_ref[0])
    g = jnp.log(f)
    kin = 1.0 - f
    qh = qh_ref[0].astype(F32)
    qs = qh * _sigmoid(qh)
    g3 = jnp.concatenate(_split_bf16(g, 3), axis=1)

    def prefix(l):
        p = _dot(pm_ref[l * C:(l + 1) * C, :], g3)
        return p[:, :GROUP_W] + p[:, GROUP_W:2 * GROUP_W] + p[:, 2 * GROUP_W:]

    def head(a, h):
        return a[:, h * HG_HEAD_DIM:(h + 1) * HG_HEAD_DIM]

    level_of = lv_ref[...]
    qs_b = qs.astype(BF16)
    kin_b = kin.astype(BF16)
    scores = [jnp.where(level_of == nl, _dot_nt(head(qs_b, h), head(kin_b, h)), 0.0)
              for h in range(HG_HEADS)]
    for l in range(nl):
        e_l = jnp.exp(prefix(l))
        q_l = (qs * e_l).astype(BF16)
        k_l = (kin * e_l).astype(BF16)
        for h in range(HG_HEADS):
            scores[h] = jnp.where(level_of == l, _dot_nt(head(q_l, h), head(k_l, h)), scores[h])

    b = prefix(nl)
    q_in = (qs * jnp.exp(b)).astype(BF16)
    k_out = (kin * jnp.exp(prefix(nl + 1))).astype(BF16)
    chunk_decay = jnp.exp(b[C - 1:C, :])
    i_b = ih_ref[0]
    gh = gh_ref[0].astype(F32)
    gate = ng_ref[...] * (gh * _sigmoid(gh))
    outs = []
    for h in range(HG_HEADS):
        st = st_ref[h]
        o = _dot(scores[h].astype(BF16), head(i_b, h)) + _dot_nt(head(q_in, h), st.astype(BF16))
        st_ref[h] = st * head(chunk_decay, h) + _dot_tn(head(i_b, h), head(k_out, h))
        outs.append(o * lax.rsqrt(jnp.mean(o * o, axis=-1, keepdims=True) + RMS_EPS))
    o_ref[0] = (jnp.concatenate(outs, axis=1) * gate).astype(o_ref.dtype)

    @pl.when(t == pl.num_programs(1) - 1)
    def _():
        for h in range(HG_HEADS):
            so_ref[0, 0, h] = st_ref[h].T


def _hgrn(qh, fh, ih, gh, lb_logits, norm_g, s0, layer, *, C):
    B, T, _ = qh.shape
    prefix, level_of, nl = _hgrn_tables(C)
    has_s0 = s0 is not None
    row_spec = pl.BlockSpec((1, C, GROUP_W), lambda b, t: (b, t, 0))
    const = lambda shape: pl.BlockSpec(shape, lambda b, t: (0,) * len(shape))
    state_shape = (1, 1, HG_HEADS, HG_HEAD_DIM, HG_HEAD_DIM)
    in_specs = [row_spec, row_spec, row_spec, row_spec, const(prefix.shape), const(level_of.shape),
                const(lb_logits.shape), const((1, GROUP_W))]
    args = [qh, fh, ih, gh, prefix, level_of, lb_logits.astype(F32),
            norm_g.reshape(1, GROUP_W).astype(F32)]
    if has_s0:
        in_specs.append(pl.BlockSpec(state_shape, lambda b, t: (layer, b, 0, 0, 0)))
        args.append(s0)
    return pl.pallas_call(
        functools.partial(_hgrn_kernel, C=C, nl=nl, layer=layer, has_s0=has_s0),
        grid=(B, T // C),
        in_specs=in_specs,
        out_specs=[row_spec, pl.BlockSpec(state_shape, lambda b, t: (0, b, 0, 0, 0))],
        out_shape=[jax.ShapeDtypeStruct((B, T, GROUP_W), BF16),
                   jax.ShapeDtypeStruct((1, B, HG_HEADS, HG_HEAD_DIM, HG_HEAD_DIM), F32)],
        scratch_shapes=[pltpu.VMEM((HG_HEADS, HG_HEAD_DIM, HG_HEAD_DIM), F32)],
        compiler_params=pltpu.CompilerParams(
            dimension_semantics=("arbitrary", "arbitrary"), vmem_limit_bytes=VMEM_LIMIT),
    )(*args)


def _out_kernel(ma_ref, mb_ref, x_ref, w_ref, g_ref, b_ref, y_ref, *, alpha):
    out = _dot(ma_ref[...], w_ref[:GROUP_W, :]) + _dot(mb_ref[...], w_ref[GROUP_W:, :])
    hid = alpha * x_ref[...] + out
    mu = jnp.mean(hid, axis=-1, keepdims=True)
    cen = hid - mu
    var = jnp.mean(cen * cen, axis=-1, keepdims=True)
    y_ref[...] = cen * lax.rsqrt(var + LN_EPS) * g_ref[...] + b_ref[...]


def _out_project(ma, mb, x, w_bf, ln_g, ln_b, *, tm, alpha):
    B, T, _ = x.shape
    n = B * T
    mix_spec = pl.BlockSpec((tm, GROUP_W), lambda i: (i, 0))
    x_spec = pl.BlockSpec((tm, D_MODEL), lambda i: (i, 0))
    const = lambda shape: pl.BlockSpec(shape, lambda i: (0, 0))
    y = pl.pallas_call(
        functools.partial(_out_kernel, alpha=alpha),
        grid=(n // tm,),
        in_specs=[mix_spec, mix_spec, x_spec, const((2 * GROUP_W, D_MODEL)), const((1, D_MODEL)),
                  const((1, D_MODEL))],
        out_specs=x_spec,
        out_shape=jax.ShapeDtypeStruct((n, D_MODEL), F32),
        compiler_params=pltpu.CompilerParams(
            dimension_semantics=("arbitrary",), vmem_limit_bytes=VMEM_LIMIT),
    )(ma.reshape(n, GROUP_W), mb.reshape(n, GROUP_W), x.reshape(n, D_MODEL), w_bf,
      ln_g.reshape(1, D_MODEL).astype(F32), ln_b.reshape(1, D_MODEL).astype(F32))
    return y.reshape(B, T, D_MODEL)


PROMPT_ROWS = 512
PROMPT_CHUNK = 256


def _layer(x, cache_k, cache_v, state_s, layer, w_in_bf, w_out_bf, lb_logits, norm_g, ln_g, ln_b,
           alpha):
    B, T, _ = x.shape
    decode = cache_k is not None
    if decode:
        qa, k, va, sga, qh, fh, ih, gh, k_out, v_out = _project(
            x, w_in_bf, nb=B, tm=T, k_transposed=False)
        ma = _sb_decode(qa, k, va, cache_k, cache_v, layer, sga)
        mb, s_out = _hgrn(qh, fh, ih, gh, lb_logits, norm_g, state_s, layer, C=T)
        y = _out_project(ma, mb, x, w_out_bf, ln_g, ln_b, tm=B * T, alpha=alpha)
    else:
        qa, kt, va, sga, qh, fh, ih, gh, k_out, v_out = _project(
            x, w_in_bf, nb=1, tm=min(PROMPT_ROWS, T), k_transposed=True)
        ma = _sb_prompt(qa, kt, va, sga, heads=PROMPT_HEADS_PER_STEP)
        mb, s_out = _hgrn(qh, fh, ih, gh, lb_logits, norm_g, None, layer, C=min(PROMPT_CHUNK, T))
        y = _out_project(ma, mb, x, w_out_bf, ln_g, ln_b, tm=min(PROMPT_ROWS, T), alpha=alpha)
    return y, k_out, v_out, s_out


def kernel(x_prompt, x_sample, cache_k, cache_v, state_s, w_in, w_out, lb_logits, hgrn_norm_g,
           ln_g, ln_b):
    depth = w_in.shape[0]
    alpha = (2 * depth) ** 0.25
    yp, ys = x_prompt, x_sample
    per_layer = []
    for l in range(depth):
        w_in_bf = w_in[l].astype(BF16)
        w_out_bf = w_out[l].astype(BF16)
        common = (l, w_in_bf, w_out_bf, lb_logits, hgrn_norm_g[l], ln_g[l], ln_b[l], alpha)
        yp, kp, vp, sp = _layer(yp, None, None, None, *common)
        ys, kn, vn, sn = _layer(ys, cache_k, cache_v, state_s, *common)
        per_layer.append((kp, vp, sp, kn, vn, sn))
    stack = lambda i: (per_layer[0][i] if depth == 1
                       else jnp.concatenate([p[i] for p in per_layer], axis=0))
    return (yp, ys, stack(0), stack(1), stack(2), stack(3), stack(4), stack(5))
```

```python
import functools
import math

import numpy as np
import jax
import jax.numpy as jnp
from jax import lax
from jax.experimental import pallas as pl
from jax.experimental.pallas import tpu as pltpu

F32 = jnp.float32
BF16 = jnp.bfloat16

D_MODEL = 1024
GROUP_W = 512
N_GROUPS = 8
SB_HEADS = 8
SB_HEAD_DIM = 64
HG_HEADS = 4
HG_HEAD_DIM = 128
LN_EPS = 1e-5
RMS_EPS = 1e-6

SUBLANES = 8
KEY_TILE = 256
HEADS_PER_STEP = 2
PROMPT_HEADS_PER_STEP = 4
LOG2E = 1.4426950408889634
VMEM_LIMIT = 56 * 1024 * 1024


def _sigmoid(x):
    return 1.0 / (1.0 + jnp.exp(-x))


def _dot(a, b):
    return jnp.dot(a, b, preferred_element_type=F32)


def _dot_nt(a, b):
    return lax.dot_general(a, b, (((1,), (1,)), ((), ())), preferred_element_type=F32)


def _dot_tn(a, b):
    return lax.dot_general(a, b, (((0,), (0,)), ((), ())), preferred_element_type=F32)


def _split_bf16(x, parts):
    out = []
    for _ in range(parts - 1):
        p = x.astype(BF16)
        out.append(p)
        x = x - p.astype(F32)
    out.append(x.astype(BF16))
    return out


def _proj_kernel(x_ref, w_ref, qa_ref, k_ref, va_ref, sga_ref, qh_ref, fh_ref, ih_ref, gh_ref,
                 ko_ref, vo_ref, *, nb, tm, k_transposed):
    m = nb * tm
    x = x_ref[...].reshape(m, D_MODEL).astype(BF16)

    def col(c):
        return _dot(x, w_ref[:, c * GROUP_W:(c + 1) * GROUP_W])

    def put(ref, val):
        ref[...] = val.reshape(nb, tm, GROUP_W).astype(ref.dtype)

    def put_heads(ref, val):
        if k_transposed:
            val_t = val.T
            for h in range(SB_HEADS):
                ref[0, 0, h] = val_t[h * SB_HEAD_DIM:(h + 1) * SB_HEAD_DIM, :]
            return val_t
        for b in range(nb):
            for h in range(SB_HEADS):
                ref[0, b, h] = val[b * tm:(b + 1) * tm, h * SB_HEAD_DIM:(h + 1) * SB_HEAD_DIM]

    put(qa_ref, col(0) * (SB_HEAD_DIM ** -0.5 * LOG2E))
    k = col(1)
    kt = put_heads(ko_ref, k)
    if k_transposed:
        kt = kt.astype(BF16)
        for j in range(m // KEY_TILE):
            k_ref[0, j] = kt[:, j * KEY_TILE:(j + 1) * KEY_TILE]
    else:
        put(k_ref, k)
    v = col(2)
    put_heads(vo_ref, v)
    put(va_ref, v)
    ga = col(3)
    put(sga_ref, ga * _sigmoid(ga))
    put(qh_ref, col(4))
    put(fh_ref, col(5))
    put(ih_ref, col(6))
    put(gh_ref, col(7))


def _project(x, w_bf, *, nb, tm, k_transposed):
    B, T, _ = x.shape
    grid = (B // nb, T // tm)
    act = lambda dt: jax.ShapeDtypeStruct((B, T, GROUP_W), dt)
    act_spec = pl.BlockSpec((nb, tm, GROUP_W), lambda b, t: (b, t, 0))
    if k_transposed:
        assert nb == 1 and tm % KEY_TILE == 0
        k_shape = jax.ShapeDtypeStruct((B, T // KEY_TILE, GROUP_W, KEY_TILE), BF16)
        k_spec = pl.BlockSpec((1, tm // KEY_TILE, GROUP_W, KEY_TILE), lambda b, t: (b, t, 0, 0))
        kv_shape = jax.ShapeDtypeStruct((1, B, SB_HEADS, SB_HEAD_DIM, T), F32)
        kv_spec = pl.BlockSpec((1, 1, SB_HEADS, SB_HEAD_DIM, tm), lambda b, t: (0, b, 0, 0, t))
    else:
        k_shape, k_spec = act(BF16), act_spec
        kv_shape = jax.ShapeDtypeStruct((1, B, SB_HEADS, T, SB_HEAD_DIM), F32)
        kv_spec = pl.BlockSpec((1, nb, SB_HEADS, tm, SB_HEAD_DIM), lambda b, t: (0, b, 0, t, 0))
    outs = pl.pallas_call(
        functools.partial(_proj_kernel, nb=nb, tm=tm, k_transposed=k_transposed),
        grid=grid,
        in_specs=[pl.BlockSpec((nb, tm, D_MODEL), lambda b, t: (b, t, 0)),
                  pl.BlockSpec((D_MODEL, N_GROUPS * GROUP_W), lambda b, t: (0, 0))],
        out_specs=[act_spec, k_spec, act_spec, act_spec, act_spec, act_spec, act_spec, act_spec,
                   kv_spec, kv_spec],
        out_shape=[act(BF16), k_shape, act(BF16), act(BF16), act(BF16), act(F32), act(BF16),
                   act(BF16), kv_shape, kv_shape],
        compiler_params=pltpu.CompilerParams(
            dimension_semantics=("arbitrary", "arbitrary"), vmem_limit_bytes=VMEM_LIMIT),
    )(x, w_bf)
    if k_transposed:
        outs = list(outs[:8]) + [jnp.swapaxes(o, 3, 4) for o in outs[8:]]
    return outs


def _suffix_matrix(n):
    r = lax.broadcasted_iota(jnp.int32, (n, n), 0)
    c = lax.broadcasted_iota(jnp.int32, (n, n), 1)
    return jnp.where(r >= c, 1.0, 0.0).astype(BF16)


def _sb_tiles(zs, vs, suffix, carry, causal, v_transposed=False):
    drops = []
    for z in zs:
        drop = jnp.maximum(z, 0.0) + jnp.log2(1.0 + jnp.exp2(-jnp.abs(z)))
        if causal is not None:
            drop = jnp.where(causal, drop, 0.0)
        drops.append(drop.astype(BF16))
    incls = [_dot(drop, suffix) for drop in drops]
    ws = []
    for z, incl, (run, _) in zip(zs, incls, carry):
        w = jnp.exp2(z - incl - run)
        if causal is not None:
            w = jnp.where(causal, w, 0.0)
        ws.append(w.astype(BF16))
    pv = _dot_nt if v_transposed else _dot
    return tuple((run + incl[:, 0:1], acc + pv(w, v))
                 for w, v, incl, (run, acc) in zip(ws, vs, incls, carry))


def _sb_prompt_kernel(q_ref, kt_ref, v_ref, g_ref, o_ref, *, heads):
    tq = q_ref.shape[1]
    i = pl.program_id(2)
    suffix = _suffix_matrix(KEY_TILE)
    r = lax.broadcasted_iota(jnp.int32, (tq, KEY_TILE), 0)
    c = lax.broadcasted_iota(jnp.int32, (tq, KEY_TILE), 1)
    causal = c < r

    head_lanes = [slice(hh * SB_HEAD_DIM, (hh + 1) * SB_HEAD_DIM) for hh in range(heads)]

    def tiles(j, carry, mask):
        zs = [_dot(q_ref[0, :, lanes], kt_ref[0, j, lanes, :]) for lanes in head_lanes]
        vs = [v_ref[0, j, :, lanes] for lanes in head_lanes]
        return _sb_tiles(zs, vs, suffix, carry, mask)

    zero = (jnp.zeros((tq, 1), F32), jnp.zeros((tq, SB_HEAD_DIM), F32))
    carry = tiles(i, (zero,) * heads, causal)
    carry = lax.fori_loop(0, i, lambda n, cr: tiles(i - 1 - n, cr, None), carry)
    o = jnp.concatenate([acc for _, acc in carry], axis=1) * g_ref[0].astype(F32)
    o_ref[0] = o.astype(o_ref.dtype)


def _sb_prompt(qa, kt, va, sga, *, heads):
    B, T, _ = qa.shape
    tq = KEY_TILE
    nkt = T // KEY_TILE
    lanes = heads * SB_HEAD_DIM
    v4 = va.reshape(B, nkt, KEY_TILE, GROUP_W)
    row_spec = pl.BlockSpec((1, tq, lanes), lambda b, h, i: (b, i, h))
    return pl.pallas_call(
        functools.partial(_sb_prompt_kernel, heads=heads),
        grid=(B, GROUP_W // lanes, T // tq),
        in_specs=[row_spec,
                  pl.BlockSpec((1, nkt, lanes, KEY_TILE), lambda b, h, i: (b, 0, h, 0)),
                  pl.BlockSpec((1, nkt, KEY_TILE, lanes), lambda b, h, i: (b, 0, 0, h)),
                  row_spec],
        out_specs=row_spec,
        out_shape=jax.ShapeDtypeStruct((B, T, GROUP_W), BF16),
        compiler_params=pltpu.CompilerParams(
            dimension_semantics=("arbitrary", "arbitrary", "arbitrary"),
            vmem_limit_bytes=VMEM_LIMIT),
    )(qa, kt, v4, sga)


def _sb_decode_kernel(q_ref, kn_ref, vn_ref, ck_ref, cv_ref, g_ref, o_ref):
    tq = q_ref.shape[1]
    past = ck_ref.shape[4]
    suffix = _suffix_matrix(KEY_TILE)
    suffix_new = _suffix_matrix(tq)
    r = lax.broadcasted_iota(jnp.int32, (tq, tq), 0)
    c = lax.broadcasted_iota(jnp.int32, (tq, tq), 1)
    causal = c < r
    heads = range(HEADS_PER_STEP)
    head_lanes = [slice(hh * SB_HEAD_DIM, (hh + 1) * SB_HEAD_DIM) for hh in heads]
    qs = [q_ref[0, :, lanes] for lanes in head_lanes]
    zero = (jnp.zeros((tq, 1), F32), jnp.zeros((tq, SB_HEAD_DIM), F32))
    carry = _sb_tiles([_dot_nt(q, kn_ref[0, :, lanes]) for q, lanes in zip(qs, head_lanes)],
                      [vn_ref[0, :, lanes] for lanes in head_lanes], suffix_new,
                      (zero,) * HEADS_PER_STEP, causal)
    for j in reversed(range(past // KEY_TILE)):
        keys = slice(j * KEY_TILE, (j + 1) * KEY_TILE)
        zs = [_dot(q, ck_ref[0, 0, hh, :, keys].astype(BF16)) for q, hh in zip(qs, heads)]
        vs = [cv_ref[0, 0, hh, :, keys].astype(BF16) for hh in heads]
        carry = _sb_tiles(zs, vs, suffix, carry, None, v_transposed=True)
    o = jnp.concatenate([acc for _, acc in carry], axis=1) * g_ref[0].astype(F32)
    o_ref[0] = o.astype(o_ref.dtype)


def _sb_decode(qa, kn, vn, cache_k, cache_v, layer, sga):
    B, T, _ = qa.shape
    past = cache_k.shape[3]
    assert past % KEY_TILE == 0
    lanes = HEADS_PER_STEP * SB_HEAD_DIM
    row_spec = pl.BlockSpec((1, T, lanes), lambda b, h: (b, 0, h))
    cache_spec = pl.BlockSpec((1, 1, HEADS_PER_STEP, SB_HEAD_DIM, past),
                              lambda b, h: (layer, b, h, 0, 0))
    cache_k = jnp.swapaxes(cache_k, 3, 4)
    cache_v = jnp.swapaxes(cache_v, 3, 4)
    return pl.pallas_call(
        _sb_decode_kernel,
        grid=(B, GROUP_W // lanes),
        in_specs=[row_spec, row_spec, row_spec, cache_spec, cache_spec, row_spec],
        out_specs=row_spec,
        out_shape=jax.ShapeDtypeStruct((B, T, GROUP_W), BF16),
        compiler_params=pltpu.CompilerParams(
            dimension_semantics=("arbitrary", "arbitrary"), vmem_limit_bytes=VMEM_LIMIT),
    )(qa, kn, vn, cache_k, cache_v, sga)


def _hgrn_tables(C):
    nl = int(math.log2(C))
    assert 1 << nl == C
    n_small = min(nl, int(math.log2(SUBLANES)))
    t = np.arange(C)[:, None]
    j = np.arange(C)[None, :]
    blocks = []
    for l in range(n_small):
        half = 1 << l
        mid = (t >> (l + 1) << (l + 1)) + half
        right = (t & half) != 0
        blocks.append(np.where(right, (j >= mid) & (j <= t), (j > t) & (j < mid)))
    blocks.append(j <= t)
    prefix = np.concatenate(blocks, axis=0).astype(np.float32)
    x = t ^ j
    msb = np.floor(np.log2(np.maximum(x, 1))).astype(np.int32)
    level_of = np.where(t > j, msb, np.where(t == j, nl, -1)).astype(np.int32)
    return jnp.asarray(prefix, BF16), jnp.asarray(level_of), nl, n_small


def _hgrn_kernel(*refs, C, nl, n_small, layer, has_s0):
    if has_s0:
        (qh_ref, fh_ref, ih_ref, gh_ref, pm_ref, lv_ref, lbl_ref, ng_ref, s0_ref,
         o_ref, so_ref, st_ref, b_ref) = refs
    else:
        (qh_ref, fh_ref, ih_ref, gh_ref, pm_ref, lv_ref, lbl_ref, ng_ref,
         o_ref, so_ref, st_ref, b_ref) = refs
    t = pl.program_id(1)

    @pl.when(t == 0)
    def _():
        if has_s0:
            for h in range(HG_HEADS):
                st_ref[h] = s0_ref[0, 0, h].T
        else:
            st_ref[...] = jnp.zeros_like(st_ref)

    logits = lbl_ref[...]
    e = jnp.exp(logits - jnp.max(logits, axis=0, keepdims=True))
    lb = jnp.sum(e[:layer + 1], axis=0, keepdims=True) / jnp.sum(e, axis=0, keepdims=True)

    f = lb + (1.0 - lb) * _sigmoid(fh_ref[0])
    g = jnp.log(f)
    kin = 1.0 - f
    qh = qh_ref[0].astype(F32)
    qs = qh * _sigmoid(qh)
    g2 = jnp.concatenate(_split_bf16(g, 2), axis=1)
    pre = _dot(pm_ref[...], g2)
    pre = pre[:, :GROUP_W] + pre[:, GROUP_W:]
    b = pre[n_small * C:]
    b_ref[...] = b

    def level_log(l):
        if l < n_small:
            return pre[l * C:(l + 1) * C]
        half = 1 << l
        parts = []
        for lo in range(0, C, 2 * half):
            last_left = b_ref[lo + half - 1:lo + half, :]
            parts.append(-jnp.abs(b_ref[lo:lo + 2 * half, :] - last_left))
        return parts[0] if len(parts) == 1 else jnp.concatenate(parts, axis=0)

    def head(a, h):
        return a[:, h * HG_HEAD_DIM:(h + 1) * HG_HEAD_DIM]

    def operands(l):
        if l == nl:
            return qs.astype(BF16), kin.astype(BF16)
        e_l = jnp.exp(level_log(l))
        return (qs * e_l).astype(BF16), (kin * e_l).astype(BF16)

    level_of = lv_ref[...]
    order = [nl] + list(range(nl))
    scores = [0.0] * HG_HEADS
    nxt = operands(order[0])
    for idx, l in enumerate(order):
        q_l, k_l = nxt
        if idx + 1 < len(order):
            nxt = operands(order[idx + 1])
        hit = level_of == l
        for h in range(HG_HEADS):
            scores[h] = jnp.where(hit, _dot_nt(head(q_l, h), head(k_l, h)), scores[h])

    q_in = (qs * jnp.exp(b)).astype(BF16)
    b_last = b_ref[C - 1:C, :]
    k_out = (kin * jnp.exp(b_last - b)).astype(BF16)
    chunk_decay = jnp.exp(b_last)
    i_b = ih_ref[0]
    gh = gh_ref[0].astype(F32)
    gate = ng_ref[...] * (gh * _sigmoid(gh))
    outs = []
    for h in range(HG_HEADS):
        st = st_ref[h]
        o = _dot(scores[h].astype(BF16), head(i_b, h)) + _dot_nt(head(q_in, h), st.astype(BF16))
        st_ref[h] = st * head(chunk_decay, h) + _dot_tn(head(i_b, h), head(k_out, h))
        outs.append(o * lax.rsqrt(jnp.mean(o * o, axis=-1, keepdims=True) + RMS_EPS))
    o_ref[0] = (jnp.concatenate(outs, axis=1) * gate).astype(o_ref.dtype)

    @pl.when(t == pl.num_programs(1) - 1)
    def _():
        for h in range(HG_HEADS):
            so_ref[0, 0, h] = st_ref[h].T


def _hgrn(qh, fh, ih, gh, lb_logits, norm_g, s0, layer, *, C):
    B, T, _ = qh.shape
    prefix, level_of, nl, n_small = _hgrn_tables(C)
    has_s0 = s0 is not None
    row_spec = pl.BlockSpec((1, C, GROUP_W), lambda b, t: (b, t, 0))
    const = lambda shape: pl.BlockSpec(shape, lambda b, t: (0,) * len(shape))
    state_shape = (1, 1, HG_HEADS, HG_HEAD_DIM, HG_HEAD_DIM)
    in_specs = [row_spec, row_spec, row_spec, row_spec, const(prefix.shape), const(level_of.shape),
                const(lb_logits.shape), const((1, GROUP_W))]
    args = [qh, fh, ih, gh, prefix, level_of, lb_logits.astype(F32),
            norm_g.reshape(1, GROUP_W).astype(F32)]
    if has_s0:
        in_specs.append(pl.BlockSpec(state_shape, lambda b, t: (layer, b, 0, 0, 0)))
        args.append(s0)
    return pl.pallas_call(
        functools.partial(_hgrn_kernel, C=C, nl=nl, n_small=n_small, layer=layer, has_s0=has_s0),
        grid=(B, T // C),
        in_specs=in_specs,
        out_specs=[row_spec, pl.BlockSpec(state_shape, lambda b, t: (0, b, 0, 0, 0))],
        out_shape=[jax.ShapeDtypeStruct((B, T, GROUP_W), BF16),
                   jax.ShapeDtypeStruct((1, B, HG_HEADS, HG_HEAD_DIM, HG_HEAD_DIM), F32)],
        scratch_shapes=[pltpu.VMEM((HG_HEADS, HG_HEAD_DIM, HG_HEAD_DIM), F32),
                        pltpu.VMEM((C, GROUP_W), F32)],
        compiler_params=pltpu.CompilerParams(
            dimension_semantics=("arbitrary", "arbitrary"), vmem_limit_bytes=VMEM_LIMIT),
    )(*args)


def _out_kernel(ma_ref, mb_ref, x_ref, w_ref, g_ref, b_ref, y_ref, *, alpha):
    out = _dot(ma_ref[...], w_ref[:GROUP_W, :]) + _dot(mb_ref[...], w_ref[GROUP_W:, :])
    hid = alpha * x_ref[...] + out
    mu = jnp.mean(hid, axis=-1, keepdims=True)
    cen = hid - mu
    var = jnp.mean(cen * cen, axis=-1, keepdims=True)
    y_ref[...] = cen * lax.rsqrt(var + LN_EPS) * g_ref[...] + b_ref[...]


def _out_project(ma, mb, x, w_bf, ln_g, ln_b, *, tm, alpha):
    B, T, _ = x.shape
    n = B * T
    mix_spec = pl.BlockSpec((tm, GROUP_W), lambda i: (i, 0))
    x_spec = pl.BlockSpec((tm, D_MODEL), lambda i: (i, 0))
    const = lambda shape: pl.BlockSpec(shape, lambda i: (0, 0))
    y = pl.pallas_call(
        functools.partial(_out_kernel, alpha=alpha),
        grid=(n // tm,),
        in_specs=[mix_spec, mix_spec, x_spec, const((2 * GROUP_W, D_MODEL)), const((1, D_MODEL)),
                  const((1, D_MODEL))],
        out_specs=x_spec,
        out_shape=jax.ShapeDtypeStruct((n, D_MODEL), F32),
        compiler_params=pltpu.CompilerParams(
            dimension_semantics=("arbitrary",), vmem_limit_bytes=VMEM_LIMIT),
    )(ma.reshape(n, GROUP_W), mb.reshape(n, GROUP_W), x.reshape(n, D_MODEL), w_bf,
      ln_g.reshape(1, D_MODEL).astype(F32), ln_b.reshape(1, D_MODEL).astype(F32))
    return y.reshape(B, T, D_MODEL)


PROMPT_ROWS = 512
PROMPT_CHUNK = 256


def _layer(x, cache_k, cache_v, state_s, layer, w_in_bf, w_out_bf, lb_logits, norm_g, ln_g, ln_b,
           alpha):
    B, T, _ = x.shape
    decode = cache_k is not None
    if decode:
        qa, k, va, sga, qh, fh, ih, gh, k_out, v_out = _project(
            x, w_in_bf, nb=B, tm=T, k_transposed=False)
        ma = _sb_decode(qa, k, va, cache_k, cache_v, layer, sga)
        mb, s_out = _hgrn(qh, fh, ih, gh, lb_logits, norm_g, state_s, layer, C=T)
        y = _out_project(ma, mb, x, w_out_bf, ln_g, ln_b, tm=B * T, alpha=alpha)
    else:
        qa, kt, va, sga, qh, fh, ih, gh, k_out, v_out = _project(
            x, w_in_bf, nb=1, tm=min(PROMPT_ROWS, T), k_transposed=True)
        ma = _sb_prompt(qa, kt, va, sga, heads=PROMPT_HEADS_PER_STEP)
        mb, s_out = _hgrn(qh, fh, ih, gh, lb_logits, norm_g, None, layer, C=min(PROMPT_CHUNK, T))
        y = _out_project(ma, mb, x, w_out_bf, ln_g, ln_b, tm=min(PROMPT_ROWS, T), alpha=alpha)
    return y, k_out, v_out, s_out


def kernel(x_prompt, x_sample, cache_k, cache_v, state_s, w_in, w_out, lb_logits, hgrn_norm_g,
           ln_g, ln_b):
    depth = w_in.shape[0]
    alpha = (2 * depth) ** 0.25
    yp, ys = x_prompt, x_sample
    per_layer = []
    for l in range(depth):
        w_in_bf = w_in[l].astype(BF16)
        w_out_bf = w_out[l].astype(BF16)
        common = (l, w_in_bf, w_out_bf, lb_logits, hgrn_norm_g[l], ln_g[l], ln_b[l], alpha)
        yp, kp, vp, sp = _layer(yp, None, None, None, *common)
        ys, kn, vn, sn = _layer(ys, cache_k, cache_v, state_s, *common)
        per_layer.append((kp, vp, sp, kn, vn, sn))
    stack = lambda i: (per_layer[0][i] if depth == 1
                       else jnp.concatenate([p[i] for p in per_layer], axis=0))
    return (yp, ys, stack(0), stack(1), stack(2), stack(3), stack(4), stack(5))
```

```python
import functools
import math

import numpy as np
import jax
import jax.numpy as jnp
from jax import lax
from jax.experimental import pallas as pl
from jax.experimental.pallas import tpu as pltpu

F32 = jnp.float32
BF16 = jnp.bfloat16

D_MODEL = 1024
GROUP_W = 512
N_GROUPS = 8
SB_HEADS = 8
SB_HEAD_DIM = 64
HG_HEADS = 4
HG_HEAD_DIM = 128
LN_EPS = 1e-5
RMS_EPS = 1e-6

SUBLANES = 8
KEY_TILE = 256
HEADS_PER_STEP = 2
PROMPT_HEADS_PER_STEP = 4
LOG2E = 1.4426950408889634
VMEM_LIMIT = 56 * 1024 * 1024


def _sigmoid(x):
    return 1.0 / (1.0 + jnp.exp(-x))


def _dot(a, b):
    return jnp.dot(a, b, preferred_element_type=F32)


def _dot_nt(a, b):
    return lax.dot_general(a, b, (((1,), (1,)), ((), ())), preferred_element_type=F32)


def _dot_tn(a, b):
    return lax.dot_general(a, b, (((0,), (0,)), ((), ())), preferred_element_type=F32)


def _split_bf16(x, parts):
    out = []
    for _ in range(parts - 1):
        p = x.astype(BF16)
        out.append(p)
        x = x - p.astype(F32)
    out.append(x.astype(BF16))
    return out


def _proj_kernel(x_ref, w_ref, qa_ref, k_ref, va_ref, sga_ref, qh_ref, fh_ref, ih_ref, gh_ref,
                 ko_ref, vo_ref, *, nb, tm, k_transposed):
    m = nb * tm
    x = x_ref[...].reshape(m, D_MODEL).astype(BF16)

    def col(c):
        return _dot(x, w_ref[:, c * GROUP_W:(c + 1) * GROUP_W])

    def put(ref, val):
        ref[...] = val.reshape(nb, tm, GROUP_W).astype(ref.dtype)

    def put_heads(ref, val):
        if k_transposed:
            val_t = val.T
            for h in range(SB_HEADS):
                ref[0, 0, h] = val_t[h * SB_HEAD_DIM:(h + 1) * SB_HEAD_DIM, :]
            return val_t
        for b in range(nb):
            for h in range(SB_HEADS):
                ref[0, b, h] = val[b * tm:(b + 1) * tm, h * SB_HEAD_DIM:(h + 1) * SB_HEAD_DIM]

    put(qa_ref, col(0) * (SB_HEAD_DIM ** -0.5 * LOG2E))
    k = col(1)
    kt = put_heads(ko_ref, k)
    if k_transposed:
        kt = kt.astype(BF16)
        for j in range(m // KEY_TILE):
            k_ref[0, j] = kt[:, j * KEY_TILE:(j + 1) * KEY_TILE]
    else:
        put(k_ref, k)
    v = col(2)
    put_heads(vo_ref, v)
    put(va_ref, v)
    ga = col(3)
    put(sga_ref, ga * _sigmoid(ga))
    put(qh_ref, col(4))
    put(fh_ref, col(5))
    put(ih_ref, col(6))
    put(gh_ref, col(7))


def _project(x, w_bf, *, nb, tm, k_transposed):
    B, T, _ = x.shape
    grid = (B // nb, T // tm)
    act = lambda dt: jax.ShapeDtypeStruct((B, T, GROUP_W), dt)
    act_spec = pl.BlockSpec((nb, tm, GROUP_W), lambda b, t: (b, t, 0))
    if k_transposed:
        assert nb == 1 and tm % KEY_TILE == 0
        k_shape = jax.ShapeDtypeStruct((B, T // KEY_TILE, GROUP_W, KEY_TILE), BF16)
        k_spec = pl.BlockSpec((1, tm // KEY_TILE, GROUP_W, KEY_TILE), lambda b, t: (b, t, 0, 0))
        kv_shape = jax.ShapeDtypeStruct((1, B, SB_HEADS, SB_HEAD_DIM, T), F32)
        kv_spec = pl.BlockSpec((1, 1, SB_HEADS, SB_HEAD_DIM, tm), lambda b, t: (0, b, 0, 0, t))
    else:
        k_shape, k_spec = act(BF16), act_spec
        kv_shape = jax.ShapeDtypeStruct((1, B, SB_HEADS, T, SB_HEAD_DIM), F32)
        kv_spec = pl.BlockSpec((1, nb, SB_HEADS, tm, SB_HEAD_DIM), lambda b, t: (0, b, 0, t, 0))
    outs = pl.pallas_call(
        functools.partial(_proj_kernel, nb=nb, tm=tm, k_transposed=k_transposed),
        grid=grid,
        in_specs=[pl.BlockSpec((nb, tm, D_MODEL), lambda b, t: (b, t, 0)),
                  pl.BlockSpec((D_MODEL, N_GROUPS * GROUP_W), lambda b, t: (0, 0))],
        out_specs=[act_spec, k_spec, act_spec, act_spec, act_spec, act_spec, act_spec, act_spec,
                   kv_spec, kv_spec],
        out_shape=[act(BF16), k_shape, act(BF16), act(BF16), act(BF16), act(F32), act(BF16),
                   act(BF16), kv_shape, kv_shape],
        compiler_params=pltpu.CompilerParams(
            dimension_semantics=("arbitrary", "arbitrary"), vmem_limit_bytes=VMEM_LIMIT),
    )(x, w_bf)
    if k_transposed:
        outs = list(outs[:8]) + [jnp.swapaxes(o, 3, 4) for o in outs[8:]]
    return outs


def _suffix_matrix(n):
    r = lax.broadcasted_iota(jnp.int32, (n, n), 0)
    c = lax.broadcasted_iota(jnp.int32, (n, n), 1)
    return jnp.where(r >= c, 1.0, 0.0).astype(BF16)


def _sb_weights(zs, suffix, runs, causal):
    incls = _sb_suffix_sums(zs, suffix, causal)
    ws = [_sb_weight(z, incl, run, causal) for z, incl, run in zip(zs, incls, runs)]
    return ws, [run + incl[:, 0:1] for run, incl in zip(runs, incls)]


def _sb_suffix_sums(zs, suffix, causal):
    drops = []
    for z in zs:
        drop = jnp.maximum(z, 0.0) + jnp.log2(1.0 + jnp.exp2(-jnp.abs(z)))
        if causal is not None:
            drop = jnp.where(causal, drop, 0.0)
        drops.append(drop.astype(BF16))
    return [_dot(drop, suffix) for drop in drops]


def _sb_weight(z, incl, run, causal):
    w = jnp.exp2(z - incl - run)
    if causal is not None:
        w = jnp.where(causal, w, 0.0)
    return w.astype(BF16)


def _sb_prompt_kernel(q_ref, kt_ref, v_ref, g_ref, o_ref, z_ref, w_ref, run_ref, acc_ref, *, heads):
    tq = q_ref.shape[1]
    i = pl.program_id(2)
    suffix = _suffix_matrix(KEY_TILE)
    r = lax.broadcasted_iota(jnp.int32, (tq, KEY_TILE), 0)
    c = lax.broadcasted_iota(jnp.int32, (tq, KEY_TILE), 1)
    causal = c < r
    head_lanes = [slice(hh * SB_HEAD_DIM, (hh + 1) * SB_HEAD_DIM) for hh in range(heads)]

    def logits(j, hh):
        return _dot(q_ref[0, :, head_lanes[hh]], kt_ref[0, j, head_lanes[hh], :])

    def add_weighted_values(slot, j):
        for hh in range(heads):
            acc_ref[hh] += _dot(w_ref[slot, hh], v_ref[0, j, :, head_lanes[hh]])

    def step(j, cur, masked):
        mask = causal if masked else None
        zs = [z_ref[cur, hh] for hh in range(heads)]
        incls = _sb_suffix_sums(zs, suffix, mask)
        for hh in range(heads):
            z_ref[1 - cur, hh] = logits(jnp.maximum(j - 1, 0), hh)
            run = jnp.zeros((tq, 1), F32) if masked else run_ref[hh]
            w_ref[1 - cur, hh] = _sb_weight(zs[hh], incls[hh], run, mask)
            run_ref[hh] = run + incls[hh][:, 0:1]

    for hh in range(heads):
        z_ref[0, hh] = logits(i, hh)
    acc_ref[...] = jnp.zeros_like(acc_ref)
    step(i, 0, True)

    def body(n, carry):
        j = i - 1 - n
        for parity in range(2):
            @pl.when(n % 2 == parity)
            def _():
                cur = 1 - parity
                add_weighted_values(cur, j + 1)
                step(j, cur, False)
        return carry

    lax.fori_loop(0, i, body, 0)
    for parity in range(2):
        @pl.when(i % 2 == parity)
        def _():
            add_weighted_values(1 - parity, 0)
    o = jnp.concatenate([acc_ref[hh] for hh in range(heads)], axis=1) * g_ref[0].astype(F32)
    o_ref[0] = o.astype(o_ref.dtype)


def _sb_prompt(qa, kt, va, sga, *, heads):
    B, T, _ = qa.shape
    tq = KEY_TILE
    nkt = T // KEY_TILE
    lanes = heads * SB_HEAD_DIM
    v4 = va.reshape(B, nkt, KEY_TILE, GROUP_W)
    row_spec = pl.BlockSpec((1, tq, lanes), lambda b, h, i: (b, i, h))
    return pl.pallas_call(
        functools.partial(_sb_prompt_kernel, heads=heads),
        grid=(B, GROUP_W // lanes, T // tq),
        in_specs=[row_spec,
                  pl.BlockSpec((1, nkt, lanes, KEY_TILE), lambda b, h, i: (b, 0, h, 0)),
                  pl.BlockSpec((1, nkt, KEY_TILE, lanes), lambda b, h, i: (b, 0, 0, h)),
                  row_spec],
        out_specs=row_spec,
        out_shape=jax.ShapeDtypeStruct((B, T, GROUP_W), BF16),
        scratch_shapes=[pltpu.VMEM((2, heads, tq, KEY_TILE), F32),
                        pltpu.VMEM((2, heads, tq, KEY_TILE), BF16),
                        pltpu.VMEM((heads, tq, 1), F32),
                        pltpu.VMEM((heads, tq, SB_HEAD_DIM), F32)],
        compiler_params=pltpu.CompilerParams(
            dimension_semantics=("arbitrary", "arbitrary", "arbitrary"),
            vmem_limit_bytes=VMEM_LIMIT),
    )(qa, kt, v4, sga)


def _sb_decode_kernel(q_ref, kn_ref, vn_ref, ck_ref, cv_ref, g_ref, o_ref):
    tq = q_ref.shape[1]
    past = ck_ref.shape[4]
    suffix = _suffix_matrix(KEY_TILE)
    suffix_new = _suffix_matrix(tq)
    r = lax.broadcasted_iota(jnp.int32, (tq, tq), 0)
    c = lax.broadcasted_iota(jnp.int32, (tq, tq), 1)
    causal = c < r
    heads = range(HEADS_PER_STEP)
    head_lanes = [slice(hh * SB_HEAD_DIM, (hh + 1) * SB_HEAD_DIM) for hh in heads]
    qs = [q_ref[0, :, lanes] for lanes in head_lanes]
    runs = [jnp.zeros((tq, 1), F32)] * HEADS_PER_STEP
    zs = [_dot_nt(q, kn_ref[0, :, lanes]) for q, lanes in zip(qs, head_lanes)]
    ws, runs = _sb_weights(zs, suffix_new, runs, causal)
    accs = [_dot(w, vn_ref[0, :, lanes]) for w, lanes in zip(ws, head_lanes)]
    for j in reversed(range(past // KEY_TILE)):
        keys = slice(j * KEY_TILE, (j + 1) * KEY_TILE)
        zs = [_dot(q, ck_ref[0, 0, hh, :, keys].astype(BF16)) for q, hh in zip(qs, heads)]
        ws, runs = _sb_weights(zs, suffix, runs, None)
        accs = [acc + _dot_nt(w, cv_ref[0, 0, hh, :, keys].astype(BF16))
                for w, hh, acc in zip(ws, heads, accs)]
    o = jnp.concatenate(accs, axis=1) * g_ref[0].astype(F32)
    o_ref[0] = o.astype(o_ref.dtype)


def _sb_decode(qa, kn, vn, cache_k, cache_v, layer, sga):
    B, T, _ = qa.shape
    past = cache_k.shape[3]
    assert past % KEY_TILE == 0
    lanes = HEADS_PER_STEP * SB_HEAD_DIM
    row_spec = pl.BlockSpec((1, T, lanes), lambda b, h: (b, 0, h))
    cache_spec = pl.BlockSpec((1, 1, HEADS_PER_STEP, SB_HEAD_DIM, past),
                              lambda b, h: (layer, b, h, 0, 0))
    cache_k = jnp.swapaxes(cache_k, 3, 4)
    cache_v = jnp.swapaxes(cache_v, 3, 4)
    return pl.pallas_call(
        _sb_decode_kernel,
        grid=(B, GROUP_W // lanes),
        in_specs=[row_spec, row_spec, row_spec, cache_spec, cache_spec, row_spec],
        out_specs=row_spec,
        out_shape=jax.ShapeDtypeStruct((B, T, GROUP_W), BF16),
        compiler_params=pltpu.CompilerParams(
            dimension_semantics=("arbitrary", "arbitrary"), vmem_limit_bytes=VMEM_LIMIT),
    )(qa, kn, vn, cache_k, cache_v, sga)


def _hgrn_tables(C):
    nl = int(math.log2(C))
    assert 1 << nl == C
    n_small = min(nl, int(math.log2(SUBLANES)))
    t = np.arange(C)[:, None]
    j = np.arange(C)[None, :]
    blocks = []
    for l in range(n_small):
        half = 1 << l
        mid = (t >> (l + 1) << (l + 1)) + half
        right = (t & half) != 0
        blocks.append(np.where(right, (j >= mid) & (j <= t), (j > t) & (j < mid)))
    blocks.append(j <= t)
    prefix = np.concatenate(blocks, axis=0).astype(np.float32)
    x = t ^ j
    msb = np.floor(np.log2(np.maximum(x, 1))).astype(np.int32)
    level_of = np.where(t > j, msb, np.where(t == j, nl, -1)).astype(np.int32)
    return jnp.asarray(prefix, BF16), jnp.asarray(level_of), nl, n_small


def _hgrn_kernel(*refs, C, nl, n_small, layer, has_s0):
    if has_s0:
        (qh_ref, fh_ref, ih_ref, gh_ref, pm_ref, lv_ref, lbl_ref, ng_ref, s0_ref,
         o_ref, so_ref, st_ref, b_ref) = refs
    else:
        (qh_ref, fh_ref, ih_ref, gh_ref, pm_ref, lv_ref, lbl_ref, ng_ref,
         o_ref, so_ref, st_ref, b_ref) = refs
    t = pl.program_id(1)

    @pl.when(t == 0)
    def _():
        if has_s0:
            for h in range(HG_HEADS):
                st_ref[h] = s0_ref[0, 0, h].T
        else:
            st_ref[...] = jnp.zeros_like(st_ref)

    logits = lbl_ref[...]
    e = jnp.exp(logits - jnp.max(logits, axis=0, keepdims=True))
    lb = jnp.sum(e[:layer + 1], axis=0, keepdims=True) / jnp.sum(e, axis=0, keepdims=True)

    f = lb + (1.0 - lb) * _sigmoid(fh_ref[0])
    g = jnp.log(f)
    kin = 1.0 - f
    qh = qh_ref[0].astype(F32)
    qs = qh * _sigmoid(qh)
    g2 = jnp.concatenate(_split_bf16(g, 2), axis=1)
    pre = _dot(pm_ref[...], g2)
    pre = pre[:, :GROUP_W] + pre[:, GROUP_W:]
    b = pre[n_small * C:]
    b_ref[...] = b

    def level_log(l):
        if l < n_small:
            return pre[l * C:(l + 1) * C]
        half = 1 << l
        parts = []
        for lo in range(0, C, 2 * half):
            last_left = b_ref[lo + half - 1:lo + half, :]
            parts.append(-jnp.abs(b_ref[lo:lo + 2 * half, :] - last_left))
        return parts[0] if len(parts) == 1 else jnp.concatenate(parts, axis=0)

    def head(a, h):
        return a[:, h * HG_HEAD_DIM:(h + 1) * HG_HEAD_DIM]

    qs_b = qs.astype(BF16)
    kin_b = kin.astype(BF16)

    def operands(l):
        if l == nl:
            return qs_b, kin_b
        e_l = jnp.exp(level_log(l)).astype(BF16)
        return qs_b * e_l, kin_b * e_l

    level_of = lv_ref[...]
    order = [nl] + list(range(nl))
    scores = [0.0] * HG_HEADS
    nxt = operands(order[0])
    for idx, l in enumerate(order):
        q_l, k_l = nxt
        if idx + 1 < len(order):
            nxt = operands(order[idx + 1])
        hit = level_of == l
        for h in range(HG_HEADS):
            scores[h] = jnp.where(hit, _dot_nt(head(q_l, h), head(k_l, h)), scores[h])

    q_in = (qs * jnp.exp(b)).astype(BF16)
    b_last = b_ref[C - 1:C, :]
    k_out = (kin * jnp.exp(b_last - b)).astype(BF16)
    chunk_decay = jnp.exp(b_last)
    i_b = ih_ref[0]
    gh = gh_ref[0].astype(F32)
    gate = ng_ref[...] * (gh * _sigmoid(gh))
    outs = []
    for h in range(HG_HEADS):
        st = st_ref[h]
        o = _dot(scores[h].astype(BF16), head(i_b, h)) + _dot_nt(head(q_in, h), st.astype(BF16))
        st_ref[h] = st * head(chunk_decay, h) + _dot_tn(head(i_b, h), head(k_out, h))
        outs.append(o * lax.rsqrt(jnp.mean(o * o, axis=-1, keepdims=True) + RMS_EPS))
    o_ref[0] = (jnp.concatenate(outs, axis=1) * gate).astype(o_ref.dtype)

    @pl.when(t == pl.num_programs(1) - 1)
    def _():
        for h in range(HG_HEADS):
            so_ref[0, 0, h] = st_ref[h].T


def _hgrn(qh, fh, ih, gh, lb_logits, norm_g, s0, layer, *, C):
    B, T, _ = qh.shape
    prefix, level_of, nl, n_small = _hgrn_tables(C)
    has_s0 = s0 is not None
    row_spec = pl.BlockSpec((1, C, GROUP_W), lambda b, t: (b, t, 0))
    const = lambda shape: pl.BlockSpec(shape, lambda b, t: (0,) * len(shape))
    state_shape = (1, 1, HG_HEADS, HG_HEAD_DIM, HG_HEAD_DIM)
    in_specs = [row_spec, row_spec, row_spec, row_spec, const(prefix.shape), const(level_of.shape),
                const(lb_logits.shape), const((1, GROUP_W))]
    args = [qh, fh, ih, gh, prefix, level_of, lb_logits.astype(F32),
            norm_g.reshape(1, GROUP_W).astype(F32)]
    if has_s0:
        in_specs.append(pl.BlockSpec(state_shape, lambda b, t: (layer, b, 0, 0, 0)))
        args.append(s0)
    return pl.pallas_call(
        functools.partial(_hgrn_kernel, C=C, nl=nl, n_small=n_small, layer=layer, has_s0=has_s0),
        grid=(B, T // C),
        in_specs=in_specs,
        out_specs=[row_spec, pl.BlockSpec(state_shape, lambda b, t: (0, b, 0, 0, 0))],
        out_shape=[jax.ShapeDtypeStruct((B, T, GROUP_W), BF16),
                   jax.ShapeDtypeStruct((1, B, HG_HEADS, HG_HEAD_DIM, HG_HEAD_DIM), F32)],
        scratch_shapes=[pltpu.VMEM((HG_HEADS, HG_HEAD_DIM, HG_HEAD_DIM), F32),
                        pltpu.VMEM((C, GROUP_W), F32)],
        compiler_params=pltpu.CompilerParams(
            dimension_semantics=("arbitrary", "arbitrary"), vmem_limit_bytes=VMEM_LIMIT),
    )(*args)


def _out_kernel(ma_ref, mb_ref, x_ref, w_ref, g_ref, b_ref, y_ref, *, alpha):
    out = _dot(ma_ref[...], w_ref[:GROUP_W, :]) + _dot(mb_ref[...], w_ref[GROUP_W:, :])
    hid = alpha * x_ref[...] + out
    mu = jnp.mean(hid, axis=-1, keepdims=True)
    cen = hid - mu
    var = jnp.mean(cen * cen, axis=-1, keepdims=True)
    y_ref[...] = cen * lax.rsqrt(var + LN_EPS) * g_ref[...] + b_ref[...]


def _out_project(ma, mb, x, w_bf, ln_g, ln_b, *, tm, alpha):
    B, T, _ = x.shape
    n = B * T
    mix_spec = pl.BlockSpec((tm, GROUP_W), lambda i: (i, 0))
    x_spec = pl.BlockSpec((tm, D_MODEL), lambda i: (i, 0))
    const = lambda shape: pl.BlockSpec(shape, lambda i: (0, 0))
    y = pl.pallas_call(
        functools.partial(_out_kernel, alpha=alpha),
        grid=(n // tm,),
        in_specs=[mix_spec, mix_spec, x_spec, const((2 * GROUP_W, D_MODEL)), const((1, D_MODEL)),
                  const((1, D_MODEL))],
        out_specs=x_spec,
        out_shape=jax.ShapeDtypeStruct((n, D_MODEL), F32),
        compiler_params=pltpu.CompilerParams(
            dimension_semantics=("arbitrary",), vmem_limit_bytes=VMEM_LIMIT),
    )(ma.reshape(n, GROUP_W), mb.reshape(n, GROUP_W), x.reshape(n, D_MODEL), w_bf,
      ln_g.reshape(1, D_MODEL).astype(F32), ln_b.reshape(1, D_MODEL).astype(F32))
    return y.reshape(B, T, D_MODEL)


PROMPT_ROWS = 512
PROMPT_CHUNK = 256


def _layer(x, cache_k, cache_v, state_s, layer, w_in_bf, w_out_bf, lb_logits, norm_g, ln_g, ln_b,
           alpha):
    B, T, _ = x.shape
    decode = cache_k is not None
    if decode:
        qa, k, va, sga, qh, fh, ih, gh, k_out, v_out = _project(
            x, w_in_bf, nb=B, tm=T, k_transposed=False)
        ma = _sb_decode(qa, k, va, cache_k, cache_v, layer, sga)
        mb, s_out = _hgrn(qh, fh, ih, gh, lb_logits, norm_g, state_s, layer, C=T)
        y = _out_project(ma, mb, x, w_out_bf, ln_g, ln_b, tm=B * T, alpha=alpha)
    else:
        qa, kt, va, sga, qh, fh, ih, gh, k_out, v_out = _project(
            x, w_in_bf, nb=1, tm=min(PROMPT_ROWS, T), k_transposed=True)
        ma = _sb_prompt(qa, kt, va, sga, heads=PROMPT_HEADS_PER_STEP)
        mb, s_out = _hgrn(qh, fh, ih, gh, lb_logits, norm_g, None, layer, C=min(PROMPT_CHUNK, T))
        y = _out_project(ma, mb, x, w_out_bf, ln_g, ln_b, tm=min(PROMPT_ROWS, T), alpha=alpha)
    return y, k_out, v_out, s_out


def kernel(x_prompt, x_sample, cache_k, cache_v, state_s, w_in, w_out, lb_logits, hgrn_norm_g,
           ln_g, ln_b):
    depth = w_in.shape[0]
    alpha = (2 * depth) ** 0.25
    yp, ys = x_prompt, x_sample
    per_layer = []
    for l in range(depth):
        w_in_bf = w_in[l].astype(BF16)
        w_out_bf = w_out[l].astype(BF16)
        common = (l, w_in_bf, w_out_bf, lb_logits, hgrn_norm_g[l], ln_g[l], ln_b[l], alpha)
        yp, kp, vp, sp = _layer(yp, None, None, None, *common)
        ys, kn, vn, sn = _layer(ys, cache_k, cache_v, state_s, *common)
        per_layer.append((kp, vp, sp, kn, vn, sn))
    stack = lambda i: (per_layer[0][i] if depth == 1
                       else jnp.concatenate([p[i] for p in per_layer], axis=0))
    return (yp, ys, stack(0), stack(1), stack(2), stack(3), stack(4), stack(5))
```

```python
import functools
import math

import numpy as np
import jax
import jax.numpy as jnp
from jax import lax
from jax.experimental import pallas as pl
from jax.experimental.pallas import tpu as pltpu

F32 = jnp.float32
BF16 = jnp.bfloat16

D_MODEL = 1024
GROUP_W = 512
N_GROUPS = 8
SB_HEADS = 8
SB_HEAD_DIM = 64
HG_HEADS = 4
HG_HEAD_DIM = 128
LN_EPS = 1e-5
RMS_EPS = 1e-6

SUBLANES = 8
KEY_TILE = 256
HEADS_PER_STEP = 8
PROMPT_HEADS_PER_STEP = 8
LOG2E = 1.4426950408889634
VMEM_LIMIT = 56 * 1024 * 1024


def _sigmoid(x):
    return 1.0 / (1.0 + jnp.exp(-x))


def _dot(a, b):
    return jnp.dot(a, b, preferred_element_type=F32)


def _dot_nt(a, b):
    return lax.dot_general(a, b, (((1,), (1,)), ((), ())), preferred_element_type=F32)


def _dot_tn(a, b):
    return lax.dot_general(a, b, (((0,), (0,)), ((), ())), preferred_element_type=F32)


def _split_bf16(x, parts):
    out = []
    for _ in range(parts - 1):
        p = x.astype(BF16)
        out.append(p)
        x = x - p.astype(F32)
    out.append(x.astype(BF16))
    return out


def _proj_kernel(x_ref, w_ref, qa_ref, k_ref, va_ref, sga_ref, qh_ref, fh_ref, ih_ref, gh_ref,
                 ko_ref, vo_ref, *, nb, tm, k_transposed):
    m = nb * tm
    x = x_ref[...].reshape(m, D_MODEL).astype(BF16)

    def col(c):
        return _dot(x, w_ref[:, c * GROUP_W:(c + 1) * GROUP_W])

    def put(ref, val):
        ref[...] = val.reshape(nb, tm, GROUP_W).astype(ref.dtype)

    def put_heads(ref, val):
        if k_transposed:
            val_t = val.T
            for h in range(SB_HEADS):
                ref[0, 0, h] = val_t[h * SB_HEAD_DIM:(h + 1) * SB_HEAD_DIM, :]
            return val_t
        for b in range(nb):
            for h in range(SB_HEADS):
                ref[0, b, h] = val[b * tm:(b + 1) * tm, h * SB_HEAD_DIM:(h + 1) * SB_HEAD_DIM]

    put(qa_ref, col(0) * (SB_HEAD_DIM ** -0.5 * LOG2E))
    k = col(1)
    kt = put_heads(ko_ref, k)
    if k_transposed:
        kt = kt.astype(BF16)
        for j in range(m // KEY_TILE):
            k_ref[0, j] = kt[:, j * KEY_TILE:(j + 1) * KEY_TILE]
    else:
        put(k_ref, k)
    v = col(2)
    put_heads(vo_ref, v)
    put(va_ref, v)
    ga = col(3)
    put(sga_ref, ga * _sigmoid(ga))
    put(qh_ref, col(4))
    put(fh_ref, col(5))
    put(ih_ref, col(6))
    put(gh_ref, col(7))


def _project(x, w_bf, *, nb, tm, k_transposed):
    B, T, _ = x.shape
    grid = (B // nb, T // tm)
    act = lambda dt: jax.ShapeDtypeStruct((B, T, GROUP_W), dt)
    act_spec = pl.BlockSpec((nb, tm, GROUP_W), lambda b, t: (b, t, 0))
    if k_transposed:
        assert nb == 1 and tm % KEY_TILE == 0
        k_shape = jax.ShapeDtypeStruct((B, T // KEY_TILE, GROUP_W, KEY_TILE), BF16)
        k_spec = pl.BlockSpec((1, tm // KEY_TILE, GROUP_W, KEY_TILE), lambda b, t: (b, t, 0, 0))
        kv_shape = jax.ShapeDtypeStruct((1, B, SB_HEADS, SB_HEAD_DIM, T), F32)
        kv_spec = pl.BlockSpec((1, 1, SB_HEADS, SB_HEAD_DIM, tm), lambda b, t: (0, b, 0, 0, t))
    else:
        k_shape, k_spec = act(BF16), act_spec
        kv_shape = jax.ShapeDtypeStruct((1, B, SB_HEADS, T, SB_HEAD_DIM), F32)
        kv_spec = pl.BlockSpec((1, nb, SB_HEADS, tm, SB_HEAD_DIM), lambda b, t: (0, b, 0, t, 0))
    outs = pl.pallas_call(
        functools.partial(_proj_kernel, nb=nb, tm=tm, k_transposed=k_transposed),
        grid=grid,
        in_specs=[pl.BlockSpec((nb, tm, D_MODEL), lambda b, t: (b, t, 0)),
                  pl.BlockSpec((D_MODEL, N_GROUPS * GROUP_W), lambda b, t: (0, 0))],
        out_specs=[act_spec, k_spec, act_spec, act_spec, act_spec, act_spec, act_spec, act_spec,
                   kv_spec, kv_spec],
        out_shape=[act(BF16), k_shape, act(BF16), act(BF16), act(BF16), act(F32), act(BF16),
                   act(BF16), kv_shape, kv_shape],
        compiler_params=pltpu.CompilerParams(
            dimension_semantics=("arbitrary", "arbitrary"), vmem_limit_bytes=VMEM_LIMIT),
    )(x, w_bf)
    if k_transposed:
        outs = list(outs[:8]) + [jnp.swapaxes(o, 3, 4) for o in outs[8:]]
    return outs


def _suffix_matrix(n):
    r = lax.broadcasted_iota(jnp.int32, (n, n), 0)
    c = lax.broadcasted_iota(jnp.int32, (n, n), 1)
    return jnp.where(r >= c, 1.0, 0.0).astype(BF16)


def _sb_weights(zs, suffix, runs, causal):
    incls = _sb_suffix_sums(zs, suffix, causal)
    ws = [_sb_weight(z, incl, run, causal) for z, incl, run in zip(zs, incls, runs)]
    return ws, [run + incl[:, 0:1] for run, incl in zip(runs, incls)]


def _sb_suffix_sums(zs, suffix, causal):
    drops = []
    for z in zs:
        drop = jnp.maximum(z, 0.0) + jnp.log2(1.0 + jnp.exp2(-jnp.abs(z)))
        if causal is not None:
            drop = jnp.where(causal, drop, 0.0)
        drops.append(drop.astype(BF16))
    return [_dot(drop, suffix) for drop in drops]


def _sb_weight(z, incl, run, causal):
    w = jnp.exp2(z - incl - run)
    if causal is not None:
        w = jnp.where(causal, w, 0.0)
    return w.astype(BF16)


def _sb_prompt_kernel(q_ref, kt_ref, v_ref, g_ref, o_ref, z_ref, w_ref, run_ref, acc_ref, *, heads):
    tq = q_ref.shape[1]
    i = pl.program_id(2)
    suffix = _suffix_matrix(KEY_TILE)
    r = lax.broadcasted_iota(jnp.int32, (tq, KEY_TILE), 0)
    c = lax.broadcasted_iota(jnp.int32, (tq, KEY_TILE), 1)
    causal = c < r
    head_lanes = [slice(hh * SB_HEAD_DIM, (hh + 1) * SB_HEAD_DIM) for hh in range(heads)]

    def logits(j, hh):
        return _dot(q_ref[0, :, head_lanes[hh]], kt_ref[0, j, head_lanes[hh], :])

    def add_weighted_values(slot, j):
        for hh in range(heads):
            acc_ref[hh] += _dot(w_ref[slot, hh], v_ref[0, j, :, head_lanes[hh]])

    def step(j, cur, masked):
        mask = causal if masked else None
        zs = [z_ref[cur, hh] for hh in range(heads)]
        incls = _sb_suffix_sums(zs, suffix, mask)
        for hh in range(heads):
            z_ref[1 - cur, hh] = logits(jnp.maximum(j - 1, 0), hh)
            run = jnp.zeros((tq, 1), F32) if masked else run_ref[hh]
            w_ref[1 - cur, hh] = _sb_weight(zs[hh], incls[hh], run, mask)
            run_ref[hh] = run + incls[hh][:, 0:1]

    for hh in range(heads):
        z_ref[0, hh] = logits(i, hh)
    acc_ref[...] = jnp.zeros_like(acc_ref)
    step(i, 0, True)

    def body(n, carry):
        j = i - 1 - n
        for parity in range(2):
            @pl.when(n % 2 == parity)
            def _():
                cur = 1 - parity
                add_weighted_values(cur, j + 1)
                step(j, cur, False)
        return carry

    lax.fori_loop(0, i, body, 0)
    for parity in range(2):
        @pl.when(i % 2 == parity)
        def _():
            add_weighted_values(1 - parity, 0)
    o = jnp.concatenate([acc_ref[hh] for hh in range(heads)], axis=1) * g_ref[0].astype(F32)
    o_ref[0] = o.astype(o_ref.dtype)


def _sb_prompt(qa, kt, va, sga, *, heads):
    B, T, _ = qa.shape
    tq = KEY_TILE
    nkt = T // KEY_TILE
    lanes = heads * SB_HEAD_DIM
    v4 = va.reshape(B, nkt, KEY_TILE, GROUP_W)
    row_spec = pl.BlockSpec((1, tq, lanes), lambda b, h, i: (b, i, h))
    return pl.pallas_call(
        functools.partial(_sb_prompt_kernel, heads=heads),
        grid=(B, GROUP_W // lanes, T // tq),
        in_specs=[row_spec,
                  pl.BlockSpec((1, nkt, lanes, KEY_TILE), lambda b, h, i: (b, 0, h, 0)),
                  pl.BlockSpec((1, nkt, KEY_TILE, lanes), lambda b, h, i: (b, 0, 0, h)),
                  row_spec],
        out_specs=row_spec,
        out_shape=jax.ShapeDtypeStruct((B, T, GROUP_W), BF16),
        scratch_shapes=[pltpu.VMEM((2, heads, tq, KEY_TILE), F32),
                        pltpu.VMEM((2, heads, tq, KEY_TILE), BF16),
                        pltpu.VMEM((heads, tq, 1), F32),
                        pltpu.VMEM((heads, tq, SB_HEAD_DIM), F32)],
        compiler_params=pltpu.CompilerParams(
            dimension_semantics=("arbitrary", "arbitrary", "arbitrary"),
            vmem_limit_bytes=VMEM_LIMIT),
    )(qa, kt, v4, sga)


def _sb_decode_kernel(q_ref, kn_ref, vn_ref, ck_ref, cv_ref, g_ref, o_ref):
    tq = q_ref.shape[1]
    past = ck_ref.shape[4]
    suffix = _suffix_matrix(KEY_TILE)
    suffix_new = _suffix_matrix(tq)
    r = lax.broadcasted_iota(jnp.int32, (tq, tq), 0)
    c = lax.broadcasted_iota(jnp.int32, (tq, tq), 1)
    causal = c < r
    heads = range(HEADS_PER_STEP)
    head_lanes = [slice(hh * SB_HEAD_DIM, (hh + 1) * SB_HEAD_DIM) for hh in heads]
    qs = [q_ref[0, :, lanes] for lanes in head_lanes]
    runs = [jnp.zeros((tq, 1), F32)] * HEADS_PER_STEP
    zs = [_dot_nt(q, kn_ref[0, :, lanes]) for q, lanes in zip(qs, head_lanes)]
    ws, runs = _sb_weights(zs, suffix_new, runs, causal)
    accs = [_dot(w, vn_ref[0, :, lanes]) for w, lanes in zip(ws, head_lanes)]
    for j in reversed(range(past // KEY_TILE)):
        keys = slice(j * KEY_TILE, (j + 1) * KEY_TILE)
        zs = [_dot(q, ck_ref[0, 0, hh, :, keys].astype(BF16)) for q, hh in zip(qs, heads)]
        ws, runs = _sb_weights(zs, suffix, runs, None)
        accs = [acc + _dot_nt(w, cv_ref[0, 0, hh, :, keys].astype(BF16))
                for w, hh, acc in zip(ws, heads, accs)]
    o = jnp.concatenate(accs, axis=1) * g_ref[0].astype(F32)
    o_ref[0] = o.astype(o_ref.dtype)


def _sb_decode(qa, kn, vn, cache_k, cache_v, layer, sga):
    B, T, _ = qa.shape
    past = cache_k.shape[3]
    assert past % KEY_TILE == 0
    lanes = HEADS_PER_STEP * SB_HEAD_DIM
    row_spec = pl.BlockSpec((1, T, lanes), lambda b, h: (b, 0, h))
    cache_spec = pl.BlockSpec((1, 1, HEADS_PER_STEP, SB_HEAD_DIM, past),
                              lambda b, h: (layer, b, h, 0, 0))
    cache_k = jnp.swapaxes(cache_k, 3, 4)
    cache_v = jnp.swapaxes(cache_v, 3, 4)
    return pl.pallas_call(
        _sb_decode_kernel,
        grid=(B, GROUP_W // lanes),
        in_specs=[row_spec, row_spec, row_spec, cache_spec, cache_spec, row_spec],
        out_specs=row_spec,
        out_shape=jax.ShapeDtypeStruct((B, T, GROUP_W), BF16),
        compiler_params=pltpu.CompilerParams(
            dimension_semantics=("arbitrary", "arbitrary"), vmem_limit_bytes=VMEM_LIMIT),
    )(qa, kn, vn, cache_k, cache_v, sga)


def _hgrn_tables(C):
    nl = int(math.log2(C))
    assert 1 << nl == C
    n_small = min(nl, int(math.log2(SUBLANES)))
    t = np.arange(C)[:, None]
    j = np.arange(C)[None, :]
    blocks = []
    for l in range(n_small):
        half = 1 << l
        mid = (t >> (l + 1) << (l + 1)) + half
        right = (t & half) != 0
        blocks.append(np.where(right, (j >= mid) & (j <= t), (j > t) & (j < mid)))
    blocks.append(j <= t)
    prefix = np.concatenate(blocks, axis=0).astype(np.float32)
    x = t ^ j
    msb = np.floor(np.log2(np.maximum(x, 1))).astype(np.int32)
    level_of = np.where(t > j, msb, np.where(t == j, nl, -1)).astype(np.int32)
    return jnp.asarray(prefix, BF16), jnp.asarray(level_of), nl, n_small


def _hgrn_kernel(*refs, C, sub, nl, n_small, layer, has_s0):
    if has_s0:
        (qh_ref, fh_ref, ih_ref, gh_ref, pm_ref, lv_ref, lbl_ref, ng_ref, s0_ref,
         o_ref, so_ref, st_ref, b_ref) = refs
    else:
        (qh_ref, fh_ref, ih_ref, gh_ref, pm_ref, lv_ref, lbl_ref, ng_ref,
         o_ref, so_ref, st_ref, b_ref) = refs
    t = pl.program_id(1)

    @pl.when(t == 0)
    def _():
        if has_s0:
            for h in range(HG_HEADS):
                st_ref[h] = s0_ref[0, 0, h].T
        else:
            st_ref[...] = jnp.zeros_like(st_ref)

    logits = lbl_ref[...]
    e = jnp.exp(logits - jnp.max(logits, axis=0, keepdims=True))
    lb = jnp.sum(e[:layer + 1], axis=0, keepdims=True) / jnp.sum(e, axis=0, keepdims=True)

    def head(a, h):
        return a[:, h * HG_HEAD_DIM:(h + 1) * HG_HEAD_DIM]

    def chunk(c):
        rows = slice(c * C, (c + 1) * C)
        f = lb + (1.0 - lb) * _sigmoid(fh_ref[0, rows, :])
        g = jnp.log(f)
        kin = 1.0 - f
        qh = qh_ref[0, rows, :].astype(F32)
        qs = qh * _sigmoid(qh)
        g2 = jnp.concatenate(_split_bf16(g, 2), axis=1)
        pre = _dot(pm_ref[...], g2)
        pre = pre[:, :GROUP_W] + pre[:, GROUP_W:]
        b = pre[n_small * C:]
        b_ref[c] = b

        def level_log(l):
            if l < n_small:
                return pre[l * C:(l + 1) * C]
            half = 1 << l
            parts = []
            for lo in range(0, C, 2 * half):
                last_left = b_ref[c, lo + half - 1:lo + half, :]
                parts.append(-jnp.abs(b_ref[c, lo:lo + 2 * half, :] - last_left))
            return parts[0] if len(parts) == 1 else jnp.concatenate(parts, axis=0)

        qs_b = qs.astype(BF16)
        kin_b = kin.astype(BF16)

        def operands(l):
            if l == nl:
                return qs_b, kin_b
            e_l = jnp.exp(level_log(l)).astype(BF16)
            return qs_b * e_l, kin_b * e_l

        level_of = lv_ref[...]
        order = [nl] + list(range(nl))
        scores = [0.0] * HG_HEADS
        nxt = operands(order[0])
        for idx, l in enumerate(order):
            q_l, k_l = nxt
            if idx + 1 < len(order):
                nxt = operands(order[idx + 1])
            hit = level_of == l
            for h in range(HG_HEADS):
                scores[h] = jnp.where(hit, _dot_nt(head(q_l, h), head(k_l, h)), scores[h])

        q_in = (qs * jnp.exp(b)).astype(BF16)
        b_last = b_ref[c, C - 1:C, :]
        k_out = (kin * jnp.exp(b_last - b)).astype(BF16)
        chunk_decay = jnp.exp(b_last)
        i_b = ih_ref[0, rows, :]
        gh = gh_ref[0, rows, :].astype(F32)
        gate = ng_ref[...] * (gh * _sigmoid(gh))
        outs = []
        for h in range(HG_HEADS):
            st = st_ref[h]
            o = (_dot(scores[h].astype(BF16), head(i_b, h))
                 + _dot_nt(head(q_in, h), st.astype(BF16)))
            st_ref[h] = st * head(chunk_decay, h) + _dot_tn(head(i_b, h), head(k_out, h))
            outs.append(o * lax.rsqrt(jnp.mean(o * o, axis=-1, keepdims=True) + RMS_EPS))
        o_ref[0, rows, :] = (jnp.concatenate(outs, axis=1) * gate).astype(o_ref.dtype)

    for c in range(sub):
        chunk(c)

    @pl.when(t == pl.num_programs(1) - 1)
    def _():
        for h in range(HG_HEADS):
            so_ref[0, 0, h] = st_ref[h].T


def _hgrn(qh, fh, ih, gh, lb_logits, norm_g, s0, layer, *, C, sub):
    B, T, _ = qh.shape
    prefix, level_of, nl, n_small = _hgrn_tables(C)
    has_s0 = s0 is not None
    row_spec = pl.BlockSpec((1, sub * C, GROUP_W), lambda b, t: (b, t, 0))
    const = lambda shape: pl.BlockSpec(shape, lambda b, t: (0,) * len(shape))
    state_shape = (1, 1, HG_HEADS, HG_HEAD_DIM, HG_HEAD_DIM)
    in_specs = [row_spec, row_spec, row_spec, row_spec, const(prefix.shape), const(level_of.shape),
                const(lb_logits.shape), const((1, GROUP_W))]
    args = [qh, fh, ih, gh, prefix, level_of, lb_logits.astype(F32),
            norm_g.reshape(1, GROUP_W).astype(F32)]
    if has_s0:
        in_specs.append(pl.BlockSpec(state_shape, lambda b, t: (layer, b, 0, 0, 0)))
        args.append(s0)
    return pl.pallas_call(
        functools.partial(_hgrn_kernel, C=C, sub=sub, nl=nl, n_small=n_small, layer=layer,
                          has_s0=has_s0),
        grid=(B, T // (sub * C)),
        in_specs=in_specs,
        out_specs=[row_spec, pl.BlockSpec(state_shape, lambda b, t: (0, b, 0, 0, 0))],
        out_shape=[jax.ShapeDtypeStruct((B, T, GROUP_W), BF16),
                   jax.ShapeDtypeStruct((1, B, HG_HEADS, HG_HEAD_DIM, HG_HEAD_DIM), F32)],
        scratch_shapes=[pltpu.VMEM((HG_HEADS, HG_HEAD_DIM, HG_HEAD_DIM), F32),
                        pltpu.VMEM((sub, C, GROUP_W), F32)],
        compiler_params=pltpu.CompilerParams(
            dimension_semantics=("arbitrary", "arbitrary"), vmem_limit_bytes=VMEM_LIMIT),
    )(*args)


def _out_kernel(ma_ref, mb_ref, x_ref, w_ref, g_ref, b_ref, y_ref, *, alpha):
    out = _dot(ma_ref[...], w_ref[:GROUP_W, :]) + _dot(mb_ref[...], w_ref[GROUP_W:, :])
    hid = alpha * x_ref[...] + out
    mu = jnp.mean(hid, axis=-1, keepdims=True)
    cen = hid - mu
    var = jnp.mean(cen * cen, axis=-1, keepdims=True)
    y_ref[...] = cen * lax.rsqrt(var + LN_EPS) * g_ref[...] + b_ref[...]


def _out_project(ma, mb, x, w_bf, ln_g, ln_b, *, tm, alpha):
    B, T, _ = x.shape
    n = B * T
    mix_spec = pl.BlockSpec((tm, GROUP_W), lambda i: (i, 0))
    x_spec = pl.BlockSpec((tm, D_MODEL), lambda i: (i, 0))
    const = lambda shape: pl.BlockSpec(shape, lambda i: (0, 0))
    y = pl.pallas_call(
        functools.partial(_out_kernel, alpha=alpha),
        grid=(n // tm,),
        in_specs=[mix_spec, mix_spec, x_spec, const((2 * GROUP_W, D_MODEL)), const((1, D_MODEL)),
                  const((1, D_MODEL))],
        out_specs=x_spec,
        out_shape=jax.ShapeDtypeStruct((n, D_MODEL), F32),
        compiler_params=pltpu.CompilerParams(
            dimension_semantics=("arbitrary",), vmem_limit_bytes=VMEM_LIMIT),
    )(ma.reshape(n, GROUP_W), mb.reshape(n, GROUP_W), x.reshape(n, D_MODEL), w_bf,
      ln_g.reshape(1, D_MODEL).astype(F32), ln_b.reshape(1, D_MODEL).astype(F32))
    return y.reshape(B, T, D_MODEL)


PROMPT_ROWS = 512
PROMPT_CHUNK = 128
PROMPT_CHUNKS_PER_STEP = 2


def _layer(x, cache_k, cache_v, state_s, layer, w_in_bf, w_out_bf, lb_logits, norm_g, ln_g, ln_b,
           alpha):
    B, T, _ = x.shape
    decode = cache_k is not None
    if decode:
        qa, k, va, sga, qh, fh, ih, gh, k_out, v_out = _project(
            x, w_in_bf, nb=B, tm=T, k_transposed=False)
        ma = _sb_decode(qa, k, va, cache_k, cache_v, layer, sga)
        mb, s_out = _hgrn(qh, fh, ih, gh, lb_logits, norm_g, state_s, layer, C=T, sub=1)
        y = _out_project(ma, mb, x, w_out_bf, ln_g, ln_b, tm=B * T, alpha=alpha)
    else:
        qa, kt, va, sga, qh, fh, ih, gh, k_out, v_out = _project(
            x, w_in_bf, nb=1, tm=min(PROMPT_ROWS, T), k_transposed=True)
        ma = _sb_prompt(qa, kt, va, sga, heads=PROMPT_HEADS_PER_STEP)
        chunk = min(PROMPT_CHUNK, T)
        mb, s_out = _hgrn(qh, fh, ih, gh, lb_logits, norm_g, None, layer, C=chunk,
                          sub=min(PROMPT_CHUNKS_PER_STEP, T // chunk))
        y = _out_project(ma, mb, x, w_out_bf, ln_g, ln_b, tm=min(PROMPT_ROWS, T), alpha=alpha)
    return y, k_out, v_out, s_out


def kernel(x_prompt, x_sample, cache_k, cache_v, state_s, w_in, w_out, lb_logits, hgrn_norm_g,
           ln_g, ln_b):
    depth = w_in.shape[0]
    alpha = (2 * depth) ** 0.25
    yp, ys = x_prompt, x_sample
    per_layer = []
    for l in range(depth):
        w_in_bf = w_in[l].astype(BF16)
        w_out_bf = w_out[l].astype(BF16)
        common = (l, w_in_bf, w_out_bf, lb_logits, hgrn_norm_g[l], ln_g[l], ln_b[l], alpha)
        yp, kp, vp, sp = _layer(yp, None, None, None, *common)
        ys, kn, vn, sn = _layer(ys, cache_k, cache_v, state_s, *common)
        per_layer.append((kp, vp, sp, kn, vn, sn))
    stack = lambda i: (per_layer[0][i] if depth == 1
                       else jnp.concatenate([p[i] for p in per_layer], axis=0))
    return (yp, ys, stack(0), stack(1), stack(2), stack(3), stack(4), stack(5))
```

```python
import functools
import math

import numpy as np
import jax
import jax.numpy as jnp
from jax import lax
from jax.experimental import pallas as pl
from jax.experimental.pallas import tpu as pltpu

F32 = jnp.float32
BF16 = jnp.bfloat16

D_MODEL = 1024
GROUP_W = 512
N_GROUPS = 8
SB_HEADS = 8
SB_HEAD_DIM = 64
HG_HEADS = 4
HG_HEAD_DIM = 128
LN_EPS = 1e-5
RMS_EPS = 1e-6

SUBLANES = 8
KEY_TILE = 256
HEADS_PER_STEP = 8
PROMPT_HEADS_PER_STEP = 8
LOG2E = 1.4426950408889634
VMEM_LIMIT = 56 * 1024 * 1024


def _sigmoid(x):
    return 1.0 / (1.0 + jnp.exp2(x * (-LOG2E)))


def _dot(a, b):
    return jnp.dot(a, b, preferred_element_type=F32)


def _dot_nt(a, b):
    return lax.dot_general(a, b, (((1,), (1,)), ((), ())), preferred_element_type=F32)


def _dot_tn(a, b):
    return lax.dot_general(a, b, (((0,), (0,)), ((), ())), preferred_element_type=F32)


def _split_bf16(x, parts):
    out = []
    for _ in range(parts - 1):
        p = x.astype(BF16)
        out.append(p)
        x = x - p.astype(F32)
    out.append(x.astype(BF16))
    return out


def _proj_kernel(x_ref, w_ref, qa_ref, k_ref, va_ref, sga_ref, qh_ref, fh_ref, ih_ref, gh_ref,
                 ko_ref, vo_ref, *, nb, tm, k_transposed):
    m = nb * tm
    x = x_ref[...].reshape(m, D_MODEL).astype(BF16)

    def col(c):
        return _dot(x, w_ref[:, c * GROUP_W:(c + 1) * GROUP_W])

    def put(ref, val):
        ref[...] = val.reshape(nb, tm, GROUP_W).astype(ref.dtype)

    def put_heads(ref, val):
        if k_transposed:
            val_t = val.T
            for h in range(SB_HEADS):
                ref[0, 0, h] = val_t[h * SB_HEAD_DIM:(h + 1) * SB_HEAD_DIM, :]
            return val_t
        for b in range(nb):
            for h in range(SB_HEADS):
                ref[0, b, h] = val[b * tm:(b + 1) * tm, h * SB_HEAD_DIM:(h + 1) * SB_HEAD_DIM]

    put(qa_ref, col(0) * (SB_HEAD_DIM ** -0.5 * LOG2E))
    k = col(1)
    kt = put_heads(ko_ref, k)
    if k_transposed:
        kt = kt.astype(BF16)
        for j in range(m // KEY_TILE):
            k_ref[0, j] = kt[:, j * KEY_TILE:(j + 1) * KEY_TILE]
    else:
        put(k_ref, k)
    v = col(2)
    put_heads(vo_ref, v)
    put(va_ref, v)
    ga = col(3)
    put(sga_ref, ga * _sigmoid(ga))
    put(qh_ref, col(4))
    put(fh_ref, col(5))
    put(ih_ref, col(6))
    put(gh_ref, col(7))


def _project(x, w_bf, *, nb, tm, k_transposed):
    B, T, _ = x.shape
    grid = (B // nb, T // tm)
    act = lambda dt: jax.ShapeDtypeStruct((B, T, GROUP_W), dt)
    act_spec = pl.BlockSpec((nb, tm, GROUP_W), lambda b, t: (b, t, 0))
    if k_transposed:
        assert nb == 1 and tm % KEY_TILE == 0
        k_shape = jax.ShapeDtypeStruct((B, T // KEY_TILE, GROUP_W, KEY_TILE), BF16)
        k_spec = pl.BlockSpec((1, tm // KEY_TILE, GROUP_W, KEY_TILE), lambda b, t: (b, t, 0, 0))
        kv_shape = jax.ShapeDtypeStruct((1, B, SB_HEADS, SB_HEAD_DIM, T), F32)
        kv_spec = pl.BlockSpec((1, 1, SB_HEADS, SB_HEAD_DIM, tm), lambda b, t: (0, b, 0, 0, t))
    else:
        k_shape, k_spec = act(BF16), act_spec
        kv_shape = jax.ShapeDtypeStruct((1, B, SB_HEADS, T, SB_HEAD_DIM), F32)
        kv_spec = pl.BlockSpec((1, nb, SB_HEADS, tm, SB_HEAD_DIM), lambda b, t: (0, b, 0, t, 0))
    outs = pl.pallas_call(
        functools.partial(_proj_kernel, nb=nb, tm=tm, k_transposed=k_transposed),
        grid=grid,
        in_specs=[pl.BlockSpec((nb, tm, D_MODEL), lambda b, t: (b, t, 0)),
                  pl.BlockSpec((D_MODEL, N_GROUPS * GROUP_W), lambda b, t: (0, 0))],
        out_specs=[act_spec, k_spec, act_spec, act_spec, act_spec, act_spec, act_spec, act_spec,
                   kv_spec, kv_spec],
        out_shape=[act(BF16), k_shape, act(BF16), act(BF16), act(BF16), act(F32), act(BF16),
                   act(BF16), kv_shape, kv_shape],
        compiler_params=pltpu.CompilerParams(
            dimension_semantics=("arbitrary", "arbitrary"), vmem_limit_bytes=VMEM_LIMIT),
    )(x, w_bf)
    if k_transposed:
        outs = list(outs[:8]) + [jnp.swapaxes(o, 3, 4) for o in outs[8:]]
    return outs


def _suffix_matrix(n):
    r = lax.broadcasted_iota(jnp.int32, (n, n), 0)
    c = lax.broadcasted_iota(jnp.int32, (n, n), 1)
    return jnp.where(r >= c, 1.0, 0.0).astype(BF16)


def _sb_weights(zs, suffix, runs, causal):
    incls = _sb_suffix_sums(zs, suffix, causal)
    ws = [_sb_weight(z, incl, run, causal) for z, incl, run in zip(zs, incls, runs)]
    return ws, [run + incl[:, 0:1] for run, incl in zip(runs, incls)]


def _sb_suffix_sums(zs, suffix, causal):
    drops = []
    for z in zs:
        drop = jnp.maximum(z, 0.0) + jnp.log2(1.0 + jnp.exp2(-jnp.abs(z)))
        if causal is not None:
            drop = jnp.where(causal, drop, 0.0)
        drops.append(drop.astype(BF16))
    return [_dot(drop, suffix) for drop in drops]


def _sb_weight(z, incl, run, causal):
    w = jnp.exp2(z - incl - run)
    if causal is not None:
        w = jnp.where(causal, w, 0.0)
    return w.astype(BF16)


def _sb_prompt_kernel(q_ref, kt_ref, v_ref, g_ref, o_ref, z_ref, w_ref, run_ref, acc_ref, *, heads):
    tq = q_ref.shape[1]
    i = pl.program_id(2)
    suffix = _suffix_matrix(KEY_TILE)
    r = lax.broadcasted_iota(jnp.int32, (tq, KEY_TILE), 0)
    c = lax.broadcasted_iota(jnp.int32, (tq, KEY_TILE), 1)
    causal = c < r
    head_lanes = [slice(hh * SB_HEAD_DIM, (hh + 1) * SB_HEAD_DIM) for hh in range(heads)]

    def logits(j, hh):
        return _dot(q_ref[0, :, head_lanes[hh]], kt_ref[0, j, head_lanes[hh], :])

    def add_weighted_values(slot, j):
        for hh in range(heads):
            acc_ref[hh] += _dot(w_ref[slot, hh], v_ref[0, j, :, head_lanes[hh]])

    def step(j, cur, masked):
        mask = causal if masked else None
        zs = [z_ref[cur, hh] for hh in range(heads)]
        incls = _sb_suffix_sums(zs, suffix, mask)
        for hh in range(heads):
            z_ref[1 - cur, hh] = logits(jnp.maximum(j - 1, 0), hh)
            run = jnp.zeros((tq, 1), F32) if masked else run_ref[hh]
            w_ref[1 - cur, hh] = _sb_weight(zs[hh], incls[hh], run, mask)
            run_ref[hh] = run + incls[hh][:, 0:1]

    for hh in range(heads):
        z_ref[0, hh] = logits(i, hh)
    acc_ref[...] = jnp.zeros_like(acc_ref)
    step(i, 0, True)

    def body(n, carry):
        j = i - 1 - n
        for parity in range(2):
            @pl.when(n % 2 == parity)
            def _():
                cur = 1 - parity
                add_weighted_values(cur, j + 1)
                step(j, cur, False)
        return carry

    lax.fori_loop(0, i, body, 0)
    for parity in range(2):
        @pl.when(i % 2 == parity)
        def _():
            add_weighted_values(1 - parity, 0)
    o = jnp.concatenate([acc_ref[hh] for hh in range(heads)], axis=1) * g_ref[0].astype(F32)
    o_ref[0] = o.astype(o_ref.dtype)


def _sb_prompt(qa, kt, va, sga, *, heads):
    B, T, _ = qa.shape
    tq = KEY_TILE
    nkt = T // KEY_TILE
    lanes = heads * SB_HEAD_DIM
    v4 = va.reshape(B, nkt, KEY_TILE, GROUP_W)
    row_spec = pl.BlockSpec((1, tq, lanes), lambda b, h, i: (b, i, h))
    return pl.pallas_call(
        functools.partial(_sb_prompt_kernel, heads=heads),
        grid=(B, GROUP_W // lanes, T // tq),
        in_specs=[row_spec,
                  pl.BlockSpec((1, nkt, lanes, KEY_TILE), lambda b, h, i: (b, 0, h, 0)),
                  pl.BlockSpec((1, nkt, KEY_TILE, lanes), lambda b, h, i: (b, 0, 0, h)),
                  row_spec],
        out_specs=row_spec,
        out_shape=jax.ShapeDtypeStruct((B, T, GROUP_W), BF16),
        scratch_shapes=[pltpu.VMEM((2, heads, tq, KEY_TILE), F32),
                        pltpu.VMEM((2, heads, tq, KEY_TILE), BF16),
                        pltpu.VMEM((heads, tq, 1), F32),
                        pltpu.VMEM((heads, tq, SB_HEAD_DIM), F32)],
        compiler_params=pltpu.CompilerParams(
            dimension_semantics=("arbitrary", "arbitrary", "arbitrary"),
            vmem_limit_bytes=VMEM_LIMIT),
    )(qa, kt, v4, sga)


def _sb_decode_kernel(q_ref, kn_ref, vn_ref, ck_ref, cv_ref, g_ref, o_ref):
    tq = q_ref.shape[1]
    past = ck_ref.shape[4]
    suffix = _suffix_matrix(KEY_TILE)
    suffix_new = _suffix_matrix(tq)
    r = lax.broadcasted_iota(jnp.int32, (tq, tq), 0)
    c = lax.broadcasted_iota(jnp.int32, (tq, tq), 1)
    causal = c < r
    heads = range(HEADS_PER_STEP)
    head_lanes = [slice(hh * SB_HEAD_DIM, (hh + 1) * SB_HEAD_DIM) for hh in heads]
    qs = [q_ref[0, :, lanes] for lanes in head_lanes]
    runs = [jnp.zeros((tq, 1), F32)] * HEADS_PER_STEP
    zs = [_dot_nt(q, kn_ref[0, :, lanes]) for q, lanes in zip(qs, head_lanes)]
    ws, runs = _sb_weights(zs, suffix_new, runs, causal)
    accs = [_dot(w, vn_ref[0, :, lanes]) for w, lanes in zip(ws, head_lanes)]
    for j in reversed(range(past // KEY_TILE)):
        keys = slice(j * KEY_TILE, (j + 1) * KEY_TILE)
        zs = [_dot(q, ck_ref[0, 0, hh, :, keys].astype(BF16)) for q, hh in zip(qs, heads)]
        ws, runs = _sb_weights(zs, suffix, runs, None)
        accs = [acc + _dot_nt(w, cv_ref[0, 0, hh, :, keys].astype(BF16))
                for w, hh, acc in zip(ws, heads, accs)]
    o = jnp.concatenate(accs, axis=1) * g_ref[0].astype(F32)
    o_ref[0] = o.astype(o_ref.dtype)


def _sb_decode(qa, kn, vn, cache_k, cache_v, layer, sga):
    B, T, _ = qa.shape
    past = cache_k.shape[3]
    assert past % KEY_TILE == 0
    lanes = HEADS_PER_STEP * SB_HEAD_DIM
    row_spec = pl.BlockSpec((1, T, lanes), lambda b, h: (b, 0, h))
    cache_spec = pl.BlockSpec((1, 1, HEADS_PER_STEP, SB_HEAD_DIM, past),
                              lambda b, h: (layer, b, h, 0, 0))
    cache_k = jnp.swapaxes(cache_k, 3, 4)
    cache_v = jnp.swapaxes(cache_v, 3, 4)
    return pl.pallas_call(
        _sb_decode_kernel,
        grid=(B, GROUP_W // lanes),
        in_specs=[row_spec, row_spec, row_spec, cache_spec, cache_spec, row_spec],
        out_specs=row_spec,
        out_shape=jax.ShapeDtypeStruct((B, T, GROUP_W), BF16),
        compiler_params=pltpu.CompilerParams(
            dimension_semantics=("arbitrary", "arbitrary"), vmem_limit_bytes=VMEM_LIMIT),
    )(qa, kn, vn, cache_k, cache_v, sga)


def _hgrn_tables(C):
    nl = int(math.log2(C))
    assert 1 << nl == C
    n_small = min(nl, int(math.log2(SUBLANES)))
    t = np.arange(C)[:, None]
    j = np.arange(C)[None, :]
    blocks = []
    for l in range(n_small):
        half = 1 << l
        mid = (t >> (l + 1) << (l + 1)) + half
        right = (t & half) != 0
        blocks.append(np.where(right, (j >= mid) & (j <= t), (j > t) & (j < mid)))
    blocks.append(j <= t)
    prefix = np.concatenate(blocks, axis=0).astype(np.float32)
    x = t ^ j
    msb = np.floor(np.log2(np.maximum(x, 1))).astype(np.int32)
    level_of = np.where(t > j, msb, np.where(t == j, nl, -1)).astype(np.int32)
    return jnp.asarray(prefix, BF16), jnp.asarray(level_of), nl, n_small


def _hgrn_kernel(*refs, C, sub, nl, n_small, layer, has_s0):
    if has_s0:
        (qh_ref, fh_ref, ih_ref, gh_ref, pm_ref, lv_ref, lbl_ref, ng_ref, s0_ref,
         o_ref, so_ref, st_ref, b_ref) = refs
    else:
        (qh_ref, fh_ref, ih_ref, gh_ref, pm_ref, lv_ref, lbl_ref, ng_ref,
         o_ref, so_ref, st_ref, b_ref) = refs
    t = pl.program_id(1)

    @pl.when(t == 0)
    def _():
        if has_s0:
            for h in range(HG_HEADS):
                st_ref[h] = s0_ref[0, 0, h].T
        else:
            st_ref[...] = jnp.zeros_like(st_ref)

    logits = lbl_ref[...]
    e = jnp.exp(logits - jnp.max(logits, axis=0, keepdims=True))
    lb = jnp.sum(e[:layer + 1], axis=0, keepdims=True) / jnp.sum(e, axis=0, keepdims=True)

    def head(a, h):
        return a[:, h * HG_HEAD_DIM:(h + 1) * HG_HEAD_DIM]

    def prepare(c):
        rows = slice(c * C, (c + 1) * C)
        f = lb + (1.0 - lb) * _sigmoid(fh_ref[0, rows, :])
        g = jnp.log2(f)
        kin = 1.0 - f
        qh = qh_ref[0, rows, :].astype(F32)
        qs = qh * _sigmoid(qh)
        g_hi, g_lo = _split_bf16(g, 2)
        pre = _dot(pm_ref[:n_small * C, :], g_hi)
        cum = _dot(pm_ref[n_small * C:, :], jnp.concatenate([g_hi, g_lo], axis=1))
        b_ref[c] = cum[:, :GROUP_W] + cum[:, GROUP_W:]
        return dict(rows=rows, qs=qs, kin=kin, pre=pre, qs_b=qs.astype(BF16),
                    kin_b=kin.astype(BF16))

    def level_log(c, pre, l):
        if l < n_small:
            return pre[l * C:(l + 1) * C]
        half = 1 << l
        parts = []
        for lo in range(0, C, 2 * half):
            mid = lo + half
            last_left = b_ref[c, mid - 1:mid, :]
            parts.append(last_left - b_ref[c, lo:mid, :])
            parts.append(b_ref[c, mid:mid + half, :] - last_left)
        return jnp.concatenate(parts, axis=0)

    def operands(c, p, l):
        if l == nl:
            return p["qs_b"], p["kin_b"]
        e_l = jnp.exp2(level_log(c, p["pre"], l)).astype(BF16)
        return p["qs_b"] * e_l, p["kin_b"] * e_l

    def finish(c, p, scores):
        rows = p["rows"]
        b = b_ref[c]
        b_last = b_ref[c, C - 1:C, :]
        q_in = (p["qs"] * jnp.exp2(b)).astype(BF16)
        k_out = (p["kin"] * jnp.exp2(b_last - b)).astype(BF16)
        chunk_decay = jnp.exp2(b_last)
        i_b = ih_ref[0, rows, :]
        gh = gh_ref[0, rows, :].astype(F32)
        gate = ng_ref[...] * (gh * _sigmoid(gh))
        outs = []
        for h in range(HG_HEADS):
            st = st_ref[h]
            o = (_dot(scores[h].astype(BF16), head(i_b, h))
                 + _dot_nt(head(q_in, h), st.astype(BF16)))
            st_ref[h] = st * head(chunk_decay, h) + _dot_tn(head(i_b, h), head(k_out, h))
            outs.append(o * lax.rsqrt(jnp.mean(o * o, axis=-1, keepdims=True) + RMS_EPS))
        o_ref[0, rows, :] = (jnp.concatenate(outs, axis=1) * gate).astype(o_ref.dtype)

    chunks = range(sub)
    preps = [prepare(c) for c in chunks]
    level_of = lv_ref[...]
    order = [(l, c) for l in [nl] + list(range(nl)) for c in chunks]
    scores = [[0.0] * HG_HEADS for _ in chunks]
    nxt = operands(order[0][1], preps[order[0][1]], order[0][0])
    for idx, (l, c) in enumerate(order):
        q_l, k_l = nxt
        if idx + 1 < len(order):
            l_n, c_n = order[idx + 1]
            nxt = operands(c_n, preps[c_n], l_n)
        hit = level_of == l
        for h in range(HG_HEADS):
            scores[c][h] = jnp.where(hit, _dot_nt(head(q_l, h), head(k_l, h)), scores[c][h])
    for c in chunks:
        finish(c, preps[c], scores[c])

    @pl.when(t == pl.num_programs(1) - 1)
    def _():
        for h in range(HG_HEADS):
            so_ref[0, 0, h] = st_ref[h].T


def _hgrn(qh, fh, ih, gh, lb_logits, norm_g, s0, layer, *, C, sub):
    B, T, _ = qh.shape
    prefix, level_of, nl, n_small = _hgrn_tables(C)
    has_s0 = s0 is not None
    row_spec = pl.BlockSpec((1, sub * C, GROUP_W), lambda b, t: (b, t, 0))
    const = lambda shape: pl.BlockSpec(shape, lambda b, t: (0,) * len(shape))
    state_shape = (1, 1, HG_HEADS, HG_HEAD_DIM, HG_HEAD_DIM)
    in_specs = [row_spec, row_spec, row_spec, row_spec, const(prefix.shape), const(level_of.shape),
                const(lb_logits.shape), const((1, GROUP_W))]
    args = [qh, fh, ih, gh, prefix, level_of, lb_logits.astype(F32),
            norm_g.reshape(1, GROUP_W).astype(F32)]
    if has_s0:
        in_specs.append(pl.BlockSpec(state_shape, lambda b, t: (layer, b, 0, 0, 0)))
        args.append(s0)
    return pl.pallas_call(
        functools.partial(_hgrn_kernel, C=C, sub=sub, nl=nl, n_small=n_small, layer=layer,
                          has_s0=has_s0),
        grid=(B, T // (sub * C)),
        in_specs=in_specs,
        out_specs=[row_spec, pl.BlockSpec(state_shape, lambda b, t: (0, b, 0, 0, 0))],
        out_shape=[jax.ShapeDtypeStruct((B, T, GROUP_W), BF16),
                   jax.ShapeDtypeStruct((1, B, HG_HEADS, HG_HEAD_DIM, HG_HEAD_DIM), F32)],
        scratch_shapes=[pltpu.VMEM((HG_HEADS, HG_HEAD_DIM, HG_HEAD_DIM), F32),
                        pltpu.VMEM((sub, C, GROUP_W), F32)],
        compiler_params=pltpu.CompilerParams(
            dimension_semantics=("arbitrary", "arbitrary"), vmem_limit_bytes=VMEM_LIMIT),
    )(*args)


def _out_kernel(ma_ref, mb_ref, x_ref, w_ref, g_ref, b_ref, y_ref, *, alpha):
    out = _dot(ma_ref[...], w_ref[:GROUP_W, :]) + _dot(mb_ref[...], w_ref[GROUP_W:, :])
    hid = alpha * x_ref[...] + out
    mu = jnp.mean(hid, axis=-1, keepdims=True)
    cen = hid - mu
    var = jnp.mean(cen * cen, axis=-1, keepdims=True)
    y_ref[...] = cen * lax.rsqrt(var + LN_EPS) * g_ref[...] + b_ref[...]


def _out_project(ma, mb, x, w_bf, ln_g, ln_b, *, tm, alpha):
    B, T, _ = x.shape
    n = B * T
    mix_spec = pl.BlockSpec((tm, GROUP_W), lambda i: (i, 0))
    x_spec = pl.BlockSpec((tm, D_MODEL), lambda i: (i, 0))
    const = lambda shape: pl.BlockSpec(shape, lambda i: (0, 0))
    y = pl.pallas_call(
        functools.partial(_out_kernel, alpha=alpha),
        grid=(n // tm,),
        in_specs=[mix_spec, mix_spec, x_spec, const((2 * GROUP_W, D_MODEL)), const((1, D_MODEL)),
                  const((1, D_MODEL))],
        out_specs=x_spec,
        out_shape=jax.ShapeDtypeStruct((n, D_MODEL), F32),
        compiler_params=pltpu.CompilerParams(
            dimension_semantics=("arbitrary",), vmem_limit_bytes=VMEM_LIMIT),
    )(ma.reshape(n, GROUP_W), mb.reshape(n, GROUP_W), x.reshape(n, D_MODEL), w_bf,
      ln_g.reshape(1, D_MODEL).astype(F32), ln_b.reshape(1, D_MODEL).astype(F32))
    return y.reshape(B, T, D_MODEL)


PROMPT_ROWS = 512
PROMPT_CHUNK = 128
PROMPT_CHUNKS_PER_STEP = 4


def _layer(x, cache_k, cache_v, state_s, layer, w_in_bf, w_out_bf, lb_logits, norm_g, ln_g, ln_b,
           alpha):
    B, T, _ = x.shape
    decode = cache_k is not None
    if decode:
        qa, k, va, sga, qh, fh, ih, gh, k_out, v_out = _project(
            x, w_in_bf, nb=B, tm=T, k_transposed=False)
        ma = _sb_decode(qa, k, va, cache_k, cache_v, layer, sga)
        mb, s_out = _hgrn(qh, fh, ih, gh, lb_logits, norm_g, state_s, layer, C=T, sub=1)
        y = _out_project(ma, mb, x, w_out_bf, ln_g, ln_b, tm=B * T, alpha=alpha)
    else:
        qa, kt, va, sga, qh, fh, ih, gh, k_out, v_out = _project(
            x, w_in_bf, nb=1, tm=min(PROMPT_ROWS, T), k_transposed=True)
        ma = _sb_prompt(qa, kt, va, sga, heads=PROMPT_HEADS_PER_STEP)
        chunk = min(PROMPT_CHUNK, T)
        mb, s_out = _hgrn(qh, fh, ih, gh, lb_logits, norm_g, None, layer, C=chunk,
                          sub=min(PROMPT_CHUNKS_PER_STEP, T // chunk))
        y = _out_project(ma, mb, x, w_out_bf, ln_g, ln_b, tm=min(PROMPT_ROWS, T), alpha=alpha)
    return y, k_out, v_out, s_out


def kernel(x_prompt, x_sample, cache_k, cache_v, state_s, w_in, w_out, lb_logits, hgrn_norm_g,
           ln_g, ln_b):
    depth = w_in.shape[0]
    alpha = (2 * depth) ** 0.25
    yp, ys = x_prompt, x_sample
    per_layer = []
    for l in range(depth):
        w_in_bf = w_in[l].astype(BF16)
        w_out_bf = w_out[l].astype(BF16)
        common = (l, w_in_bf, w_out_bf, lb_logits, hgrn_norm_g[l], ln_g[l], ln_b[l], alpha)
        yp, kp, vp, sp = _layer(yp, None, None, None, *common)
        ys, kn, vn, sn = _layer(ys, cache_k, cache_v, state_s, *common)
        per_layer.append((kp, vp, sp, kn, vn, sn))
    stack = lambda i: (per_layer[0][i] if depth == 1
                       else jnp.concatenate([p[i] for p in per_layer], axis=0))
    return (yp, ys, stack(0), stack(1), stack(2), stack(3), stack(4), stack(5))
```

```python
import functools
import math

import numpy as np
import jax
import jax.numpy as jnp
from jax import lax
from jax.experimental import pallas as pl
from jax.experimental.pallas import tpu as pltpu

F32 = jnp.float32
BF16 = jnp.bfloat16

D_MODEL = 1024
GROUP_W = 512
N_GROUPS = 8
SB_HEADS = 8
SB_HEAD_DIM = 64
HG_HEADS = 4
HG_HEAD_DIM = 128
LN_EPS = 1e-5
RMS_EPS = 1e-6

SUBLANES = 8
KEY_TILE = 256
HEADS_PER_STEP = 8
LOG2E = 1.4426950408889634
VMEM_LIMIT = 56 * 1024 * 1024


def _sigmoid(x):
    return 1.0 / (1.0 + jnp.exp2(x * (-LOG2E)))


def _dot(a, b):
    return jnp.dot(a, b, preferred_element_type=F32)


def _dot_nt(a, b):
    return lax.dot_general(a, b, (((1,), (1,)), ((), ())), preferred_element_type=F32)


def _dot_tn(a, b):
    return lax.dot_general(a, b, (((0,), (0,)), ((), ())), preferred_element_type=F32)


def _split_bf16(x, parts):
    out = []
    for _ in range(parts - 1):
        p = x.astype(BF16)
        out.append(p)
        x = x - p.astype(F32)
    out.append(x.astype(BF16))
    return out


def _proj_kernel(x_ref, w_ref, qa_ref, k_ref, va_ref, sga_ref, qh_ref, fh_ref, ih_ref, gh_ref,
                 ko_ref, vo_ref, *, nb, tm, k_transposed):
    m = nb * tm
    x = x_ref[...].reshape(m, D_MODEL).astype(BF16)

    def col(c):
        return _dot(x, w_ref[:, c * GROUP_W:(c + 1) * GROUP_W])

    def put(ref, val):
        ref[...] = val.reshape(nb, tm, GROUP_W).astype(ref.dtype)

    def put_heads(ref, val):
        if k_transposed:
            val_t = val.T
            for h in range(SB_HEADS):
                ref[0, 0, h] = val_t[h * SB_HEAD_DIM:(h + 1) * SB_HEAD_DIM, :]
            return val_t
        for b in range(nb):
            for h in range(SB_HEADS):
                ref[0, b, h] = val[b * tm:(b + 1) * tm, h * SB_HEAD_DIM:(h + 1) * SB_HEAD_DIM]

    put(qa_ref, col(0) * (SB_HEAD_DIM ** -0.5 * LOG2E))
    k = col(1)
    kt = put_heads(ko_ref, k)
    if k_transposed:
        kt = kt.astype(BF16)
        for j in range(m // KEY_TILE):
            k_ref[0, j] = kt[:, j * KEY_TILE:(j + 1) * KEY_TILE]
    else:
        put(k_ref, k)
    v = col(2)
    put_heads(vo_ref, v)
    put(va_ref, v)
    ga = col(3)
    put(sga_ref, ga * _sigmoid(ga))
    put(qh_ref, col(4))
    put(fh_ref, col(5))
    put(ih_ref, col(6))
    put(gh_ref, col(7))


def _project(x, w_bf, *, nb, tm, k_transposed):
    B, T, _ = x.shape
    grid = (B // nb, T // tm)
    act = lambda dt: jax.ShapeDtypeStruct((B, T, GROUP_W), dt)
    act_spec = pl.BlockSpec((nb, tm, GROUP_W), lambda b, t: (b, t, 0))
    if k_transposed:
        assert nb == 1 and tm % KEY_TILE == 0
        k_shape = jax.ShapeDtypeStruct((B, T // KEY_TILE, GROUP_W, KEY_TILE), BF16)
        k_spec = pl.BlockSpec((1, tm // KEY_TILE, GROUP_W, KEY_TILE), lambda b, t: (b, t, 0, 0))
        kv_shape = jax.ShapeDtypeStruct((1, B, SB_HEADS, SB_HEAD_DIM, T), F32)
        kv_spec = pl.BlockSpec((1, 1, SB_HEADS, SB_HEAD_DIM, tm), lambda b, t: (0, b, 0, 0, t))
    else:
        k_shape, k_spec = act(BF16), act_spec
        kv_shape = jax.ShapeDtypeStruct((1, B, SB_HEADS, T, SB_HEAD_DIM), F32)
        kv_spec = pl.BlockSpec((1, nb, SB_HEADS, tm, SB_HEAD_DIM), lambda b, t: (0, b, 0, t, 0))
    outs = pl.pallas_call(
        functools.partial(_proj_kernel, nb=nb, tm=tm, k_transposed=k_transposed),
        grid=grid,
        in_specs=[pl.BlockSpec((nb, tm, D_MODEL), lambda b, t: (b, t, 0)),
                  pl.BlockSpec((D_MODEL, N_GROUPS * GROUP_W), lambda b, t: (0, 0))],
        out_specs=[act_spec, k_spec, act_spec, act_spec, act_spec, act_spec, act_spec, act_spec,
                   kv_spec, kv_spec],
        out_shape=[act(BF16), k_shape, act(BF16), act(BF16), act(BF16), act(F32), act(BF16),
                   act(BF16), kv_shape, kv_shape],
        compiler_params=pltpu.CompilerParams(
            dimension_semantics=("arbitrary", "arbitrary"), vmem_limit_bytes=VMEM_LIMIT),
    )(x, w_bf)
    if k_transposed:
        outs = list(outs[:8]) + [jnp.swapaxes(o, 3, 4) for o in outs[8:]]
    return outs


def _suffix_matrix(n):
    r = lax.broadcasted_iota(jnp.int32, (n, n), 0)
    c = lax.broadcasted_iota(jnp.int32, (n, n), 1)
    return jnp.where(r >= c, 1.0, 0.0).astype(BF16)


def _sb_weights(zs, suffix, runs, causal):
    incls = _sb_suffix_sums(zs, suffix, causal)
    ws = [_sb_weight(z, incl, run, causal) for z, incl, run in zip(zs, incls, runs)]
    return ws, [run + incl[:, 0:1] for run, incl in zip(runs, incls)]


def _sb_suffix_sums(zs, suffix, causal):
    drops = []
    for z in zs:
        drop = jnp.maximum(z, 0.0) + jnp.log2(1.0 + jnp.exp2(-jnp.abs(z)))
        if causal is not None:
            drop = jnp.where(causal, drop, 0.0)
        drops.append(drop.astype(BF16))
    return [_dot(drop, suffix) for drop in drops]


def _sb_weight(z, incl, run, causal):
    w = jnp.exp2(z - incl - run)
    if causal is not None:
        w = jnp.where(causal, w, 0.0)
    return w.astype(BF16)


def _sb_prompt_kernel(q_ref, kt_ref, v_ref, g_ref, o_ref, z_ref, w_ref, run_ref, acc_ref, *, heads):
    tq = KEY_TILE
    nq = q_ref.shape[1] // tq
    i = pl.program_id(1)
    suffix = _suffix_matrix(KEY_TILE)
    r = lax.broadcasted_iota(jnp.int32, (tq, KEY_TILE), 0)
    c = lax.broadcasted_iota(jnp.int32, (tq, KEY_TILE), 1)
    causal = c < r
    head_lanes = [slice(hh * SB_HEAD_DIM, (hh + 1) * SB_HEAD_DIM) for hh in range(heads)]

    def block_rows(bi):
        return pl.ds(pl.multiple_of(bi * tq, tq), tq)

    def logits(bi, j, hh):
        return _dot(q_ref[0, block_rows(bi), head_lanes[hh]], kt_ref[0, j, head_lanes[hh], :])

    def add_weighted_values(slot, j):
        for hh in range(heads):
            acc_ref[hh] += _dot(w_ref[slot, hh], v_ref[0, j, :, head_lanes[hh]])

    def finalize(bi):
        o = jnp.concatenate([acc_ref[hh] for hh in range(heads)], axis=1)
        o_ref[0, block_rows(bi), :] = (o * g_ref[0, block_rows(bi), :].astype(F32)).astype(o_ref.dtype)

    def step(bi, j, cur, masked):
        mask = causal if masked else None
        zs = [z_ref[cur, hh] for hh in range(heads)]
        incls = _sb_suffix_sums(zs, suffix, mask)
        same_block = j > 0
        nxt = jnp.minimum(bi + 1, nq - 1)
        next_bi = jnp.where(same_block, bi, nxt)
        next_j = jnp.where(same_block, j - 1, nxt)
        for hh in range(heads):
            z_ref[1 - cur, hh] = logits(next_bi, next_j, hh)
            run = jnp.zeros((tq, 1), F32) if masked else run_ref[hh]
            w_ref[1 - cur, hh] = _sb_weight(zs[hh], incls[hh], run, mask)
            run_ref[hh] = run + incls[hh][:, 0:1]

    first_slot = (i * (i + 1) // 2) % 2

    @pl.when(i == 0)
    def _():
        for hh in range(heads):
            z_ref[0, hh] = logits(0, 0, hh)
        w_ref[0] = jnp.zeros_like(w_ref[0])
        acc_ref[...] = jnp.zeros_like(acc_ref)

    for parity in range(2):
        @pl.when(first_slot == parity)
        def _():
            add_weighted_values(parity, 0)
            finalize(jnp.maximum(i - 1, 0))
            acc_ref[...] = jnp.zeros_like(acc_ref)
            step(i, i, parity, True)

    def body(n, carry):
        j = i - 1 - n
        for parity in range(2):
            @pl.when((first_slot + 1 + n) % 2 == parity)
            def _():
                add_weighted_values(parity, j + 1)
                step(i, j, parity, False)
        return carry

    lax.fori_loop(0, i, body, 0)

    @pl.when(i == nq - 1)
    def _():
        for parity in range(2):
            @pl.when((first_slot + 1 + i) % 2 == parity)
            def _():
                add_weighted_values(parity, 0)
        finalize(i)


def _sb_prompt(qa, kt, va, sga):
    B, T, _ = qa.shape
    tq = KEY_TILE
    nkt = T // KEY_TILE
    v4 = va.reshape(B, nkt, KEY_TILE, GROUP_W)
    row_spec = pl.BlockSpec((1, T, GROUP_W), lambda b, i: (b, 0, 0))
    tile_spec = lambda shape: pl.BlockSpec((1,) + shape, lambda b, i: (b, 0, 0, 0))
    return pl.pallas_call(
        functools.partial(_sb_prompt_kernel, heads=SB_HEADS),
        grid=(B, T // tq),
        in_specs=[row_spec, tile_spec((nkt, GROUP_W, KEY_TILE)), tile_spec((nkt, KEY_TILE, GROUP_W)),
                  row_spec],
        out_specs=row_spec,
        out_shape=jax.ShapeDtypeStruct((B, T, GROUP_W), BF16),
        scratch_shapes=[pltpu.VMEM((2, SB_HEADS, tq, KEY_TILE), F32),
                        pltpu.VMEM((2, SB_HEADS, tq, KEY_TILE), BF16),
                        pltpu.VMEM((SB_HEADS, tq, 1), F32),
                        pltpu.VMEM((SB_HEADS, tq, SB_HEAD_DIM), F32)],
        compiler_params=pltpu.CompilerParams(
            dimension_semantics=("arbitrary", "arbitrary"), vmem_limit_bytes=VMEM_LIMIT),
    )(qa, kt, v4, sga)


def _sb_decode_kernel(q_ref, kn_ref, vn_ref, ck_ref, cv_ref, g_ref, o_ref):
    tq = q_ref.shape[1]
    past = ck_ref.shape[4]
    suffix = _suffix_matrix(KEY_TILE)
    suffix_new = _suffix_matrix(tq)
    r = lax.broadcasted_iota(jnp.int32, (tq, tq), 0)
    c = lax.broadcasted_iota(jnp.int32, (tq, tq), 1)
    causal = c < r
    heads = range(HEADS_PER_STEP)
    head_lanes = [slice(hh * SB_HEAD_DIM, (hh + 1) * SB_HEAD_DIM) for hh in heads]
    qs = [q_ref[0, :, lanes] for lanes in head_lanes]
    runs = [jnp.zeros((tq, 1), F32)] * HEADS_PER_STEP
    zs = [_dot_nt(q, kn_ref[0, :, lanes]) for q, lanes in zip(qs, head_lanes)]
    ws, runs = _sb_weights(zs, suffix_new, runs, causal)
    accs = [_dot(w, vn_ref[0, :, lanes]) for w, lanes in zip(ws, head_lanes)]
    for j in reversed(range(past // KEY_TILE)):
        keys = slice(j * KEY_TILE, (j + 1) * KEY_TILE)
        zs = [_dot(q, ck_ref[0, 0, hh, :, keys].astype(BF16)) for q, hh in zip(qs, heads)]
        ws, runs = _sb_weights(zs, suffix, runs, None)
        accs = [acc + _dot_nt(w, cv_ref[0, 0, hh, :, keys].astype(BF16))
                for w, hh, acc in zip(ws, heads, accs)]
    o = jnp.concatenate(accs, axis=1) * g_ref[0].astype(F32)
    o_ref[0] = o.astype(o_ref.dtype)


def _sb_decode(qa, kn, vn, cache_k, cache_v, layer, sga):
    B, T, _ = qa.shape
    past = cache_k.shape[3]
    assert past % KEY_TILE == 0
    lanes = HEADS_PER_STEP * SB_HEAD_DIM
    row_spec = pl.BlockSpec((1, T, lanes), lambda b, h: (b, 0, h))
    cache_spec = pl.BlockSpec((1, 1, HEADS_PER_STEP, SB_HEAD_DIM, past),
                              lambda b, h: (layer, b, h, 0, 0))
    cache_k = jnp.swapaxes(cache_k, 3, 4)
    cache_v = jnp.swapaxes(cache_v, 3, 4)
    return pl.pallas_call(
        _sb_decode_kernel,
        grid=(B, GROUP_W // lanes),
        in_specs=[row_spec, row_spec, row_spec, cache_spec, cache_spec, row_spec],
        out_specs=row_spec,
        out_shape=jax.ShapeDtypeStruct((B, T, GROUP_W), BF16),
        compiler_params=pltpu.CompilerParams(
            dimension_semantics=("arbitrary", "arbitrary"), vmem_limit_bytes=VMEM_LIMIT),
    )(qa, kn, vn, cache_k, cache_v, sga)


def _hgrn_tables(C):
    nl = int(math.log2(C))
    assert 1 << nl == C
    n_small = min(nl, int(math.log2(SUBLANES)))
    t = np.arange(C)[:, None]
    j = np.arange(C)[None, :]
    blocks = []
    for l in range(n_small):
        half = 1 << l
        mid = (t >> (l + 1) << (l + 1)) + half
        right = (t & half) != 0
        blocks.append(np.where(right, (j >= mid) & (j <= t), (j > t) & (j < mid)))
    blocks.append(j <= t)
    prefix = np.concatenate(blocks, axis=0).astype(np.float32)
    x = t ^ j
    msb = np.floor(np.log2(np.maximum(x, 1))).astype(np.int32)
    level_of = np.where(t > j, msb, np.where(t == j, nl, -1)).astype(np.int32)
    return jnp.asarray(prefix, BF16), jnp.asarray(level_of), nl, n_small


def _hgrn_kernel(*refs, C, sub, nl, n_small, layer, has_s0):
    if has_s0:
        (qh_ref, fh_ref, ih_ref, gh_ref, pm_ref, lv_ref, lbl_ref, ng_ref, s0_ref,
         o_ref, so_ref, st_ref, b_ref) = refs
    else:
        (qh_ref, fh_ref, ih_ref, gh_ref, pm_ref, lv_ref, lbl_ref, ng_ref,
         o_ref, so_ref, st_ref, b_ref) = refs
    t = pl.program_id(1)

    @pl.when(t == 0)
    def _():
        if has_s0:
            for h in range(HG_HEADS):
                st_ref[h] = s0_ref[0, 0, h].T
        else:
            st_ref[...] = jnp.zeros_like(st_ref)

    logits = lbl_ref[...]
    e = jnp.exp(logits - jnp.max(logits, axis=0, keepdims=True))
    lb = jnp.sum(e[:layer + 1], axis=0, keepdims=True) / jnp.sum(e, axis=0, keepdims=True)

    def head(a, h):
        return a[:, h * HG_HEAD_DIM:(h + 1) * HG_HEAD_DIM]

    def prepare(c):
        rows = slice(c * C, (c + 1) * C)
        f = lb + (1.0 - lb) * _sigmoid(fh_ref[0, rows, :])
        g = jnp.log2(f)
        kin = 1.0 - f
        qh = qh_ref[0, rows, :].astype(F32)
        qs = qh * _sigmoid(qh)
        g_hi, g_lo = _split_bf16(g, 2)
        pre = _dot(pm_ref[:n_small * C, :], g_hi)
        cum = _dot(pm_ref[n_small * C:, :], jnp.concatenate([g_hi, g_lo], axis=1))
        b_ref[c] = cum[:, :GROUP_W] + cum[:, GROUP_W:]
        return dict(rows=rows, qs=qs, kin=kin, pre=pre, qs_b=qs.astype(BF16),
                    kin_b=kin.astype(BF16))

    def level_log(c, pre, l):
        if l < n_small:
            return pre[l * C:(l + 1) * C]
        half = 1 << l
        parts = []
        for lo in range(0, C, 2 * half):
            mid = lo + half
            last_left = b_ref[c, mid - 1:mid, :]
            parts.append(last_left - b_ref[c, lo:mid, :])
            parts.append(b_ref[c, mid:mid + half, :] - last_left)
        return jnp.concatenate(parts, axis=0)

    def operands(c, p, l):
        if l == nl:
            return p["qs_b"], p["kin_b"]
        e_l = jnp.exp2(level_log(c, p["pre"], l)).astype(BF16)
        return p["qs_b"] * e_l, p["kin_b"] * e_l

    def finish(c, p, scores):
        rows = p["rows"]
        b = b_ref[c]
        b_last = b_ref[c, C - 1:C, :]
        q_in = (p["qs"] * jnp.exp2(b)).astype(BF16)
        k_out = (p["kin"] * jnp.exp2(b_last - b)).astype(BF16)
        chunk_decay = jnp.exp2(b_last)
        i_b = ih_ref[0, rows, :]
        gh = gh_ref[0, rows, :].astype(F32)
        gate = ng_ref[...] * (gh * _sigmoid(gh))
        outs = []
        for h in range(HG_HEADS):
            st = st_ref[h]
            o = (_dot(scores[h].astype(BF16), head(i_b, h))
                 + _dot_nt(head(q_in, h), st.astype(BF16)))
            st_ref[h] = st * head(chunk_decay, h) + _dot_tn(head(i_b, h), head(k_out, h))
            outs.append(o * lax.rsqrt(jnp.mean(o * o, axis=-1, keepdims=True) + RMS_EPS))
        o_ref[0, rows, :] = (jnp.concatenate(outs, axis=1) * gate).astype(o_ref.dtype)

    chunks = range(sub)
    preps = [prepare(c) for c in chunks]
    level_of = lv_ref[...]
    order = [(l, c) for l in [nl] + list(range(nl)) for c in chunks]
    scores = [[0.0] * HG_HEADS for _ in chunks]
    nxt = operands(order[0][1], preps[order[0][1]], order[0][0])
    for idx, (l, c) in enumerate(order):
        q_l, k_l = nxt
        if idx + 1 < len(order):
            l_n, c_n = order[idx + 1]
            nxt = operands(c_n, preps[c_n], l_n)
        hit = level_of == l
        for h in range(HG_HEADS):
            scores[c][h] = jnp.where(hit, _dot_nt(head(q_l, h), head(k_l, h)), scores[c][h])
    for c in chunks:
        finish(c, preps[c], scores[c])

    @pl.when(t == pl.num_programs(1) - 1)
    def _():
        for h in range(HG_HEADS):
            so_ref[0, 0, h] = st_ref[h].T


def _hgrn(qh, fh, ih, gh, lb_logits, norm_g, s0, layer, *, C, sub):
    B, T, _ = qh.shape
    prefix, level_of, nl, n_small = _hgrn_tables(C)
    has_s0 = s0 is not None
    row_spec = pl.BlockSpec((1, sub * C, GROUP_W), lambda b, t: (b, t, 0))
    const = lambda shape: pl.BlockSpec(shape, lambda b, t: (0,) * len(shape))
    state_shape = (1, 1, HG_HEADS, HG_HEAD_DIM, HG_HEAD_DIM)
    in_specs = [row_spec, row_spec, row_spec, row_spec, const(prefix.shape), const(level_of.shape),
                const(lb_logits.shape), const((1, GROUP_W))]
    args = [qh, fh, ih, gh, prefix, level_of, lb_logits.astype(F32),
            norm_g.reshape(1, GROUP_W).astype(F32)]
    if has_s0:
        in_specs.append(pl.BlockSpec(state_shape, lambda b, t: (layer, b, 0, 0, 0)))
        args.append(s0)
    return pl.pallas_call(
        functools.partial(_hgrn_kernel, C=C, sub=sub, nl=nl, n_small=n_small, layer=layer,
                          has_s0=has_s0),
        grid=(B, T // (sub * C)),
        in_specs=in_specs,
        out_specs=[row_spec, pl.BlockSpec(state_shape, lambda b, t: (0, b, 0, 0, 0))],
        out_shape=[jax.ShapeDtypeStruct((B, T, GROUP_W), BF16),
                   jax.ShapeDtypeStruct((1, B, HG_HEADS, HG_HEAD_DIM, HG_HEAD_DIM), F32)],
        scratch_shapes=[pltpu.VMEM((HG_HEADS, HG_HEAD_DIM, HG_HEAD_DIM), F32),
                        pltpu.VMEM((sub, C, GROUP_W), F32)],
        compiler_params=pltpu.CompilerParams(
            dimension_semantics=("arbitrary", "arbitrary"), vmem_limit_bytes=VMEM_LIMIT),
    )(*args)


def _out_kernel(ma_ref, mb_ref, x_ref, w_ref, g_ref, b_ref, y_ref, *, alpha):
    out = _dot(ma_ref[...], w_ref[:GROUP_W, :]) + _dot(mb_ref[...], w_ref[GROUP_W:, :])
    hid = alpha * x_ref[...] + out
    mu = jnp.mean(hid, axis=-1, keepdims=True)
    cen = hid - mu
    var = jnp.mean(cen * cen, axis=-1, keepdims=True)
    y_ref[...] = cen * lax.rsqrt(var + LN_EPS) * g_ref[...] + b_ref[...]


def _out_project(ma, mb, x, w_bf, ln_g, ln_b, *, tm, alpha):
    B, T, _ = x.shape
    n = B * T
    mix_spec = pl.BlockSpec((tm, GROUP_W), lambda i: (i, 0))
    x_spec = pl.BlockSpec((tm, D_MODEL), lambda i: (i, 0))
    const = lambda shape: pl.BlockSpec(shape, lambda i: (0, 0))
    y = pl.pallas_call(
        functools.partial(_out_kernel, alpha=alpha),
        grid=(n // tm,),
        in_specs=[mix_spec, mix_spec, x_spec, const((2 * GROUP_W, D_MODEL)), const((1, D_MODEL)),
                  const((1, D_MODEL))],
        out_specs=x_spec,
        out_shape=jax.ShapeDtypeStruct((n, D_MODEL), F32),
        compiler_params=pltpu.CompilerParams(
            dimension_semantics=("arbitrary",), vmem_limit_bytes=VMEM_LIMIT),
    )(ma.reshape(n, GROUP_W), mb.reshape(n, GROUP_W), x.reshape(n, D_MODEL), w_bf,
      ln_g.reshape(1, D_MODEL).astype(F32), ln_b.reshape(1, D_MODEL).astype(F32))
    return y.reshape(B, T, D_MODEL)


PROMPT_ROWS = 512
PROMPT_CHUNK = 128
PROMPT_CHUNKS_PER_STEP = 4


def _layer(x, cache_k, cache_v, state_s, layer, w_in_bf, w_out_bf, lb_logits, norm_g, ln_g, ln_b,
           alpha):
    B, T, _ = x.shape
    decode = cache_k is not None
    if decode:
        qa, k, va, sga, qh, fh, ih, gh, k_out, v_out = _project(
            x, w_in_bf, nb=B, tm=T, k_transposed=False)
        ma = _sb_decode(qa, k, va, cache_k, cache_v, layer, sga)
        mb, s_out = _hgrn(qh, fh, ih, gh, lb_logits, norm_g, state_s, layer, C=T, sub=1)
        y = _out_project(ma, mb, x, w_out_bf, ln_g, ln_b, tm=B * T, alpha=alpha)
    else:
        qa, kt, va, sga, qh, fh, ih, gh, k_out, v_out = _project(
            x, w_in_bf, nb=1, tm=min(PROMPT_ROWS, T), k_transposed=True)
        ma = _sb_prompt(qa, kt, va, sga)
        chunk = min(PROMPT_CHUNK, T)
        mb, s_out = _hgrn(qh, fh, ih, gh, lb_logits, norm_g, None, layer, C=chunk,
                          sub=min(PROMPT_CHUNKS_PER_STEP, T // chunk))
        y = _out_project(ma, mb, x, w_out_bf, ln_g, ln_b, tm=min(PROMPT_ROWS, T), alpha=alpha)
    return y, k_out, v_out, s_out


def kernel(x_prompt, x_sample, cache_k, cache_v, state_s, w_in, w_out, lb_logits, hgrn_norm_g,
           ln_g, ln_b):
    depth = w_in.shape[0]
    alpha = (2 * depth) ** 0.25
    yp, ys = x_prompt, x_sample
    per_layer = []
    for l in range(depth):
        w_in_bf = w_in[l].astype(BF16)
        w_out_bf = w_out[l].astype(BF16)
        common = (l, w_in_bf, w_out_bf, lb_logits, hgrn_norm_g[l], ln_g[l], ln_b[l], alpha)
        yp, kp, vp, sp = _layer(yp, None, None, None, *common)
        ys, kn, vn, sn = _layer(ys, cache_k, cache_v, state_s, *common)
        per_layer.append((kp, vp, sp, kn, vn, sn))
    stack = lambda i: (per_layer[0][i] if depth == 1
                       else jnp.concatenate([p[i] for p in per_layer], axis=0))
    return (yp, ys, stack(0), stack(1), stack(2), stack(3), stack(4), stack(5))
```

```python
import functools
import math

import numpy as np
import jax
import jax.numpy as jnp
from jax import lax
from jax.experimental import pallas as pl
from jax.experimental.pallas import tpu as pltpu

F32 = jnp.float32
BF16 = jnp.bfloat16

D_MODEL = 1024
GROUP_W = 512
N_GROUPS = 8
SB_HEADS = 8
SB_HEAD_DIM = 64
HG_HEADS = 4
HG_HEAD_DIM = 128
LN_EPS = 1e-5
RMS_EPS = 1e-6

SUBLANES = 8
KEY_TILE = 256
HEADS_PER_STEP = 8
LOG2E = 1.4426950408889634
VMEM_LIMIT = 56 * 1024 * 1024


def _sigmoid(x):
    return 1.0 / (1.0 + jnp.exp2(x * (-LOG2E)))


def _dot(a, b):
    return jnp.dot(a, b, preferred_element_type=F32)


def _dot_nt(a, b):
    return lax.dot_general(a, b, (((1,), (1,)), ((), ())), preferred_element_type=F32)


def _dot_tn(a, b):
    return lax.dot_general(a, b, (((0,), (0,)), ((), ())), preferred_element_type=F32)


def _split_bf16(x, parts):
    out = []
    for _ in range(parts - 1):
        p = x.astype(BF16)
        out.append(p)
        x = x - p.astype(F32)
    out.append(x.astype(BF16))
    return out


def _proj_kernel(x_ref, w_ref, qa_ref, k_ref, va_ref, sga_ref, qh_ref, fh_ref, ih_ref, gh_ref,
                 ko_ref, vo_ref, *, nb, tm, k_transposed):
    m = nb * tm
    x = x_ref[...].reshape(m, D_MODEL).astype(BF16)

    def col(c):
        return _dot(x, w_ref[:, c * GROUP_W:(c + 1) * GROUP_W].astype(BF16))

    def put(ref, val):
        ref[...] = val.reshape(nb, tm, GROUP_W).astype(ref.dtype)

    def put_heads(ref, val):
        if k_transposed:
            val_t = val.T
            for h in range(SB_HEADS):
                ref[0, 0, h] = val_t[h * SB_HEAD_DIM:(h + 1) * SB_HEAD_DIM, :]
            return val_t
        for b in range(nb):
            for h in range(SB_HEADS):
                ref[0, b, h] = val[b * tm:(b + 1) * tm, h * SB_HEAD_DIM:(h + 1) * SB_HEAD_DIM]

    put(qa_ref, col(0) * (SB_HEAD_DIM ** -0.5 * LOG2E))
    k = col(1)
    kt = put_heads(ko_ref, k)
    if k_transposed:
        kt = kt.astype(BF16)
        for j in range(m // KEY_TILE):
            k_ref[0, j] = kt[:, j * KEY_TILE:(j + 1) * KEY_TILE]
    else:
        put(k_ref, k)
    v = col(2)
    put_heads(vo_ref, v)
    put(va_ref, v)
    ga = col(3)
    put(sga_ref, ga * _sigmoid(ga))
    put(qh_ref, col(4))
    put(fh_ref, col(5))
    put(ih_ref, col(6))
    put(gh_ref, col(7))


def _project(x, w, *, nb, tm, k_transposed):
    B, T, _ = x.shape
    grid = (B // nb, T // tm)
    act = lambda dt: jax.ShapeDtypeStruct((B, T, GROUP_W), dt)
    act_spec = pl.BlockSpec((nb, tm, GROUP_W), lambda b, t: (b, t, 0))
    if k_transposed:
        assert nb == 1 and tm % KEY_TILE == 0
        k_shape = jax.ShapeDtypeStruct((B, T // KEY_TILE, GROUP_W, KEY_TILE), BF16)
        k_spec = pl.BlockSpec((1, tm // KEY_TILE, GROUP_W, KEY_TILE), lambda b, t: (b, t, 0, 0))
        kv_shape = jax.ShapeDtypeStruct((1, B, SB_HEADS, SB_HEAD_DIM, T), F32)
        kv_spec = pl.BlockSpec((1, 1, SB_HEADS, SB_HEAD_DIM, tm), lambda b, t: (0, b, 0, 0, t))
    else:
        k_shape, k_spec = act(BF16), act_spec
        kv_shape = jax.ShapeDtypeStruct((1, B, SB_HEADS, T, SB_HEAD_DIM), F32)
        kv_spec = pl.BlockSpec((1, nb, SB_HEADS, tm, SB_HEAD_DIM), lambda b, t: (0, b, 0, t, 0))
    outs = pl.pallas_call(
        functools.partial(_proj_kernel, nb=nb, tm=tm, k_transposed=k_transposed),
        grid=grid,
        in_specs=[pl.BlockSpec((nb, tm, D_MODEL), lambda b, t: (b, t, 0)),
                  pl.BlockSpec((D_MODEL, N_GROUPS * GROUP_W), lambda b, t: (0, 0),
                               pipeline_mode=pl.Buffered(1))],
        out_specs=[act_spec, k_spec, act_spec, act_spec, act_spec, act_spec, act_spec, act_spec,
                   kv_spec, kv_spec],
        out_shape=[act(BF16), k_shape, act(BF16), act(BF16), act(BF16), act(F32), act(BF16),
                   act(BF16), kv_shape, kv_shape],
        compiler_params=pltpu.CompilerParams(
            dimension_semantics=("arbitrary", "arbitrary"), vmem_limit_bytes=VMEM_LIMIT),
    )(x, w)
    if k_transposed:
        outs = list(outs[:8]) + [jnp.swapaxes(o, 3, 4) for o in outs[8:]]
    return outs


def _suffix_matrix(n):
    r = lax.broadcasted_iota(jnp.int32, (n, n), 0)
    c = lax.broadcasted_iota(jnp.int32, (n, n), 1)
    return jnp.where(r >= c, 1.0, 0.0).astype(BF16)


def _sb_weights(zs, suffix, runs, causal):
    incls = _sb_suffix_sums(zs, suffix, causal)
    ws = [_sb_weight(z, incl, run, causal) for z, incl, run in zip(zs, incls, runs)]
    return ws, [run + incl[:, 0:1] for run, incl in zip(runs, incls)]


def _sb_suffix_sums(zs, suffix, causal):
    drops = []
    for z in zs:
        drop = jnp.maximum(z, 0.0) + jnp.log2(1.0 + jnp.exp2(-jnp.abs(z)))
        if causal is not None:
            drop = jnp.where(causal, drop, 0.0)
        drops.append(drop.astype(BF16))
    return [_dot(drop, suffix) for drop in drops]


def _sb_weight(z, incl, run, causal):
    w = jnp.exp2(z - incl - run)
    if causal is not None:
        w = jnp.where(causal, w, 0.0)
    return w.astype(BF16)


def _sb_prompt_kernel(q_ref, kt_ref, v_ref, g_ref, o_ref, z_ref, w_ref, run_ref, acc_ref, *, heads):
    tq = KEY_TILE
    nq = q_ref.shape[1] // tq
    i = pl.program_id(1)
    suffix = _suffix_matrix(KEY_TILE)
    r = lax.broadcasted_iota(jnp.int32, (tq, KEY_TILE), 0)
    c = lax.broadcasted_iota(jnp.int32, (tq, KEY_TILE), 1)
    causal = c < r
    head_lanes = [slice(hh * SB_HEAD_DIM, (hh + 1) * SB_HEAD_DIM) for hh in range(heads)]

    def block_rows(bi):
        return pl.ds(pl.multiple_of(bi * tq, tq), tq)

    def logits(bi, j, hh):
        return _dot(q_ref[0, block_rows(bi), head_lanes[hh]], kt_ref[0, j, head_lanes[hh], :])

    def add_weighted_values(slot, j):
        for hh in range(heads):
            acc_ref[hh] += _dot(w_ref[slot, hh], v_ref[0, j, :, head_lanes[hh]])

    def finalize(bi):
        o = jnp.concatenate([acc_ref[hh] for hh in range(heads)], axis=1)
        o_ref[0, block_rows(bi), :] = (o * g_ref[0, block_rows(bi), :].astype(F32)).astype(o_ref.dtype)

    def step(bi, j, cur, masked):
        mask = causal if masked else None
        zs = [z_ref[cur, hh] for hh in range(heads)]
        incls = _sb_suffix_sums(zs, suffix, mask)
        same_block = j > 0
        nxt = jnp.minimum(bi + 1, nq - 1)
        next_bi = jnp.where(same_block, bi, nxt)
        next_j = jnp.where(same_block, j - 1, nxt)
        for hh in range(heads):
            z_ref[1 - cur, hh] = logits(next_bi, next_j, hh)
            run = jnp.zeros((tq, 1), F32) if masked else run_ref[hh]
            w_ref[1 - cur, hh] = _sb_weight(zs[hh], incls[hh], run, mask)
            run_ref[hh] = run + incls[hh][:, 0:1]

    first_slot = (i * (i + 1) // 2) % 2

    @pl.when(i == 0)
    def _():
        for hh in range(heads):
            z_ref[0, hh] = logits(0, 0, hh)
        w_ref[0] = jnp.zeros_like(w_ref[0])
        acc_ref[...] = jnp.zeros_like(acc_ref)

    for parity in range(2):
        @pl.when(first_slot == parity)
        def _():
            add_weighted_values(parity, 0)
            finalize(jnp.maximum(i - 1, 0))
            acc_ref[...] = jnp.zeros_like(acc_ref)
            step(i, i, parity, True)

    def body(n, carry):
        j = i - 1 - n
        for parity in range(2):
            @pl.when((first_slot + 1 + n) % 2 == parity)
            def _():
                add_weighted_values(parity, j + 1)
                step(i, j, parity, False)
        return carry

    lax.fori_loop(0, i, body, 0)

    @pl.when(i == nq - 1)
    def _():
        for parity in range(2):
            @pl.when((first_slot + 1 + i) % 2 == parity)
            def _():
                add_weighted_values(parity, 0)
        finalize(i)


def _sb_prompt(qa, kt, va, sga):
    B, T, _ = qa.shape
    tq = KEY_TILE
    nkt = T // KEY_TILE
    v4 = va.reshape(B, nkt, KEY_TILE, GROUP_W)
    row_spec = pl.BlockSpec((1, T, GROUP_W), lambda b, i: (b, 0, 0))
    tile_spec = lambda shape: pl.BlockSpec((1,) + shape, lambda b, i: (b, 0, 0, 0))
    return pl.pallas_call(
        functools.partial(_sb_prompt_kernel, heads=SB_HEADS),
        grid=(B, T // tq),
        in_specs=[row_spec, tile_spec((nkt, GROUP_W, KEY_TILE)), tile_spec((nkt, KEY_TILE, GROUP_W)),
                  row_spec],
        out_specs=row_spec,
        out_shape=jax.ShapeDtypeStruct((B, T, GROUP_W), BF16),
        scratch_shapes=[pltpu.VMEM((2, SB_HEADS, tq, KEY_TILE), F32),
                        pltpu.VMEM((2, SB_HEADS, tq, KEY_TILE), BF16),
                        pltpu.VMEM((SB_HEADS, tq, 1), F32),
                        pltpu.VMEM((SB_HEADS, tq, SB_HEAD_DIM), F32)],
        compiler_params=pltpu.CompilerParams(
            dimension_semantics=("arbitrary", "arbitrary"), vmem_limit_bytes=VMEM_LIMIT),
    )(qa, kt, v4, sga)


def _sb_decode_kernel(q_ref, kn_ref, vn_ref, ck_ref, cv_ref, g_ref, o_ref):
    tq = q_ref.shape[1]
    past = ck_ref.shape[4]
    suffix = _suffix_matrix(KEY_TILE)
    suffix_new = _suffix_matrix(tq)
    r = lax.broadcasted_iota(jnp.int32, (tq, tq), 0)
    c = lax.broadcasted_iota(jnp.int32, (tq, tq), 1)
    causal = c < r
    heads = range(HEADS_PER_STEP)
    head_lanes = [slice(hh * SB_HEAD_DIM, (hh + 1) * SB_HEAD_DIM) for hh in heads]
    qs = [q_ref[0, :, lanes] for lanes in head_lanes]
    runs = [jnp.zeros((tq, 1), F32)] * HEADS_PER_STEP
    zs = [_dot_nt(q, kn_ref[0, :, lanes]) for q, lanes in zip(qs, head_lanes)]
    ws, runs = _sb_weights(zs, suffix_new, runs, causal)
    accs = [_dot(w, vn_ref[0, :, lanes]) for w, lanes in zip(ws, head_lanes)]
    for j in reversed(range(past // KEY_TILE)):
        keys = slice(j * KEY_TILE, (j + 1) * KEY_TILE)
        zs = [_dot(q, ck_ref[0, 0, hh, :, keys].astype(BF16)) for q, hh in zip(qs, heads)]
        ws, runs = _sb_weights(zs, suffix, runs, None)
        accs = [acc + _dot_nt(w, cv_ref[0, 0, hh, :, keys].astype(BF16))
                for w, hh, acc in zip(ws, heads, accs)]
    o = jnp.concatenate(accs, axis=1) * g_ref[0].astype(F32)
    o_ref[0] = o.astype(o_ref.dtype)


def _sb_decode(qa, kn, vn, cache_k, cache_v, layer, sga):
    B, T, _ = qa.shape
    past = cache_k.shape[3]
    assert past % KEY_TILE == 0
    lanes = HEADS_PER_STEP * SB_HEAD_DIM
    row_spec = pl.BlockSpec((1, T, lanes), lambda b, h: (b, 0, h))
    cache_spec = pl.BlockSpec((1, 1, HEADS_PER_STEP, SB_HEAD_DIM, past),
                              lambda b, h: (layer, b, h, 0, 0))
    cache_k = jnp.swapaxes(cache_k, 3, 4)
    cache_v = jnp.swapaxes(cache_v, 3, 4)
    return pl.pallas_call(
        _sb_decode_kernel,
        grid=(B, GROUP_W // lanes),
        in_specs=[row_spec, row_spec, row_spec, cache_spec, cache_spec, row_spec],
        out_specs=row_spec,
        out_shape=jax.ShapeDtypeStruct((B, T, GROUP_W), BF16),
        compiler_params=pltpu.CompilerParams(
            dimension_semantics=("arbitrary", "arbitrary"), vmem_limit_bytes=VMEM_LIMIT),
    )(qa, kn, vn, cache_k, cache_v, sga)


def _hgrn_tables(C):
    nl = int(math.log2(C))
    assert 1 << nl == C
    n_small = min(nl, int(math.log2(SUBLANES)))
    t = np.arange(C)[:, None]
    j = np.arange(C)[None, :]
    blocks = []
    for l in range(n_small):
        half = 1 << l
        mid = (t >> (l + 1) << (l + 1)) + half
        right = (t & half) != 0
        blocks.append(np.where(right, (j >= mid) & (j <= t), (j > t) & (j < mid)))
    blocks.append(j <= t)
    prefix = np.concatenate(blocks, axis=0).astype(np.float32)
    x = t ^ j
    msb = np.floor(np.log2(np.maximum(x, 1))).astype(np.int32)
    level_of = np.where(t > j, msb, np.where(t == j, nl, -1)).astype(np.int32)
    return jnp.asarray(prefix, BF16), jnp.asarray(level_of), nl, n_small


def _hgrn_out_kernel(*refs, C, sub, nl, n_small, layer, has_s0, alpha):
    (qh_ref, fh_ref, ih_ref, gh_ref, pm_ref, lv_ref, lbl_ref, ng_ref, ma_ref, x_ref, wo_ref,
     lg_ref, lnb_ref) = refs[:13]
    if has_s0:
        s0_ref, y_ref, so_ref, st_ref, b_ref = refs[13:]
    else:
        y_ref, so_ref, st_ref, b_ref = refs[13:]
    t = pl.program_id(1)

    w_attn = wo_ref[:GROUP_W, :].astype(BF16)
    w_hgrn = wo_ref[GROUP_W:, :].astype(BF16)
    out_attn = [_dot(ma_ref[0, c * C:(c + 1) * C, :], w_attn) for c in range(sub)]

    @pl.when(t == 0)
    def _():
        if has_s0:
            for h in range(HG_HEADS):
                st_ref[h] = s0_ref[0, 0, h].T
        else:
            st_ref[...] = jnp.zeros_like(st_ref)

    logits = lbl_ref[...]
    e = jnp.exp(logits - jnp.max(logits, axis=0, keepdims=True))
    lb = jnp.sum(e[:layer + 1], axis=0, keepdims=True) / jnp.sum(e, axis=0, keepdims=True)

    def head(a, h):
        return a[:, h * HG_HEAD_DIM:(h + 1) * HG_HEAD_DIM]

    def prepare(c):
        rows = slice(c * C, (c + 1) * C)
        f = lb + (1.0 - lb) * _sigmoid(fh_ref[0, rows, :])
        g = jnp.log2(f)
        kin = 1.0 - f
        qh = qh_ref[0, rows, :].astype(F32)
        qs = qh * _sigmoid(qh)
        g_hi, g_lo = _split_bf16(g, 2)
        pre = _dot(pm_ref[:n_small * C, :], g_hi)
        cum = _dot(pm_ref[n_small * C:, :], jnp.concatenate([g_hi, g_lo], axis=1))
        b_ref[c] = cum[:, :GROUP_W] + cum[:, GROUP_W:]
        return dict(rows=rows, qs=qs, kin=kin, pre=pre, qs_b=qs.astype(BF16),
                    kin_b=kin.astype(BF16))

    def level_log(c, pre, l):
        if l < n_small:
            return pre[l * C:(l + 1) * C]
        half = 1 << l
        parts = []
        for lo in range(0, C, 2 * half):
            mid = lo + half
            last_left = b_ref[c, mid - 1:mid, :]
            parts.append(last_left - b_ref[c, lo:mid, :])
            parts.append(b_ref[c, mid:mid + half, :] - last_left)
        return jnp.concatenate(parts, axis=0)

    def operands(c, p, l):
        if l == nl:
            return p["qs_b"], p["kin_b"]
        e_l = jnp.exp2(level_log(c, p["pre"], l)).astype(BF16)
        return p["qs_b"] * e_l, p["kin_b"] * e_l

    def finish(c, p, scores):
        rows = p["rows"]
        b = b_ref[c]
        b_last = b_ref[c, C - 1:C, :]
        q_in = (p["qs"] * jnp.exp2(b)).astype(BF16)
        k_out = (p["kin"] * jnp.exp2(b_last - b)).astype(BF16)
        chunk_decay = jnp.exp2(b_last)
        i_b = ih_ref[0, rows, :]
        gh = gh_ref[0, rows, :].astype(F32)
        gate = ng_ref[...] * (gh * _sigmoid(gh))
        outs = []
        for h in range(HG_HEADS):
            st = st_ref[h]
            o = (_dot(scores[h].astype(BF16), head(i_b, h))
                 + _dot_nt(head(q_in, h), st.astype(BF16)))
            st_ref[h] = st * head(chunk_decay, h) + _dot_tn(head(i_b, h), head(k_out, h))
            outs.append(o * lax.rsqrt(jnp.mean(o * o, axis=-1, keepdims=True) + RMS_EPS))
        mixed_hgrn = (jnp.concatenate(outs, axis=1) * gate).astype(BF16)
        hid = alpha * x_ref[0, rows, :] + (out_attn[c] + _dot(mixed_hgrn, w_hgrn))
        mu = jnp.mean(hid, axis=-1, keepdims=True)
        cen = hid - mu
        var = jnp.mean(cen * cen, axis=-1, keepdims=True)
        y_ref[0, rows, :] = cen * lax.rsqrt(var + LN_EPS) * lg_ref[...] + lnb_ref[...]

    chunks = range(sub)
    preps = [prepare(c) for c in chunks]
    level_of = lv_ref[...]
    order = [(l, c) for l in [nl] + list(range(nl)) for c in chunks]
    scores = [[0.0] * HG_HEADS for _ in chunks]
    nxt = operands(order[0][1], preps[order[0][1]], order[0][0])
    for idx, (l, c) in enumerate(order):
        q_l, k_l = nxt
        if idx + 1 < len(order):
            l_n, c_n = order[idx + 1]
            nxt = operands(c_n, preps[c_n], l_n)
        hit = level_of == l
        for h in range(HG_HEADS):
            scores[c][h] = jnp.where(hit, _dot_nt(head(q_l, h), head(k_l, h)), scores[c][h])
    for c in chunks:
        finish(c, preps[c], scores[c])

    @pl.when(t == pl.num_programs(1) - 1)
    def _():
        for h in range(HG_HEADS):
            so_ref[0, 0, h] = st_ref[h].T


def _hgrn_out(qh, fh, ih, gh, lb_logits, norm_g, s0, layer, ma, x, w_out, ln_g, ln_b, *, C, sub,
              alpha):
    B, T, _ = qh.shape
    prefix, level_of, nl, n_small = _hgrn_tables(C)
    has_s0 = s0 is not None
    rows = sub * C
    row_spec = pl.BlockSpec((1, rows, GROUP_W), lambda b, t: (b, t, 0))
    wide_spec = pl.BlockSpec((1, rows, D_MODEL), lambda b, t: (b, t, 0))
    const = lambda shape: pl.BlockSpec(shape, lambda b, t: (0,) * len(shape))
    state_shape = (1, 1, HG_HEADS, HG_HEAD_DIM, HG_HEAD_DIM)
    in_specs = [row_spec, row_spec, row_spec, row_spec, const(prefix.shape), const(level_of.shape),
                const(lb_logits.shape), const((1, GROUP_W)), row_spec, wide_spec,
                const(w_out.shape), const((1, D_MODEL)), const((1, D_MODEL))]
    args = [qh, fh, ih, gh, prefix, level_of, lb_logits.astype(F32),
            norm_g.reshape(1, GROUP_W).astype(F32), ma, x, w_out,
            ln_g.reshape(1, D_MODEL).astype(F32), ln_b.reshape(1, D_MODEL).astype(F32)]
    if has_s0:
        in_specs.append(pl.BlockSpec(state_shape, lambda b, t: (layer, b, 0, 0, 0)))
        args.append(s0)
    return pl.pallas_call(
        functools.partial(_hgrn_out_kernel, C=C, sub=sub, nl=nl, n_small=n_small, layer=layer,
                          has_s0=has_s0, alpha=alpha),
        grid=(B, T // rows),
        in_specs=in_specs,
        out_specs=[wide_spec, pl.BlockSpec(state_shape, lambda b, t: (0, b, 0, 0, 0))],
        out_shape=[jax.ShapeDtypeStruct((B, T, D_MODEL), F32),
                   jax.ShapeDtypeStruct((1, B, HG_HEADS, HG_HEAD_DIM, HG_HEAD_DIM), F32)],
        scratch_shapes=[pltpu.VMEM((HG_HEADS, HG_HEAD_DIM, HG_HEAD_DIM), F32),
                        pltpu.VMEM((sub, C, GROUP_W), F32)],
        compiler_params=pltpu.CompilerParams(
            dimension_semantics=("arbitrary", "arbitrary"), vmem_limit_bytes=VMEM_LIMIT),
    )(*args)


PROMPT_ROWS = 512
PROMPT_CHUNK = 128
PROMPT_CHUNKS_PER_STEP = 4


def _layer(x, cache_k, cache_v, state_s, layer, w_in, w_out, lb_logits, norm_g, ln_g, ln_b, alpha):
    B, T, _ = x.shape
    decode = cache_k is not None
    if decode:
        qa, k, va, sga, qh, fh, ih, gh, k_out, v_out = _project(
            x, w_in, nb=B, tm=T, k_transposed=False)
        ma = _sb_decode(qa, k, va, cache_k, cache_v, layer, sga)
        y, s_out = _hgrn_out(qh, fh, ih, gh, lb_logits, norm_g, state_s, layer, ma, x, w_out,
                             ln_g, ln_b, C=T, sub=1, alpha=alpha)
    else:
        qa, kt, va, sga, qh, fh, ih, gh, k_out, v_out = _project(
            x, w_in, nb=1, tm=min(PROMPT_ROWS, T), k_transposed=True)
        ma = _sb_prompt(qa, kt, va, sga)
        chunk = min(PROMPT_CHUNK, T)
        y, s_out = _hgrn_out(qh, fh, ih, gh, lb_logits, norm_g, None, layer, ma, x, w_out, ln_g,
                             ln_b, C=chunk, sub=min(PROMPT_CHUNKS_PER_STEP, T // chunk),
                             alpha=alpha)
    return y, k_out, v_out, s_out


def kernel(x_prompt, x_sample, cache_k, cache_v, state_s, w_in, w_out, lb_logits, hgrn_norm_g,
           ln_g, ln_b):
    depth = w_in.shape[0]
    alpha = (2 * depth) ** 0.25
    yp, ys = x_prompt, x_sample
    per_layer = []
    for l in range(depth):
        common = (l, w_in[l], w_out[l], lb_logits, hgrn_norm_g[l], ln_g[l], ln_b[l], alpha)
        yp, kp, vp, sp = _layer(yp, None, None, None, *common)
        ys, kn, vn, sn = _layer(ys, cache_k, cache_v, state_s, *common)
        per_layer.append((kp, vp, sp, kn, vn, sn))
    stack = lambda i: (per_layer[0][i] if depth == 1
                       else jnp.concatenate([p[i] for p in per_layer], axis=0))
    return (yp, ys, stack(0), stack(1), stack(2), stack(3), stack(4), stack(5))
```

```python
import functools
import math

import numpy as np
import jax
import jax.numpy as jnp
from jax import lax
from jax.experimental import pallas as pl
from jax.experimental.pallas import tpu as pltpu

F32 = jnp.float32
BF16 = jnp.bfloat16

D_MODEL = 1024
GROUP_W = 512
N_GROUPS = 8
SB_HEADS = 8
SB_HEAD_DIM = 64
HG_HEADS = 4
HG_HEAD_DIM = 128
LN_EPS = 1e-5
RMS_EPS = 1e-6

SUBLANES = 8
KEY_TILE = 256
HEADS_PER_STEP = 8
LOG2E = 1.4426950408889634
VMEM_LIMIT = 56 * 1024 * 1024


def _sigmoid(x):
    return 1.0 / (1.0 + jnp.exp2(x * (-LOG2E)))


def _dot(a, b):
    return jnp.dot(a, b, preferred_element_type=F32)


def _dot_nt(a, b):
    return lax.dot_general(a, b, (((1,), (1,)), ((), ())), preferred_element_type=F32)


def _dot_tn(a, b):
    return lax.dot_general(a, b, (((0,), (0,)), ((), ())), preferred_element_type=F32)


def _split_bf16(x, parts):
    out = []
    for _ in range(parts - 1):
        p = x.astype(BF16)
        out.append(p)
        x = x - p.astype(F32)
    out.append(x.astype(BF16))
    return out


def _proj_kernel(x_ref, w_ref, qa_ref, k_ref, va_ref, sga_ref, qh_ref, fh_ref, ih_ref, gh_ref,
                 ko_ref, vo_ref, *, nb, tm, k_transposed):
    m = nb * tm
    x = x_ref[...].reshape(m, D_MODEL).astype(BF16)

    def col(c):
        return _dot(x, w_ref[:, c * GROUP_W:(c + 1) * GROUP_W].astype(BF16))

    def put(ref, val):
        ref[...] = val.reshape(nb, tm, GROUP_W).astype(ref.dtype)

    def put_heads(ref, val):
        if k_transposed:
            val_t = val.T
            for h in range(SB_HEADS):
                ref[0, 0, h] = val_t[h * SB_HEAD_DIM:(h + 1) * SB_HEAD_DIM, :]
            return val_t
        for b in range(nb):
            for h in range(SB_HEADS):
                ref[0, b, h] = val[b * tm:(b + 1) * tm, h * SB_HEAD_DIM:(h + 1) * SB_HEAD_DIM]

    put(qa_ref, col(0) * (SB_HEAD_DIM ** -0.5 * LOG2E))
    k = col(1)
    kt = put_heads(ko_ref, k)
    if k_transposed:
        kt = kt.astype(BF16)
        for j in range(m // KEY_TILE):
            k_ref[0, j] = kt[:, j * KEY_TILE:(j + 1) * KEY_TILE]
    else:
        put(k_ref, k)
    v = col(2)
    put_heads(vo_ref, v)
    put(va_ref, v)
    ga = col(3)
    put(sga_ref, ga * _sigmoid(ga))
    put(qh_ref, col(4))
    put(fh_ref, col(5))
    put(ih_ref, col(6))
    put(gh_ref, col(7))


def _project(x, w, *, nb, tm, k_transposed):
    B, T, _ = x.shape
    grid = (B // nb, T // tm)
    act = lambda dt: jax.ShapeDtypeStruct((B, T, GROUP_W), dt)
    act_spec = pl.BlockSpec((nb, tm, GROUP_W), lambda b, t: (b, t, 0))
    if k_transposed:
        assert nb == 1 and tm % KEY_TILE == 0
        k_shape = jax.ShapeDtypeStruct((B, T // KEY_TILE, GROUP_W, KEY_TILE), BF16)
        k_spec = pl.BlockSpec((1, tm // KEY_TILE, GROUP_W, KEY_TILE), lambda b, t: (b, t, 0, 0))
        kv_shape = jax.ShapeDtypeStruct((1, B, SB_HEADS, SB_HEAD_DIM, T), F32)
        kv_spec = pl.BlockSpec((1, 1, SB_HEADS, SB_HEAD_DIM, tm), lambda b, t: (0, b, 0, 0, t))
    else:
        k_shape, k_spec = act(BF16), act_spec
        kv_shape = jax.ShapeDtypeStruct((1, B, SB_HEADS, T, SB_HEAD_DIM), F32)
        kv_spec = pl.BlockSpec((1, nb, SB_HEADS, tm, SB_HEAD_DIM), lambda b, t: (0, b, 0, t, 0))
    outs = pl.pallas_call(
        functools.partial(_proj_kernel, nb=nb, tm=tm, k_transposed=k_transposed),
        grid=grid,
        in_specs=[pl.BlockSpec((nb, tm, D_MODEL), lambda b, t: (b, t, 0)),
                  pl.BlockSpec((D_MODEL, N_GROUPS * GROUP_W), lambda b, t: (0, 0),
                               pipeline_mode=pl.Buffered(1))],
        out_specs=[act_spec, k_spec, act_spec, act_spec, act_spec, act_spec, act_spec, act_spec,
                   kv_spec, kv_spec],
        out_shape=[act(BF16), k_shape, act(BF16), act(BF16), act(BF16), act(F32), act(BF16),
                   act(BF16), kv_shape, kv_shape],
        compiler_params=pltpu.CompilerParams(
            dimension_semantics=("arbitrary", "arbitrary"), vmem_limit_bytes=VMEM_LIMIT),
    )(x, w)
    if k_transposed:
        outs = list(outs[:8]) + [jnp.swapaxes(o, 3, 4) for o in outs[8:]]
    return outs


def _suffix_matrix(n):
    r = lax.broadcasted_iota(jnp.int32, (n, n), 0)
    c = lax.broadcasted_iota(jnp.int32, (n, n), 1)
    return jnp.where(r >= c, 1.0, 0.0).astype(BF16)


def _sb_weights(zs, suffix, runs, causal):
    incls = _sb_suffix_sums(zs, suffix, causal)
    ws = [_sb_weight(z, incl, run, causal) for z, incl, run in zip(zs, incls, runs)]
    return ws, [run + incl[:, 0:1] for run, incl in zip(runs, incls)]


def _sb_suffix_sums(zs, suffix, causal):
    drops = []
    for z in zs:
        drop = jnp.maximum(z, 0.0) + jnp.log2(1.0 + jnp.exp2(-jnp.abs(z)))
        if causal is not None:
            drop = jnp.where(causal, drop, 0.0)
        drops.append(drop.astype(BF16))
    return [_dot(drop, suffix) for drop in drops]


def _sb_weight(z, incl, run, causal):
    w = jnp.exp2(z - incl - run)
    if causal is not None:
        w = jnp.where(causal, w, 0.0)
    return w.astype(BF16)


def _sb_prompt_kernel(q_ref, kt_ref, v_ref, g_ref, o_ref, z_ref, w_ref, run_ref, acc_ref, *, heads):
    tq = KEY_TILE
    nq = q_ref.shape[1] // tq
    i = pl.program_id(1)
    suffix = _suffix_matrix(KEY_TILE)
    r = lax.broadcasted_iota(jnp.int32, (tq, KEY_TILE), 0)
    c = lax.broadcasted_iota(jnp.int32, (tq, KEY_TILE), 1)
    causal = c < r
    head_lanes = [slice(hh * SB_HEAD_DIM, (hh + 1) * SB_HEAD_DIM) for hh in range(heads)]

    def block_rows(bi):
        return pl.ds(pl.multiple_of(bi * tq, tq), tq)

    def logits(bi, j, hh):
        return _dot(q_ref[0, block_rows(bi), head_lanes[hh]], kt_ref[0, j, head_lanes[hh], :])

    def add_weighted_values(slot, j):
        for hh in range(heads):
            acc_ref[hh] += _dot(w_ref[slot, hh], v_ref[0, j, :, head_lanes[hh]])

    def finalize(bi):
        o = jnp.concatenate([acc_ref[hh] for hh in range(heads)], axis=1)
        o_ref[0, block_rows(bi), :] = (o * g_ref[0, block_rows(bi), :].astype(F32)).astype(o_ref.dtype)

    def step(bi, j, cur, masked):
        mask = causal if masked else None
        zs = [z_ref[cur, hh] for hh in range(heads)]
        incls = _sb_suffix_sums(zs, suffix, mask)
        same_block = j > 0
        nxt = jnp.minimum(bi + 1, nq - 1)
        next_bi = jnp.where(same_block, bi, nxt)
        next_j = jnp.where(same_block, j - 1, nxt)
        for hh in range(heads):
            z_ref[1 - cur, hh] = logits(next_bi, next_j, hh)
            run = jnp.zeros((tq, 1), F32) if masked else run_ref[hh]
            w_ref[1 - cur, hh] = _sb_weight(zs[hh], incls[hh], run, mask)
            run_ref[hh] = run + incls[hh][:, 0:1]

    first_slot = (i * (i + 1) // 2) % 2

    @pl.when(i == 0)
    def _():
        for hh in range(heads):
            z_ref[0, hh] = logits(0, 0, hh)
        w_ref[0] = jnp.zeros_like(w_ref[0])
        acc_ref[...] = jnp.zeros_like(acc_ref)

    for parity in range(2):
        @pl.when(first_slot == parity)
        def _():
            add_weighted_values(parity, 0)
            finalize(jnp.maximum(i - 1, 0))
            acc_ref[...] = jnp.zeros_like(acc_ref)
            step(i, i, parity, True)

    def pair_body(n, carry):
        j = i - 1 - 2 * n
        for parity in range(2):
            @pl.when((first_slot + 1) % 2 == parity)
            def _():
                add_weighted_values(parity, j + 1)
                step(i, j, parity, False)
                add_weighted_values(1 - parity, j)
                step(i, j - 1, 1 - parity, False)
        return carry

    lax.fori_loop(0, i // 2, pair_body, 0)

    @pl.when(i % 2 == 1)
    def _():
        for parity in range(2):
            @pl.when((first_slot + 1) % 2 == parity)
            def _():
                add_weighted_values(parity, 1)
                step(i, 0, parity, False)

    @pl.when(i == nq - 1)
    def _():
        for parity in range(2):
            @pl.when((first_slot + 1 + i) % 2 == parity)
            def _():
                add_weighted_values(parity, 0)
        finalize(i)


def _sb_prompt(qa, kt, va, sga):
    B, T, _ = qa.shape
    tq = KEY_TILE
    nkt = T // KEY_TILE
    v4 = va.reshape(B, nkt, KEY_TILE, GROUP_W)
    row_spec = pl.BlockSpec((1, T, GROUP_W), lambda b, i: (b, 0, 0))
    tile_spec = lambda shape: pl.BlockSpec((1,) + shape, lambda b, i: (b, 0, 0, 0))
    return pl.pallas_call(
        functools.partial(_sb_prompt_kernel, heads=SB_HEADS),
        grid=(B, T // tq),
        in_specs=[row_spec, tile_spec((nkt, GROUP_W, KEY_TILE)), tile_spec((nkt, KEY_TILE, GROUP_W)),
                  row_spec],
        out_specs=row_spec,
        out_shape=jax.ShapeDtypeStruct((B, T, GROUP_W), BF16),
        scratch_shapes=[pltpu.VMEM((2, SB_HEADS, tq, KEY_TILE), F32),
                        pltpu.VMEM((2, SB_HEADS, tq, KEY_TILE), BF16),
                        pltpu.VMEM((SB_HEADS, tq, 1), F32),
                        pltpu.VMEM((SB_HEADS, tq, SB_HEAD_DIM), F32)],
        compiler_params=pltpu.CompilerParams(
            dimension_semantics=("arbitrary", "arbitrary"), vmem_limit_bytes=VMEM_LIMIT),
    )(qa, kt, v4, sga)


def _sb_decode_kernel(q_ref, kn_ref, vn_ref, ck_ref, cv_ref, g_ref, o_ref):
    tq = q_ref.shape[1]
    past = ck_ref.shape[4]
    suffix = _suffix_matrix(KEY_TILE)
    suffix_new = _suffix_matrix(tq)
    r = lax.broadcasted_iota(jnp.int32, (tq, tq), 0)
    c = lax.broadcasted_iota(jnp.int32, (tq, tq), 1)
    causal = c < r
    heads = range(HEADS_PER_STEP)
    head_lanes = [slice(hh * SB_HEAD_DIM, (hh + 1) * SB_HEAD_DIM) for hh in heads]
    qs = [q_ref[0, :, lanes] for lanes in head_lanes]
    runs = [jnp.zeros((tq, 1), F32)] * HEADS_PER_STEP
    zs = [_dot_nt(q, kn_ref[0, :, lanes]) for q, lanes in zip(qs, head_lanes)]
    ws, runs = _sb_weights(zs, suffix_new, runs, causal)
    accs = [_dot(w, vn_ref[0, :, lanes]) for w, lanes in zip(ws, head_lanes)]
    for j in reversed(range(past // KEY_TILE)):
        keys = slice(j * KEY_TILE, (j + 1) * KEY_TILE)
        zs = [_dot(q, ck_ref[0, 0, hh, :, keys].astype(BF16)) for q, hh in zip(qs, heads)]
        ws, runs = _sb_weights(zs, suffix, runs, None)
        accs = [acc + _dot_nt(w, cv_ref[0, 0, hh, :, keys].astype(BF16))
                for w, hh, acc in zip(ws, heads, accs)]
    o = jnp.concatenate(accs, axis=1) * g_ref[0].astype(F32)
    o_ref[0] = o.astype(o_ref.dtype)


def _sb_decode(qa, kn, vn, cache_k, cache_v, layer, sga):
    B, T, _ = qa.shape
    past = cache_k.shape[3]
    assert past % KEY_TILE == 0
    lanes = HEADS_PER_STEP * SB_HEAD_DIM
    row_spec = pl.BlockSpec((1, T, lanes), lambda b, h: (b, 0, h))
    cache_spec = pl.BlockSpec((1, 1, HEADS_PER_STEP, SB_HEAD_DIM, past),
                              lambda b, h: (layer, b, h, 0, 0))
    cache_k = jnp.swapaxes(cache_k, 3, 4)
    cache_v = jnp.swapaxes(cache_v, 3, 4)
    return pl.pallas_call(
        _sb_decode_kernel,
        grid=(B, GROUP_W // lanes),
        in_specs=[row_spec, row_spec, row_spec, cache_spec, cache_spec, row_spec],
        out_specs=row_spec,
        out_shape=jax.ShapeDtypeStruct((B, T, GROUP_W), BF16),
        compiler_params=pltpu.CompilerParams(
            dimension_semantics=("arbitrary", "arbitrary"), vmem_limit_bytes=VMEM_LIMIT),
    )(qa, kn, vn, cache_k, cache_v, sga)


def _hgrn_tables(C):
    nl = int(math.log2(C))
    assert 1 << nl == C
    n_small = min(nl, int(math.log2(SUBLANES)))
    t = np.arange(C)[:, None]
    j = np.arange(C)[None, :]
    blocks = []
    for l in range(n_small):
        half = 1 << l
        mid = (t >> (l + 1) << (l + 1)) + half
        right = (t & half) != 0
        blocks.append(np.where(right, (j >= mid) & (j <= t), (j > t) & (j < mid)))
    blocks.append(j <= t)
    prefix = np.concatenate(blocks, axis=0).astype(np.float32)
    x = t ^ j
    msb = np.floor(np.log2(np.maximum(x, 1))).astype(np.int32)
    level_of = np.where(t > j, msb, np.where(t == j, nl, -1)).astype(np.int32)
    return jnp.asarray(prefix, BF16), jnp.asarray(level_of), nl, n_small


def _hgrn_out_kernel(*refs, C, sub, nl, n_small, layer, has_s0, alpha):
    (qh_ref, fh_ref, ih_ref, gh_ref, pm_ref, lv_ref, lbl_ref, ng_ref, ma_ref, x_ref, wo_ref,
     lg_ref, lnb_ref) = refs[:13]
    if has_s0:
        s0_ref, y_ref, so_ref, st_ref, b_ref = refs[13:]
    else:
        y_ref, so_ref, st_ref, b_ref = refs[13:]
    t = pl.program_id(1)

    group = min(sub, OUT_GROUP_CHUNKS)
    group_rows = [slice(g * group * C, (g + 1) * group * C) for g in range(sub // group)]
    w_attn = wo_ref[:GROUP_W, :].astype(BF16)
    w_hgrn = wo_ref[GROUP_W:, :].astype(BF16)
    out_attn = [_dot(ma_ref[0, rows, :], w_attn) for rows in group_rows]

    @pl.when(t == 0)
    def _():
        if has_s0:
            for h in range(HG_HEADS):
                st_ref[h] = s0_ref[0, 0, h].T
        else:
            st_ref[...] = jnp.zeros_like(st_ref)

    logits = lbl_ref[...]
    e = jnp.exp(logits - jnp.max(logits, axis=0, keepdims=True))
    lb = jnp.sum(e[:layer + 1], axis=0, keepdims=True) / jnp.sum(e, axis=0, keepdims=True)

    def head(a, h):
        return a[:, h * HG_HEAD_DIM:(h + 1) * HG_HEAD_DIM]

    def prepare(c):
        rows = slice(c * C, (c + 1) * C)
        f = lb + (1.0 - lb) * _sigmoid(fh_ref[0, rows, :])
        g = jnp.log2(f)
        kin = 1.0 - f
        qh = qh_ref[0, rows, :].astype(F32)
        qs = qh * _sigmoid(qh)
        g_hi, g_lo = _split_bf16(g, 2)
        pre = _dot(pm_ref[:n_small * C, :], g_hi)
        cum = _dot(pm_ref[n_small * C:, :], jnp.concatenate([g_hi, g_lo], axis=1))
        b_ref[c] = cum[:, :GROUP_W] + cum[:, GROUP_W:]
        return dict(rows=rows, qs=qs, kin=kin, pre=pre, qs_b=qs.astype(BF16),
                    kin_b=kin.astype(BF16))

    def level_log(c, pre, l):
        if l < n_small:
            return pre[l * C:(l + 1) * C]
        half = 1 << l
        parts = []
        for lo in range(0, C, 2 * half):
            mid = lo + half
            last_left = b_ref[c, mid - 1:mid, :]
            parts.append(last_left - b_ref[c, lo:mid, :])
            parts.append(b_ref[c, mid:mid + half, :] - last_left)
        return jnp.concatenate(parts, axis=0)

    def operands(c, p, l):
        if l == nl:
            return p["qs_b"], p["kin_b"]
        e_l = jnp.exp2(level_log(c, p["pre"], l)).astype(BF16)
        return p["qs_b"] * e_l, p["kin_b"] * e_l

    def finish(c, p, scores):
        rows = p["rows"]
        b = b_ref[c]
        b_last = b_ref[c, C - 1:C, :]
        q_in = (p["qs"] * jnp.exp2(b)).astype(BF16)
        k_out = (p["kin"] * jnp.exp2(b_last - b)).astype(BF16)
        chunk_decay = jnp.exp2(b_last)
        i_b = ih_ref[0, rows, :]
        gh = gh_ref[0, rows, :].astype(F32)
        gate = ng_ref[...] * (gh * _sigmoid(gh))
        outs = []
        for h in range(HG_HEADS):
            st = st_ref[h]
            o = (_dot(scores[h].astype(BF16), head(i_b, h))
                 + _dot_nt(head(q_in, h), st.astype(BF16)))
            st_ref[h] = st * head(chunk_decay, h) + _dot_tn(head(i_b, h), head(k_out, h))
            outs.append(o * lax.rsqrt(jnp.mean(o * o, axis=-1, keepdims=True) + RMS_EPS))
        return (jnp.concatenate(outs, axis=1) * gate).astype(BF16)

    def project_out(g, mixed_hgrn):
        rows = group_rows[g]
        hid = alpha * x_ref[0, rows, :] + (out_attn[g] + _dot(mixed_hgrn, w_hgrn))
        mu = jnp.mean(hid, axis=-1, keepdims=True)
        cen = hid - mu
        var = jnp.mean(cen * cen, axis=-1, keepdims=True)
        y_ref[0, rows, :] = cen * lax.rsqrt(var + LN_EPS) * lg_ref[...] + lnb_ref[...]

    chunks = range(sub)
    preps = [prepare(c) for c in chunks]
    level_of = lv_ref[...]
    order = [(l, c) for l in [nl] + list(range(nl)) for c in chunks]
    scores = [[0.0] * HG_HEADS for _ in chunks]
    nxt = operands(order[0][1], preps[order[0][1]], order[0][0])
    for idx, (l, c) in enumerate(order):
        q_l, k_l = nxt
        if idx + 1 < len(order):
            l_n, c_n = order[idx + 1]
            nxt = operands(c_n, preps[c_n], l_n)
        hit = level_of == l
        for h in range(HG_HEADS):
            scores[c][h] = jnp.where(hit, _dot_nt(head(q_l, h), head(k_l, h)), scores[c][h])
    mixed = []
    for c in chunks:
        mixed.append(finish(c, preps[c], scores[c]))
        if len(mixed) == group:
            project_out(c // group, mixed[0] if group == 1 else jnp.concatenate(mixed, axis=0))
            mixed = []

    @pl.when(t == pl.num_programs(1) - 1)
    def _():
        for h in range(HG_HEADS):
            so_ref[0, 0, h] = st_ref[h].T


def _hgrn_out(qh, fh, ih, gh, lb_logits, norm_g, s0, layer, ma, x, w_out, ln_g, ln_b, *, C, sub,
              alpha):
    B, T, _ = qh.shape
    prefix, level_of, nl, n_small = _hgrn_tables(C)
    has_s0 = s0 is not None
    rows = sub * C
    row_spec = pl.BlockSpec((1, rows, GROUP_W), lambda b, t: (b, t, 0))
    wide_spec = pl.BlockSpec((1, rows, D_MODEL), lambda b, t: (b, t, 0))
    const = lambda shape: pl.BlockSpec(shape, lambda b, t: (0,) * len(shape))
    state_shape = (1, 1, HG_HEADS, HG_HEAD_DIM, HG_HEAD_DIM)
    in_specs = [row_spec, row_spec, row_spec, row_spec, const(prefix.shape), const(level_of.shape),
                const(lb_logits.shape), const((1, GROUP_W)), row_spec, wide_spec,
                const(w_out.shape), const((1, D_MODEL)), const((1, D_MODEL))]
    args = [qh, fh, ih, gh, prefix, level_of, lb_logits.astype(F32),
            norm_g.reshape(1, GROUP_W).astype(F32), ma, x, w_out,
            ln_g.reshape(1, D_MODEL).astype(F32), ln_b.reshape(1, D_MODEL).astype(F32)]
    if has_s0:
        in_specs.append(pl.BlockSpec(state_shape, lambda b, t: (layer, b, 0, 0, 0)))
        args.append(s0)
    return pl.pallas_call(
        functools.partial(_hgrn_out_kernel, C=C, sub=sub, nl=nl, n_small=n_small, layer=layer,
                          has_s0=has_s0, alpha=alpha),
        grid=(B, T // rows),
        in_specs=in_specs,
        out_specs=[wide_spec, pl.BlockSpec(state_shape, lambda b, t: (0, b, 0, 0, 0))],
        out_shape=[jax.ShapeDtypeStruct((B, T, D_MODEL), F32),
                   jax.ShapeDtypeStruct((1, B, HG_HEADS, HG_HEAD_DIM, HG_HEAD_DIM), F32)],
        scratch_shapes=[pltpu.VMEM((HG_HEADS, HG_HEAD_DIM, HG_HEAD_DIM), F32),
                        pltpu.VMEM((sub, C, GROUP_W), F32)],
        compiler_params=pltpu.CompilerParams(
            dimension_semantics=("arbitrary", "arbitrary"), vmem_limit_bytes=VMEM_LIMIT),
    )(*args)


PROMPT_ROWS = 512
PROMPT_CHUNK = 128
OUT_GROUP_CHUNKS = 2
PROMPT_CHUNKS_PER_STEP = 4


def _layer(x, cache_k, cache_v, state_s, layer, w_in, w_out, lb_logits, norm_g, ln_g, ln_b, alpha):
    B, T, _ = x.shape
    decode = cache_k is not None
    if decode:
        qa, k, va, sga, qh, fh, ih, gh, k_out, v_out = _project(
            x, w_in, nb=B, tm=T, k_transposed=False)
        ma = _sb_decode(qa, k, va, cache_k, cache_v, layer, sga)
        y, s_out = _hgrn_out(qh, fh, ih, gh, lb_logits, norm_g, state_s, layer, ma, x, w_out,
                             ln_g, ln_b, C=T, sub=1, alpha=alpha)
    else:
        qa, kt, va, sga, qh, fh, ih, gh, k_out, v_out = _project(
            x, w_in, nb=1, tm=min(PROMPT_ROWS, T), k_transposed=True)
        ma = _sb_prompt(qa, kt, va, sga)
        chunk = min(PROMPT_CHUNK, T)
        y, s_out = _hgrn_out(qh, fh, ih, gh, lb_logits, norm_g, None, layer, ma, x, w_out, ln_g,
                             ln_b, C=chunk, sub=min(PROMPT_CHUNKS_PER_STEP, T // chunk),
                             alpha=alpha)
    return y, k_out, v_out, s_out


def kernel(x_prompt, x_sample, cache_k, cache_v, state_s, w_in, w_out, lb_logits, hgrn_norm_g,
           ln_g, ln_b):
    depth = w_in.shape[0]
    alpha = (2 * depth) ** 0.25
    yp, ys = x_prompt, x_sample
    per_layer = []
    for l in range(depth):
        common = (l, w_in[l], w_out[l], lb_logits, hgrn_norm_g[l], ln_g[l], ln_b[l], alpha)
        yp, kp, vp, sp = _layer(yp, None, None, None, *common)
        ys, kn, vn, sn = _layer(ys, cache_k, cache_v, state_s, *common)
        per_layer.append((kp, vp, sp, kn, vn, sn))
    stack = lambda i: (per_layer[0][i] if depth == 1
                       else jnp.concatenate([p[i] for p in per_layer], axis=0))
    return (yp, ys, stack(0), stack(1), stack(2), stack(3), stack(4), stack(5))
```

```python
import functools
import math

import numpy as np
import jax
import jax.numpy as jnp
from jax import lax
from jax.experimental import pallas as pl
from jax.experimental.pallas import tpu as pltpu

F32 = jnp.float32
BF16 = jnp.bfloat16

D_MODEL = 1024
GROUP_W = 512
N_GROUPS = 8
SB_HEADS = 8
SB_HEAD_DIM = 64
HG_HEADS = 4
HG_HEAD_DIM = 128
LN_EPS = 1e-5
RMS_EPS = 1e-6

SUBLANES = 8
KEY_TILE = 256
SB_Q, SB_V, SB_GATE = 0, 1, 2
HG_Q, HG_I, HG_GATE = 0, 1, 2
LOG2E = 1.4426950408889634
VMEM_LIMIT = 56 * 1024 * 1024


def _sigmoid(x):
    return 1.0 / (1.0 + jnp.exp2(x * (-LOG2E)))


def _dot(a, b):
    return jnp.dot(a, b, preferred_element_type=F32)


def _dot_nt(a, b):
    return lax.dot_general(a, b, (((1,), (1,)), ((), ())), preferred_element_type=F32)


def _dot_tn(a, b):
    return lax.dot_general(a, b, (((0,), (0,)), ((), ())), preferred_element_type=F32)


def _split_bf16(x, parts):
    out = []
    for _ in range(parts - 1):
        p = x.astype(BF16)
        out.append(p)
        x = x - p.astype(F32)
    out.append(x.astype(BF16))
    return out


def _proj_kernel(x_ref, w_ref, sb_ref, k_ref, hg_ref, fh_ref, ko_ref, vo_ref, *, nb, tm,
                 k_transposed):
    m = nb * tm
    x = x_ref[...].reshape(m, D_MODEL).astype(BF16)

    def col(c):
        return _dot(x, w_ref[:, c * GROUP_W:(c + 1) * GROUP_W].astype(BF16))

    def put(ref, val, slot=0):
        ref[:, :, slot * GROUP_W:(slot + 1) * GROUP_W] = (
            val.reshape(nb, tm, GROUP_W).astype(ref.dtype))

    def put_heads(ref, val):
        if k_transposed:
            val_t = val.T
            for h in range(SB_HEADS):
                ref[0, 0, h] = val_t[h * SB_HEAD_DIM:(h + 1) * SB_HEAD_DIM, :]
            return val_t
        for b in range(nb):
            for h in range(SB_HEADS):
                ref[0, b, h] = val[b * tm:(b + 1) * tm, h * SB_HEAD_DIM:(h + 1) * SB_HEAD_DIM]

    put(sb_ref, col(0) * (SB_HEAD_DIM ** -0.5 * LOG2E), SB_Q)
    k = col(1)
    kt = put_heads(ko_ref, k)
    if k_transposed:
        kt = kt.astype(BF16)
        for j in range(m // KEY_TILE):
            k_ref[0, j] = kt[:, j * KEY_TILE:(j + 1) * KEY_TILE]
    else:
        put(k_ref, k)
    v = col(2)
    put_heads(vo_ref, v)
    put(sb_ref, v, SB_V)
    ga = col(3)
    put(sb_ref, ga * _sigmoid(ga), SB_GATE)
    put(hg_ref, col(4), HG_Q)
    put(fh_ref, col(5))
    put(hg_ref, col(6), HG_I)
    put(hg_ref, col(7), HG_GATE)


def _project(x, w, *, nb, tm, k_transposed):
    B, T, _ = x.shape
    grid = (B // nb, T // tm)
    act = lambda dt: jax.ShapeDtypeStruct((B, T, GROUP_W), dt)
    act_spec = pl.BlockSpec((nb, tm, GROUP_W), lambda b, t: (b, t, 0))
    act3 = jax.ShapeDtypeStruct((B, T, 3 * GROUP_W), BF16)
    act3_spec = pl.BlockSpec((nb, tm, 3 * GROUP_W), lambda b, t: (b, t, 0))
    if k_transposed:
        assert nb == 1 and tm % KEY_TILE == 0
        k_shape = jax.ShapeDtypeStruct((B, T // KEY_TILE, GROUP_W, KEY_TILE), BF16)
        k_spec = pl.BlockSpec((1, tm // KEY_TILE, GROUP_W, KEY_TILE), lambda b, t: (b, t, 0, 0))
        kv_shape = jax.ShapeDtypeStruct((1, B, SB_HEADS, SB_HEAD_DIM, T), F32)
        kv_spec = pl.BlockSpec((1, 1, SB_HEADS, SB_HEAD_DIM, tm), lambda b, t: (0, b, 0, 0, t))
    else:
        k_shape, k_spec = act(BF16), act_spec
        kv_shape = jax.ShapeDtypeStruct((1, B, SB_HEADS, T, SB_HEAD_DIM), F32)
        kv_spec = pl.BlockSpec((1, nb, SB_HEADS, tm, SB_HEAD_DIM), lambda b, t: (0, b, 0, t, 0))
    outs = pl.pallas_call(
        functools.partial(_proj_kernel, nb=nb, tm=tm, k_transposed=k_transposed),
        grid=grid,
        in_specs=[pl.BlockSpec((nb, tm, D_MODEL), lambda b, t: (b, t, 0)),
                  pl.BlockSpec((D_MODEL, N_GROUPS * GROUP_W), lambda b, t: (0, 0),
                               pipeline_mode=pl.Buffered(1))],
        out_specs=[act3_spec, k_spec, act3_spec, act_spec, kv_spec, kv_spec],
        out_shape=[act3, k_shape, act3, act(F32), kv_shape, kv_shape],
        compiler_params=pltpu.CompilerParams(
            dimension_semantics=("arbitrary", "arbitrary"), vmem_limit_bytes=VMEM_LIMIT),
    )(x, w)
    if k_transposed:
        outs = list(outs[:4]) + [jnp.swapaxes(o, 3, 4) for o in outs[4:]]
    return outs


def _suffix_matrix(n):
    r = lax.broadcasted_iota(jnp.int32, (n, n), 0)
    c = lax.broadcasted_iota(jnp.int32, (n, n), 1)
    return jnp.where(r >= c, 1.0, 0.0).astype(BF16)


def _sb_weights(zs, suffix, runs, causal):
    incls = _sb_suffix_sums(zs, suffix, causal)
    ws = [_sb_weight(z, incl, run, causal) for z, incl, run in zip(zs, incls, runs)]
    return ws, [run + incl[:, 0:1] for run, incl in zip(runs, incls)]


def _sb_suffix_sums(zs, suffix, causal):
    drops = []
    for z in zs:
        drop = jnp.maximum(z, 0.0) + jnp.log2(1.0 + jnp.exp2(-jnp.abs(z)))
        if causal is not None:
            drop = jnp.where(causal, drop, 0.0)
        drops.append(drop.astype(BF16))
    return [_dot(drop, suffix) for drop in drops]


def _sb_weight(z, incl, run, causal):
    w = jnp.exp2(z - incl - run)
    if causal is not None:
        w = jnp.where(causal, w, 0.0)
    return w.astype(BF16)


def _sb_prompt_kernel(sb_ref, kt_ref, o_ref, z_ref, w_ref, run_ref, acc_ref, *, heads):
    tq = KEY_TILE
    nq = sb_ref.shape[1]
    i = pl.program_id(1)
    suffix = _suffix_matrix(KEY_TILE)
    r = lax.broadcasted_iota(jnp.int32, (tq, KEY_TILE), 0)
    c = lax.broadcasted_iota(jnp.int32, (tq, KEY_TILE), 1)
    causal = c < r
    head_lanes = [slice(hh * SB_HEAD_DIM, (hh + 1) * SB_HEAD_DIM) for hh in range(heads)]

    def packed(slot, lanes):
        return slice(slot * GROUP_W + lanes.start, slot * GROUP_W + lanes.stop)

    def logits(bi, j, hh):
        return _dot(sb_ref[0, bi, :, packed(SB_Q, head_lanes[hh])],
                    kt_ref[0, j, head_lanes[hh], :])

    def add_weighted_values(slot, j):
        for hh in range(heads):
            acc_ref[hh] += _dot(w_ref[slot, hh], sb_ref[0, j, :, packed(SB_V, head_lanes[hh])])

    def finalize(bi):
        o = jnp.concatenate([acc_ref[hh] for hh in range(heads)], axis=1)
        gate = sb_ref[0, bi, :, SB_GATE * GROUP_W:(SB_GATE + 1) * GROUP_W].astype(F32)
        o_ref[0, pl.ds(pl.multiple_of(bi * tq, tq), tq), :] = (o * gate).astype(o_ref.dtype)

    def step(bi, j, cur, masked):
        mask = causal if masked else None
        zs = [z_ref[cur, hh] for hh in range(heads)]
        incls = _sb_suffix_sums(zs, suffix, mask)
        same_block = j > 0
        nxt = jnp.minimum(bi + 1, nq - 1)
        next_bi = jnp.where(same_block, bi, nxt)
        next_j = jnp.where(same_block, j - 1, nxt)
        for hh in range(heads):
            z_ref[1 - cur, hh] = logits(next_bi, next_j, hh)
            run = jnp.zeros((tq, 1), F32) if masked else run_ref[hh]
            w_ref[1 - cur, hh] = _sb_weight(zs[hh], incls[hh], run, mask)
            run_ref[hh] = run + incls[hh][:, 0:1]

    first_slot = (i * (i + 1) // 2) % 2

    @pl.when(i == 0)
    def _():
        for hh in range(heads):
            z_ref[0, hh] = logits(0, 0, hh)
        w_ref[0] = jnp.zeros_like(w_ref[0])
        acc_ref[...] = jnp.zeros_like(acc_ref)

    for parity in range(2):
        @pl.when(first_slot == parity)
        def _():
            add_weighted_values(parity, 0)
            finalize(jnp.maximum(i - 1, 0))
            acc_ref[...] = jnp.zeros_like(acc_ref)
            step(i, i, parity, True)

    def pair_body(n, carry):
        j = i - 1 - 2 * n
        for parity in range(2):
            @pl.when((first_slot + 1) % 2 == parity)
            def _():
                add_weighted_values(parity, j + 1)
                step(i, j, parity, False)
                add_weighted_values(1 - parity, j)
                step(i, j - 1, 1 - parity, False)
        return carry

    lax.fori_loop(0, i // 2, pair_body, 0)

    @pl.when(i % 2 == 1)
    def _():
        for parity in range(2):
            @pl.when((first_slot + 1) % 2 == parity)
            def _():
                add_weighted_values(parity, 1)
                step(i, 0, parity, False)

    @pl.when(i == nq - 1)
    def _():
        for parity in range(2):
            @pl.when((first_slot + 1 + i) % 2 == parity)
            def _():
                add_weighted_values(parity, 0)
        finalize(i)


def _sb_prompt(sb, kt):
    B, T, _ = sb.shape
    tq = KEY_TILE
    nkt = T // KEY_TILE
    sb4 = sb.reshape(B, nkt, KEY_TILE, 3 * GROUP_W)
    row_spec = pl.BlockSpec((1, T, GROUP_W), lambda b, i: (b, 0, 0))
    tile_spec = lambda shape: pl.BlockSpec((1,) + shape, lambda b, i: (b, 0, 0, 0))
    return pl.pallas_call(
        functools.partial(_sb_prompt_kernel, heads=SB_HEADS),
        grid=(B, T // tq),
        in_specs=[tile_spec((nkt, KEY_TILE, 3 * GROUP_W)), tile_spec((nkt, GROUP_W, KEY_TILE))],
        out_specs=row_spec,
        out_shape=jax.ShapeDtypeStruct((B, T, GROUP_W), BF16),
        scratch_shapes=[pltpu.VMEM((2, SB_HEADS, tq, KEY_TILE), F32),
                        pltpu.VMEM((2, SB_HEADS, tq, KEY_TILE), BF16),
                        pltpu.VMEM((SB_HEADS, tq, 1), F32),
                        pltpu.VMEM((SB_HEADS, tq, SB_HEAD_DIM), F32)],
        compiler_params=pltpu.CompilerParams(
            dimension_semantics=("arbitrary", "arbitrary"), vmem_limit_bytes=VMEM_LIMIT),
    )(sb4, kt)


def _sb_decode_kernel(sb_ref, kn_ref, ck_ref, cv_ref, o_ref):
    tq = sb_ref.shape[1]
    past = ck_ref.shape[4]
    suffix = _suffix_matrix(KEY_TILE)
    suffix_new = _suffix_matrix(tq)
    r = lax.broadcasted_iota(jnp.int32, (tq, tq), 0)
    c = lax.broadcasted_iota(jnp.int32, (tq, tq), 1)
    causal = c < r
    heads = range(SB_HEADS)
    head_lanes = [slice(hh * SB_HEAD_DIM, (hh + 1) * SB_HEAD_DIM) for hh in heads]
    group = lambda slot: sb_ref[0, :, slot * GROUP_W:(slot + 1) * GROUP_W]
    q_all, v_all = group(SB_Q), group(SB_V)
    qs = [q_all[:, lanes] for lanes in head_lanes]
    runs = [jnp.zeros((tq, 1), F32)] * SB_HEADS
    zs = [_dot_nt(q, kn_ref[0, :, lanes]) for q, lanes in zip(qs, head_lanes)]
    ws, runs = _sb_weights(zs, suffix_new, runs, causal)
    accs = [_dot(w, v_all[:, lanes]) for w, lanes in zip(ws, head_lanes)]
    for j in reversed(range(past // KEY_TILE)):
        keys = slice(j * KEY_TILE, (j + 1) * KEY_TILE)
        zs = [_dot(q, ck_ref[0, 0, hh, :, keys].astype(BF16)) for q, hh in zip(qs, heads)]
        ws, runs = _sb_weights(zs, suffix, runs, None)
        accs = [acc + _dot_nt(w, cv_ref[0, 0, hh, :, keys].astype(BF16))
                for w, hh, acc in zip(ws, heads, accs)]
    o = jnp.concatenate(accs, axis=1) * group(SB_GATE).astype(F32)
    o_ref[0] = o.astype(o_ref.dtype)


def _sb_decode(sb, kn, cache_k, cache_v, layer):
    B, T, _ = kn.shape
    past = cache_k.shape[3]
    assert past % KEY_TILE == 0
    row_spec = lambda width: pl.BlockSpec((1, T, width), lambda b: (b, 0, 0))
    cache_spec = pl.BlockSpec((1, 1, SB_HEADS, SB_HEAD_DIM, past), lambda b: (layer, b, 0, 0, 0))
    cache_k = jnp.swapaxes(cache_k, 3, 4)
    cache_v = jnp.swapaxes(cache_v, 3, 4)
    return pl.pallas_call(
        _sb_decode_kernel,
        grid=(B,),
        in_specs=[row_spec(3 * GROUP_W), row_spec(GROUP_W), cache_spec, cache_spec],
        out_specs=row_spec(GROUP_W),
        out_shape=jax.ShapeDtypeStruct((B, T, GROUP_W), BF16),
        compiler_params=pltpu.CompilerParams(
            dimension_semantics=("arbitrary",), vmem_limit_bytes=VMEM_LIMIT),
    )(sb, kn, cache_k, cache_v)


def _hgrn_tables(C):
    nl = int(math.log2(C))
    assert 1 << nl == C
    n_small = min(nl, int(math.log2(SUBLANES)))
    t = np.arange(C)[:, None]
    j = np.arange(C)[None, :]
    blocks = []
    for l in range(n_small):
        half = 1 << l
        mid = (t >> (l + 1) << (l + 1)) + half
        right = (t & half) != 0
        blocks.append(np.where(right, (j >= mid) & (j <= t), (j > t) & (j < mid)))
    blocks.append(j <= t)
    prefix = np.concatenate(blocks, axis=0).astype(np.float32)
    x = t ^ j
    msb = np.floor(np.log2(np.maximum(x, 1))).astype(np.int32)
    level_of = np.where(t > j, msb, np.where(t == j, nl, -1)).astype(np.int32)
    return jnp.asarray(prefix, BF16), jnp.asarray(level_of), nl, n_small


def _hgrn_out_kernel(*refs, C, sub, nl, n_small, layer, has_s0, alpha):
    (hg_ref, fh_ref, pm_ref, lv_ref, lbl_ref, ng_ref, ma_ref, x_ref, wo_ref,
     lg_ref, lnb_ref) = refs[:11]
    if has_s0:
        s0_ref, y_ref, so_ref, st_ref, b_ref = refs[11:]
    else:
        y_ref, so_ref, st_ref, b_ref = refs[11:]
    t = pl.program_id(1)

    def packed(rows, slot):
        return hg_ref[0, rows, slot * GROUP_W:(slot + 1) * GROUP_W]

    group = min(sub, OUT_GROUP_CHUNKS)
    group_rows = [slice(g * group * C, (g + 1) * group * C) for g in range(sub // group)]
    w_attn = wo_ref[:GROUP_W, :].astype(BF16)
    w_hgrn = wo_ref[GROUP_W:, :].astype(BF16)
    out_attn = [_dot(ma_ref[0, rows, :], w_attn) for rows in group_rows]

    @pl.when(t == 0)
    def _():
        if has_s0:
            for h in range(HG_HEADS):
                st_ref[h] = s0_ref[0, 0, h].T
        else:
            st_ref[...] = jnp.zeros_like(st_ref)

    logits = lbl_ref[...]
    e = jnp.exp(logits - jnp.max(logits, axis=0, keepdims=True))
    lb = jnp.sum(e[:layer + 1], axis=0, keepdims=True) / jnp.sum(e, axis=0, keepdims=True)

    def head(a, h):
        return a[:, h * HG_HEAD_DIM:(h + 1) * HG_HEAD_DIM]

    def prepare(c):
        rows = slice(c * C, (c + 1) * C)
        f = lb + (1.0 - lb) * _sigmoid(fh_ref[0, rows, :])
        g = jnp.log2(f)
        kin = 1.0 - f
        qh = packed(rows, HG_Q).astype(F32)
        qs = qh * _sigmoid(qh)
        g_hi, g_lo = _split_bf16(g, 2)
        pre = _dot(pm_ref[:n_small * C, :], g_hi)
        cum = _dot(pm_ref[n_small * C:, :], jnp.concatenate([g_hi, g_lo], axis=1))
        b_ref[c] = cum[:, :GROUP_W] + cum[:, GROUP_W:]
        return dict(rows=rows, qs=qs, kin=kin, pre=pre, qs_b=qs.astype(BF16),
                    kin_b=kin.astype(BF16))

    def level_log(c, pre, l):
        if l < n_small:
            return pre[l * C:(l + 1) * C]
        half = 1 << l
        parts = []
        for lo in range(0, C, 2 * half):
            mid = lo + half
            last_left = b_ref[c, mid - 1:mid, :]
            parts.append(last_left - b_ref[c, lo:mid, :])
            parts.append(b_ref[c, mid:mid + half, :] - last_left)
        return jnp.concatenate(parts, axis=0)

    def operands(c, p, l):
        if l == nl:
            return p["qs_b"], p["kin_b"]
        e_l = jnp.exp2(level_log(c, p["pre"], l)).astype(BF16)
        return p["qs_b"] * e_l, p["kin_b"] * e_l

    def finish(c, p, scores):
        rows = p["rows"]
        b = b_ref[c]
        b_last = b_ref[c, C - 1:C, :]
        q_in = (p["qs"] * jnp.exp2(b)).astype(BF16)
        k_out = (p["kin"] * jnp.exp2(b_last - b)).astype(BF16)
        chunk_decay = jnp.exp2(b_last)
        i_b = packed(rows, HG_I)
        gh = packed(rows, HG_GATE).astype(F32)
        gate = ng_ref[...] * (gh * _sigmoid(gh))
        outs = []
        for h in range(HG_HEADS):
            st = st_ref[h]
            o = (_dot(scores[h].astype(BF16), head(i_b, h))
                 + _dot_nt(head(q_in, h), st.astype(BF16)))
            st_ref[h] = st * head(chunk_decay, h) + _dot_tn(head(i_b, h), head(k_out, h))
            outs.append(o * lax.rsqrt(jnp.mean(o * o, axis=-1, keepdims=True) + RMS_EPS))
        return (jnp.concatenate(outs, axis=1) * gate).astype(BF16)

    def project_out(g, mixed_hgrn):
        rows = group_rows[g]
        hid = alpha * x_ref[0, rows, :] + (out_attn[g] + _dot(mixed_hgrn, w_hgrn))
        mu = jnp.mean(hid, axis=-1, keepdims=True)
        cen = hid - mu
        var = jnp.mean(cen * cen, axis=-1, keepdims=True)
        y_ref[0, rows, :] = cen * lax.rsqrt(var + LN_EPS) * lg_ref[...] + lnb_ref[...]

    chunks = range(sub)
    preps = [prepare(c) for c in chunks]
    level_of = lv_ref[...]
    order = [(l, c) for l in [nl] + list(range(nl)) for c in chunks]
    scores = [[0.0] * HG_HEADS for _ in chunks]
    nxt = operands(order[0][1], preps[order[0][1]], order[0][0])
    for idx, (l, c) in enumerate(order):
        q_l, k_l = nxt
        if idx + 1 < len(order):
            l_n, c_n = order[idx + 1]
            nxt = operands(c_n, preps[c_n], l_n)
        hit = level_of == l
        for h in range(HG_HEADS):
            scores[c][h] = jnp.where(hit, _dot_nt(head(q_l, h), head(k_l, h)), scores[c][h])
    mixed = []
    for c in chunks:
        mixed.append(finish(c, preps[c], scores[c]))
        if len(mixed) == group:
            project_out(c // group, mixed[0] if group == 1 else jnp.concatenate(mixed, axis=0))
            mixed = []

    @pl.when(t == pl.num_programs(1) - 1)
    def _():
        for h in range(HG_HEADS):
            so_ref[0, 0, h] = st_ref[h].T


def _hgrn_out(hg, fh, lb_logits, norm_g, s0, layer, ma, x, w_out, ln_g, ln_b, *, C, sub, alpha):
    B, T, _ = fh.shape
    prefix, level_of, nl, n_small = _hgrn_tables(C)
    has_s0 = s0 is not None
    rows = sub * C
    row_spec = pl.BlockSpec((1, rows, GROUP_W), lambda b, t: (b, t, 0))
    row3_spec = pl.BlockSpec((1, rows, 3 * GROUP_W), lambda b, t: (b, t, 0))
    wide_spec = pl.BlockSpec((1, rows, D_MODEL), lambda b, t: (b, t, 0))
    const = lambda shape: pl.BlockSpec(shape, lambda b, t: (0,) * len(shape))
    state_shape = (1, 1, HG_HEADS, HG_HEAD_DIM, HG_HEAD_DIM)
    in_specs = [row3_spec, row_spec, const(prefix.shape), const(level_of.shape),
                const(lb_logits.shape), const((1, GROUP_W)), row_spec, wide_spec,
                const(w_out.shape), const((1, D_MODEL)), const((1, D_MODEL))]
    args = [hg, fh, prefix, level_of, lb_logits.astype(F32),
            norm_g.reshape(1, GROUP_W).astype(F32), ma, x, w_out,
            ln_g.reshape(1, D_MODEL).astype(F32), ln_b.reshape(1, D_MODEL).astype(F32)]
    if has_s0:
        in_specs.append(pl.BlockSpec(state_shape, lambda b, t: (layer, b, 0, 0, 0)))
        args.append(s0)
    return pl.pallas_call(
        functools.partial(_hgrn_out_kernel, C=C, sub=sub, nl=nl, n_small=n_small, layer=layer,
                          has_s0=has_s0, alpha=alpha),
        grid=(B, T // rows),
        in_specs=in_specs,
        out_specs=[wide_spec, pl.BlockSpec(state_shape, lambda b, t: (0, b, 0, 0, 0))],
        out_shape=[jax.ShapeDtypeStruct((B, T, D_MODEL), F32),
                   jax.ShapeDtypeStruct((1, B, HG_HEADS, HG_HEAD_DIM, HG_HEAD_DIM), F32)],
        scratch_shapes=[pltpu.VMEM((HG_HEADS, HG_HEAD_DIM, HG_HEAD_DIM), F32),
                        pltpu.VMEM((sub, C, GROUP_W), F32)],
        compiler_params=pltpu.CompilerParams(
            dimension_semantics=("arbitrary", "arbitrary"), vmem_limit_bytes=VMEM_LIMIT),
    )(*args)


PROMPT_ROWS = 512
PROMPT_CHUNK = 128
OUT_GROUP_CHUNKS = 2
PROMPT_CHUNKS_PER_STEP = 4


def _layer(x, cache_k, cache_v, state_s, layer, w_in, w_out, lb_logits, norm_g, ln_g, ln_b, alpha):
    B, T, _ = x.shape
    decode = cache_k is not None
    if decode:
        sb, k, hg, fh, k_out, v_out = _project(x, w_in, nb=B, tm=T, k_transposed=False)
        ma = _sb_decode(sb, k, cache_k, cache_v, layer)
        y, s_out = _hgrn_out(hg, fh, lb_logits, norm_g, state_s, layer, ma, x, w_out, ln_g, ln_b,
                             C=T, sub=1, alpha=alpha)
    else:
        sb, kt, hg, fh, k_out, v_out = _project(
            x, w_in, nb=1, tm=min(PROMPT_ROWS, T), k_transposed=True)
        ma = _sb_prompt(sb, kt)
        chunk = min(PROMPT_CHUNK, T)
        y, s_out = _hgrn_out(hg, fh, lb_logits, norm_g, None, layer, ma, x, w_out, ln_g, ln_b,
                             C=chunk, sub=min(PROMPT_CHUNKS_PER_STEP, T // chunk), alpha=alpha)
    return y, k_out, v_out, s_out


def kernel(x_prompt, x_sample, cache_k, cache_v, state_s, w_in, w_out, lb_logits, hgrn_norm_g,
           ln_g, ln_b):
    depth = w_in.shape[0]
    alpha = (2 * depth) ** 0.25
    yp, ys = x_prompt, x_sample
    per_layer = []
    for l in range(depth):
        common = (l, w_in[l], w_out[l], lb_logits, hgrn_norm_g[l], ln_g[l], ln_b[l], alpha)
        yp, kp, vp, sp = _layer(yp, None, None, None, *common)
        ys, kn, vn, sn = _layer(ys, cache_k, cache_v, state_s, *common)
        per_layer.append((kp, vp, sp, kn, vn, sn))
    stack = lambda i: (per_layer[0][i] if depth == 1
                       else jnp.concatenate([p[i] for p in per_layer], axis=0))
    return (yp, ys, stack(0), stack(1), stack(2), stack(3), stack(4), stack(5))
```

```python
import functools
import math

import numpy as np
import jax
import jax.numpy as jnp
from jax import lax
from jax.experimental import pallas as pl
from jax.experimental.pallas import tpu as pltpu

F32 = jnp.float32
BF16 = jnp.bfloat16

D_MODEL = 1024
GROUP_W = 512
N_GROUPS = 8
SB_HEADS = 8
SB_HEAD_DIM = 64
HG_HEADS = 4
HG_HEAD_DIM = 128
LN_EPS = 1e-5
RMS_EPS = 1e-6

SUBLANES = 8
KEY_TILE = 256
SB_Q, SB_V, SB_GATE = 0, 1, 2
HG_Q, HG_I, HG_GATE = 0, 1, 2
NEGLIGIBLE_LOG2 = 200.0
LOG2E = 1.4426950408889634
VMEM_LIMIT = 56 * 1024 * 1024


def _sigmoid(x):
    return 1.0 / (1.0 + jnp.exp2(x * (-LOG2E)))


def _dot(a, b):
    return jnp.dot(a, b, preferred_element_type=F32)


def _dot_nt(a, b):
    return lax.dot_general(a, b, (((1,), (1,)), ((), ())), preferred_element_type=F32)


def _dot_tn(a, b):
    return lax.dot_general(a, b, (((0,), (0,)), ((), ())), preferred_element_type=F32)


def _split_bf16(x, parts):
    out = []
    for _ in range(parts - 1):
        p = x.astype(BF16)
        out.append(p)
        x = x - p.astype(F32)
    out.append(x.astype(BF16))
    return out


def _proj_kernel(x_ref, w_ref, sb_ref, k_ref, hg_ref, fh_ref, ko_ref, vo_ref, *, nb, tm,
                 k_transposed):
    m = nb * tm
    x = x_ref[...].reshape(m, D_MODEL).astype(BF16)

    def col(c):
        return _dot(x, w_ref[:, c * GROUP_W:(c + 1) * GROUP_W].astype(BF16))

    def put(ref, val, slot=0):
        ref[:, :, slot * GROUP_W:(slot + 1) * GROUP_W] = (
            val.reshape(nb, tm, GROUP_W).astype(ref.dtype))

    def put_heads(ref, val):
        if k_transposed:
            val_t = val.T
            for h in range(SB_HEADS):
                ref[0, 0, h] = val_t[h * SB_HEAD_DIM:(h + 1) * SB_HEAD_DIM, :]
            return val_t
        for b in range(nb):
            for h in range(SB_HEADS):
                ref[0, b, h] = val[b * tm:(b + 1) * tm, h * SB_HEAD_DIM:(h + 1) * SB_HEAD_DIM]

    put(sb_ref, col(0) * (SB_HEAD_DIM ** -0.5 * LOG2E), SB_Q)
    k = col(1)
    kt = put_heads(ko_ref, k)
    if k_transposed:
        kt = kt.astype(BF16)
        for j in range(m // KEY_TILE):
            k_ref[0, j] = kt[:, j * KEY_TILE:(j + 1) * KEY_TILE]
    else:
        put(k_ref, k)
    v = col(2)
    put_heads(vo_ref, v)
    put(sb_ref, v, SB_V)
    ga = col(3)
    put(sb_ref, ga * _sigmoid(ga), SB_GATE)
    put(hg_ref, col(4), HG_Q)
    put(fh_ref, col(5))
    put(hg_ref, col(6), HG_I)
    put(hg_ref, col(7), HG_GATE)


def _project(x, w, *, nb, tm, k_transposed):
    B, T, _ = x.shape
    grid = (B // nb, T // tm)
    act = lambda dt: jax.ShapeDtypeStruct((B, T, GROUP_W), dt)
    act_spec = pl.BlockSpec((nb, tm, GROUP_W), lambda b, t: (b, t, 0))
    act3 = jax.ShapeDtypeStruct((B, T, 3 * GROUP_W), BF16)
    act3_spec = pl.BlockSpec((nb, tm, 3 * GROUP_W), lambda b, t: (b, t, 0))
    if k_transposed:
        assert nb == 1 and tm % KEY_TILE == 0
        k_shape = jax.ShapeDtypeStruct((B, T // KEY_TILE, GROUP_W, KEY_TILE), BF16)
        k_spec = pl.BlockSpec((1, tm // KEY_TILE, GROUP_W, KEY_TILE), lambda b, t: (b, t, 0, 0))
        kv_shape = jax.ShapeDtypeStruct((1, B, SB_HEADS, SB_HEAD_DIM, T), F32)
        kv_spec = pl.BlockSpec((1, 1, SB_HEADS, SB_HEAD_DIM, tm), lambda b, t: (0, b, 0, 0, t))
    else:
        k_shape, k_spec = act(BF16), act_spec
        kv_shape = jax.ShapeDtypeStruct((1, B, SB_HEADS, T, SB_HEAD_DIM), F32)
        kv_spec = pl.BlockSpec((1, nb, SB_HEADS, tm, SB_HEAD_DIM), lambda b, t: (0, b, 0, t, 0))
    outs = pl.pallas_call(
        functools.partial(_proj_kernel, nb=nb, tm=tm, k_transposed=k_transposed),
        grid=grid,
        in_specs=[pl.BlockSpec((nb, tm, D_MODEL), lambda b, t: (b, t, 0)),
                  pl.BlockSpec((D_MODEL, N_GROUPS * GROUP_W), lambda b, t: (0, 0),
                               pipeline_mode=pl.Buffered(1))],
        out_specs=[act3_spec, k_spec, act3_spec, act_spec, kv_spec, kv_spec],
        out_shape=[act3, k_shape, act3, act(F32), kv_shape, kv_shape],
        compiler_params=pltpu.CompilerParams(
            dimension_semantics=("arbitrary", "arbitrary"), vmem_limit_bytes=VMEM_LIMIT),
    )(x, w)
    if k_transposed:
        outs = list(outs[:4]) + [jnp.swapaxes(o, 3, 4) for o in outs[4:]]
    return outs


def _suffix_matrix(n):
    r = lax.broadcasted_iota(jnp.int32, (n, n), 0)
    c = lax.broadcasted_iota(jnp.int32, (n, n), 1)
    return jnp.where(r >= c, 1.0, 0.0).astype(BF16)


def _sb_weights(zs, suffix, runs, causal):
    incls = _sb_suffix_sums(zs, suffix, causal)
    ws = [_sb_weight(z, incl, run, causal) for z, incl, run in zip(zs, incls, runs)]
    return ws, [run + incl[:, 0:1] for run, incl in zip(runs, incls)]


def _sb_suffix_sums(zs, suffix, causal):
    drops = []
    for z in zs:
        drop = jnp.maximum(z, 0.0) + jnp.log2(1.0 + jnp.exp2(-jnp.abs(z)))
        if causal is not None:
            drop = jnp.where(causal, drop, 0.0)
        drops.append(drop.astype(BF16))
    return [_dot(drop, suffix) for drop in drops]


def _sb_weight(z, incl, run, causal):
    w = jnp.exp2(z - incl - run)
    if causal is not None:
        w = jnp.where(causal, w, 0.0)
    return w.astype(BF16)


def _sb_prompt_kernel(sb_ref, kt_ref, o_ref, z_ref, w_ref, run_ref, acc_ref, state_ref, *, heads):
    tq = KEY_TILE
    nq = sb_ref.shape[1]
    i = pl.program_id(1)
    suffix = _suffix_matrix(KEY_TILE)
    r = lax.broadcasted_iota(jnp.int32, (tq, KEY_TILE), 0)
    c = lax.broadcasted_iota(jnp.int32, (tq, KEY_TILE), 1)
    causal = c < r
    head_lanes = [slice(hh * SB_HEAD_DIM, (hh + 1) * SB_HEAD_DIM) for hh in range(heads)]

    def packed(slot, lanes):
        return slice(slot * GROUP_W + lanes.start, slot * GROUP_W + lanes.stop)

    def logits(bi, j, hh):
        return _dot(sb_ref[0, bi, :, packed(SB_Q, head_lanes[hh])],
                    kt_ref[0, j, head_lanes[hh], :])

    def add_weighted_values(slot, j):
        for hh in range(heads):
            acc_ref[hh] += _dot(w_ref[slot, hh], sb_ref[0, j, :, packed(SB_V, head_lanes[hh])])

    def finalize(bi):
        o = jnp.concatenate([acc_ref[hh] for hh in range(heads)], axis=1)
        gate = sb_ref[0, bi, :, SB_GATE * GROUP_W:(SB_GATE + 1) * GROUP_W].astype(F32)
        o_ref[0, pl.ds(pl.multiple_of(bi * tq, tq), tq), :] = (o * gate).astype(o_ref.dtype)

    def step(bi, j, cur, masked):
        mask = causal if masked else None
        zs = [z_ref[cur, hh] for hh in range(heads)]
        incls = _sb_suffix_sums(zs, suffix, mask)
        same_block = j > 0
        nxt = jnp.minimum(bi + 1, nq - 1)
        next_bi = jnp.where(same_block, bi, nxt)
        next_j = jnp.where(same_block, j - 1, nxt)
        for hh in range(heads):
            z_ref[1 - cur, hh] = logits(next_bi, next_j, hh)
            run = jnp.zeros((tq, 1), F32) if masked else run_ref[hh]
            w_ref[1 - cur, hh] = _sb_weight(zs[hh], incls[hh], run, mask)
            run_ref[hh] = run + incls[hh][:, 0:1]

    @pl.when(i == 0)
    def _():
        for hh in range(heads):
            z_ref[0, hh] = logits(0, 0, hh)
        w_ref[0] = jnp.zeros_like(w_ref[0])
        acc_ref[...] = jnp.zeros_like(acc_ref)
        state_ref[0] = 0
        state_ref[1] = 0

    first_slot = state_ref[0]
    pending_tile = state_ref[1]
    for parity in range(2):
        @pl.when(first_slot == parity)
        def _():
            add_weighted_values(parity, pending_tile)
            finalize(jnp.maximum(i - 1, 0))
            acc_ref[...] = jnp.zeros_like(acc_ref)
            step(i, i, parity, True)

    def weight_left():
        lowest = run_ref[0]
        for hh in range(1, heads):
            lowest = jnp.minimum(lowest, run_ref[hh])
        return jnp.min(lowest) < NEGLIGIBLE_LOG2

    def sweep(carry):
        n, _ = carry
        j = i - 1 - n
        for parity in range(2):
            @pl.when((first_slot + 1 + n) % 2 == parity)
            def _():
                add_weighted_values(parity, j + 1)
                step(i, j, parity, False)
        return n + 1, jnp.logical_and(j > 0, weight_left())

    swept, _ = lax.while_loop(lambda carry: carry[1], sweep, (jnp.int32(0), i > 0))
    slot = (first_slot + 1 + swept) % 2
    last_tile = i - swept
    nxt = jnp.minimum(i + 1, nq - 1)
    for parity in range(2):
        @pl.when(jnp.logical_and(last_tile > 0, slot == parity))
        def _():
            for hh in range(heads):
                z_ref[parity, hh] = logits(nxt, nxt, hh)
    state_ref[0] = slot
    state_ref[1] = last_tile

    @pl.when(i == nq - 1)
    def _():
        for parity in range(2):
            @pl.when(slot == parity)
            def _():
                add_weighted_values(parity, last_tile)
        finalize(i)


def _sb_prompt(sb, kt):
    B, T, _ = sb.shape
    tq = KEY_TILE
    nkt = T // KEY_TILE
    sb4 = sb.reshape(B, nkt, KEY_TILE, 3 * GROUP_W)
    row_spec = pl.BlockSpec((1, T, GROUP_W), lambda b, i: (b, 0, 0))
    tile_spec = lambda shape: pl.BlockSpec((1,) + shape, lambda b, i: (b, 0, 0, 0))
    return pl.pallas_call(
        functools.partial(_sb_prompt_kernel, heads=SB_HEADS),
        grid=(B, T // tq),
        in_specs=[tile_spec((nkt, KEY_TILE, 3 * GROUP_W)), tile_spec((nkt, GROUP_W, KEY_TILE))],
        out_specs=row_spec,
        out_shape=jax.ShapeDtypeStruct((B, T, GROUP_W), BF16),
        scratch_shapes=[pltpu.VMEM((2, SB_HEADS, tq, KEY_TILE), F32),
                        pltpu.VMEM((2, SB_HEADS, tq, KEY_TILE), BF16),
                        pltpu.VMEM((SB_HEADS, tq, 1), F32),
                        pltpu.VMEM((SB_HEADS, tq, SB_HEAD_DIM), F32),
                        pltpu.SMEM((2,), jnp.int32)],
        compiler_params=pltpu.CompilerParams(
            dimension_semantics=("arbitrary", "arbitrary"), vmem_limit_bytes=VMEM_LIMIT),
    )(sb4, kt)


def _sb_decode_kernel(sb_ref, kn_ref, ck_ref, cv_ref, o_ref):
    tq = sb_ref.shape[1]
    past = ck_ref.shape[4]
    suffix = _suffix_matrix(KEY_TILE)
    suffix_new = _suffix_matrix(tq)
    r = lax.broadcasted_iota(jnp.int32, (tq, tq), 0)
    c = lax.broadcasted_iota(jnp.int32, (tq, tq), 1)
    causal = c < r
    heads = range(SB_HEADS)
    head_lanes = [slice(hh * SB_HEAD_DIM, (hh + 1) * SB_HEAD_DIM) for hh in heads]
    group = lambda slot: sb_ref[0, :, slot * GROUP_W:(slot + 1) * GROUP_W]
    q_all, v_all = group(SB_Q), group(SB_V)
    qs = [q_all[:, lanes] for lanes in head_lanes]
    runs = [jnp.zeros((tq, 1), F32)] * SB_HEADS
    zs = [_dot_nt(q, kn_ref[0, :, lanes]) for q, lanes in zip(qs, head_lanes)]
    ws, runs = _sb_weights(zs, suffix_new, runs, causal)
    accs = [_dot(w, v_all[:, lanes]) for w, lanes in zip(ws, head_lanes)]
    for j in reversed(range(past // KEY_TILE)):
        keys = slice(j * KEY_TILE, (j + 1) * KEY_TILE)
        zs = [_dot(q, ck_ref[0, 0, hh, :, keys].astype(BF16)) for q, hh in zip(qs, heads)]
        ws, runs = _sb_weights(zs, suffix, runs, None)
        accs = [acc + _dot_nt(w, cv_ref[0, 0, hh, :, keys].astype(BF16))
                for w, hh, acc in zip(ws, heads, accs)]
    o = jnp.concatenate(accs, axis=1) * group(SB_GATE).astype(F32)
    o_ref[0] = o.astype(o_ref.dtype)


def _sb_decode(sb, kn, cache_k, cache_v, layer):
    B, T, _ = kn.shape
    past = cache_k.shape[3]
    assert past % KEY_TILE == 0
    row_spec = lambda width: pl.BlockSpec((1, T, width), lambda b: (b, 0, 0))
    cache_spec = pl.BlockSpec((1, 1, SB_HEADS, SB_HEAD_DIM, past), lambda b: (layer, b, 0, 0, 0))
    cache_k = jnp.swapaxes(cache_k, 3, 4)
    cache_v = jnp.swapaxes(cache_v, 3, 4)
    return pl.pallas_call(
        _sb_decode_kernel,
        grid=(B,),
        in_specs=[row_spec(3 * GROUP_W), row_spec(GROUP_W), cache_spec, cache_spec],
        out_specs=row_spec(GROUP_W),
        out_shape=jax.ShapeDtypeStruct((B, T, GROUP_W), BF16),
        compiler_params=pltpu.CompilerParams(
            dimension_semantics=("arbitrary",), vmem_limit_bytes=VMEM_LIMIT),
    )(sb, kn, cache_k, cache_v)


def _hgrn_tables(C):
    nl = int(math.log2(C))
    assert 1 << nl == C
    n_small = min(nl, int(math.log2(SUBLANES)))
    t = np.arange(C)[:, None]
    j = np.arange(C)[None, :]
    blocks = []
    for l in range(n_small):
        half = 1 << l
        mid = (t >> (l + 1) << (l + 1)) + half
        right = (t & half) != 0
        blocks.append(np.where(right, (j >= mid) & (j <= t), (j > t) & (j < mid)))
    blocks.append(j <= t)
    prefix = np.concatenate(blocks, axis=0).astype(np.float32)
    x = t ^ j
    msb = np.floor(np.log2(np.maximum(x, 1))).astype(np.int32)
    level_of = np.where(t > j, msb, np.where(t == j, nl, -1)).astype(np.int32)
    return jnp.asarray(prefix, BF16), jnp.asarray(level_of), nl, n_small


def _hgrn_out_kernel(*refs, C, sub, nl, n_small, layer, has_s0, alpha):
    (hg_ref, fh_ref, pm_ref, lv_ref, lbl_ref, ng_ref, ma_ref, x_ref, wo_ref,
     lg_ref, lnb_ref) = refs[:11]
    if has_s0:
        s0_ref, y_ref, so_ref, st_ref, b_ref = refs[11:]
    else:
        y_ref, so_ref, st_ref, b_ref = refs[11:]
    t = pl.program_id(1)

    def packed(rows, slot):
        return hg_ref[0, rows, slot * GROUP_W:(slot + 1) * GROUP_W]

    group = min(sub, OUT_GROUP_CHUNKS)
    group_rows = [slice(g * group * C, (g + 1) * group * C) for g in range(sub // group)]
    w_attn = wo_ref[:GROUP_W, :].astype(BF16)
    w_hgrn = wo_ref[GROUP_W:, :].astype(BF16)
    out_attn = [_dot(ma_ref[0, rows, :], w_attn) for rows in group_rows]

    @pl.when(t == 0)
    def _():
        if has_s0:
            for h in range(HG_HEADS):
                st_ref[h] = s0_ref[0, 0, h].T
        else:
            st_ref[...] = jnp.zeros_like(st_ref)

    logits = lbl_ref[...]
    e = jnp.exp(logits - jnp.max(logits, axis=0, keepdims=True))
    lb = jnp.sum(e[:layer + 1], axis=0, keepdims=True) / jnp.sum(e, axis=0, keepdims=True)

    def head(a, h):
        return a[:, h * HG_HEAD_DIM:(h + 1) * HG_HEAD_DIM]

    def prepare(c):
        rows = slice(c * C, (c + 1) * C)
        f = lb + (1.0 - lb) * _sigmoid(fh_ref[0, rows, :])
        g = jnp.log2(f)
        kin = 1.0 - f
        qh = packed(rows, HG_Q).astype(F32)
        qs = qh * _sigmoid(qh)
        g_hi, g_lo = _split_bf16(g, 2)
        pre = _dot(pm_ref[:n_small * C, :], g_hi)
        cum = _dot(pm_ref[n_small * C:, :], jnp.concatenate([g_hi, g_lo], axis=1))
        b_ref[c] = cum[:, :GROUP_W] + cum[:, GROUP_W:]
        return dict(rows=rows, qs=qs, kin=kin, pre=pre, qs_b=qs.astype(BF16),
                    kin_b=kin.astype(BF16))

    def level_log(c, pre, l):
        if l < n_small:
            return pre[l * C:(l + 1) * C]
        half = 1 << l
        parts = []
        for lo in range(0, C, 2 * half):
            mid = lo + half
            last_left = b_ref[c, mid - 1:mid, :]
            parts.append(last_left - b_ref[c, lo:mid, :])
            parts.append(b_ref[c, mid:mid + half, :] - last_left)
        return jnp.concatenate(parts, axis=0)

    def operands(c, p, l):
        if l == nl:
            return p["qs_b"], p["kin_b"]
        e_l = jnp.exp2(level_log(c, p["pre"], l)).astype(BF16)
        return p["qs_b"] * e_l, p["kin_b"] * e_l

    def finish(c, p, scores):
        rows = p["rows"]
        b = b_ref[c]
        b_last = b_ref[c, C - 1:C, :]
        q_in = (p["qs"] * jnp.exp2(b)).astype(BF16)
        k_out = (p["kin"] * jnp.exp2(b_last - b)).astype(BF16)
        chunk_decay = jnp.exp2(b_last)
        i_b = packed(rows, HG_I)
        gh = packed(rows, HG_GATE).astype(F32)
        gate = ng_ref[...] * (gh * _sigmoid(gh))
        outs = []
        for h in range(HG_HEADS):
            st = st_ref[h]
            o = (_dot(scores[h].astype(BF16), head(i_b, h))
                 + _dot_nt(head(q_in, h), st.astype(BF16)))
            st_ref[h] = st * head(chunk_decay, h) + _dot_tn(head(i_b, h), head(k_out, h))
            outs.append(o * lax.rsqrt(jnp.mean(o * o, axis=-1, keepdims=True) + RMS_EPS))
        return (jnp.concatenate(outs, axis=1) * gate).astype(BF16)

    def project_out(g, mixed_hgrn):
        rows = group_rows[g]
        hid = alpha * x_ref[0, rows, :] + (out_attn[g] + _dot(mixed_hgrn, w_hgrn))
        mu = jnp.mean(hid, axis=-1, keepdims=True)
        cen = hid - mu
        var = jnp.mean(cen * cen, axis=-1, keepdims=True)
        y_ref[0, rows, :] = cen * lax.rsqrt(var + LN_EPS) * lg_ref[...] + lnb_ref[...]

    chunks = range(sub)
    preps = [prepare(c) for c in chunks]
    level_of = lv_ref[...]
    order = [(l, c) for l in [nl] + list(range(nl)) for c in chunks]
    scores = [[0.0] * HG_HEADS for _ in chunks]
    nxt = operands(order[0][1], preps[order[0][1]], order[0][0])
    for idx, (l, c) in enumerate(order):
        q_l, k_l = nxt
        if idx + 1 < len(order):
            l_n, c_n = order[idx + 1]
            nxt = operands(c_n, preps[c_n], l_n)
        hit = level_of == l
        for h in range(HG_HEADS):
            scores[c][h] = jnp.where(hit, _dot_nt(head(q_l, h), head(k_l, h)), scores[c][h])
    mixed = []
    for c in chunks:
        mixed.append(finish(c, preps[c], scores[c]))
        if len(mixed) == group:
            project_out(c // group, mixed[0] if group == 1 else jnp.concatenate(mixed, axis=0))
            mixed = []

    @pl.when(t == pl.num_programs(1) - 1)
    def _():
        for h in range(HG_HEADS):
            so_ref[0, 0, h] = st_ref[h].T


def _hgrn_out(hg, fh, lb_logits, norm_g, s0, layer, ma, x, w_out, ln_g, ln_b, *, C, sub, alpha):
    B, T, _ = fh.shape
    prefix, level_of, nl, n_small = _hgrn_tables(C)
    has_s0 = s0 is not None
    rows = sub * C
    row_spec = pl.BlockSpec((1, rows, GROUP_W), lambda b, t: (b, t, 0))
    row3_spec = pl.BlockSpec((1, rows, 3 * GROUP_W), lambda b, t: (b, t, 0))
    wide_spec = pl.BlockSpec((1, rows, D_MODEL), lambda b, t: (b, t, 0))
    const = lambda shape: pl.BlockSpec(shape, lambda b, t: (0,) * len(shape))
    state_shape = (1, 1, HG_HEADS, HG_HEAD_DIM, HG_HEAD_DIM)
    in_specs = [row3_spec, row_spec, const(prefix.shape), const(level_of.shape),
                const(lb_logits.shape), const((1, GROUP_W)), row_spec, wide_spec,
                const(w_out.shape), const((1, D_MODEL)), const((1, D_MODEL))]
    args = [hg, fh, prefix, level_of, lb_logits.astype(F32),
            norm_g.reshape(1, GROUP_W).astype(F32), ma, x, w_out,
            ln_g.reshape(1, D_MODEL).astype(F32), ln_b.reshape(1, D_MODEL).astype(F32)]
    if has_s0:
        in_specs.append(pl.BlockSpec(state_shape, lambda b, t: (layer, b, 0, 0, 0)))
        args.append(s0)
    return pl.pallas_call(
        functools.partial(_hgrn_out_kernel, C=C, sub=sub, nl=nl, n_small=n_small, layer=layer,
                          has_s0=has_s0, alpha=alpha),
        grid=(B, T // rows),
        in_specs=in_specs,
        out_specs=[wide_spec, pl.BlockSpec(state_shape, lambda b, t: (0, b, 0, 0, 0))],
        out_shape=[jax.ShapeDtypeStruct((B, T, D_MODEL), F32),
                   jax.ShapeDtypeStruct((1, B, HG_HEADS, HG_HEAD_DIM, HG_HEAD_DIM), F32)],
        scratch_shapes=[pltpu.VMEM((HG_HEADS, HG_HEAD_DIM, HG_HEAD_DIM), F32),
                        pltpu.VMEM((sub, C, GROUP_W), F32)],
        compiler_params=pltpu.CompilerParams(
            dimension_semantics=("arbitrary", "arbitrary"), vmem_limit_bytes=VMEM_LIMIT),
    )(*args)


PROMPT_ROWS = 512
PROMPT_CHUNK = 128
OUT_GROUP_CHUNKS = 2
PROMPT_CHUNKS_PER_STEP = 4


def _layer(x, cache_k, cache_v, state_s, layer, w_in, w_out, lb_logits, norm_g, ln_g, ln_b, alpha):
    B, T, _ = x.shape
    decode = cache_k is not None
    if decode:
        sb, k, hg, fh, k_out, v_out = _project(x, w_in, nb=B, tm=T, k_transposed=False)
        ma = _sb_decode(sb, k, cache_k, cache_v, layer)
        y, s_out = _hgrn_out(hg, fh, lb_logits, norm_g, state_s, layer, ma, x, w_out, ln_g, ln_b,
                             C=T, sub=1, alpha=alpha)
    else:
        sb, kt, hg, fh, k_out, v_out = _project(
            x, w_in, nb=1, tm=min(PROMPT_ROWS, T), k_transposed=True)
        ma = _sb_prompt(sb, kt)
        chunk = min(PROMPT_CHUNK, T)
        y, s_out = _hgrn_out(hg, fh, lb_logits, norm_g, None, layer, ma, x, w_out, ln_g, ln_b,
                             C=chunk, sub=min(PROMPT_CHUNKS_PER_STEP, T // chunk), alpha=alpha)
    return y, k_out, v_out, s_out


def kernel(x_prompt, x_sample, cache_k, cache_v, state_s, w_in, w_out, lb_logits, hgrn_norm_g,
           ln_g, ln_b):
    depth = w_in.shape[0]
    alpha = (2 * depth) ** 0.25
    yp, ys = x_prompt, x_sample
    per_layer = []
    for l in range(depth):
        common = (l, w_in[l], w_out[l], lb_logits, hgrn_norm_g[l], ln_g[l], ln_b[l], alpha)
        yp, kp, vp, sp = _layer(yp, None, None, None, *common)
        ys, kn, vn, sn = _layer(ys, cache_k, cache_v, state_s, *common)
        per_layer.append((kp, vp, sp, kn, vn, sn))
    stack = lambda i: (per_layer[0][i] if depth == 1
                       else jnp.concatenate([p[i] for p in per_layer], axis=0))
    return (yp, ys, stack(0), stack(1), stack(2), stack(3), stack(4), stack(5))
```

```python
import functools
import math

import numpy as np
import jax
import jax.numpy as jnp
from jax import lax
from jax.experimental import pallas as pl
from jax.experimental.pallas import tpu as pltpu

F32 = jnp.float32
BF16 = jnp.bfloat16

D_MODEL = 1024
GROUP_W = 512
N_GROUPS = 8
SB_HEADS = 8
SB_HEAD_DIM = 64
HG_HEADS = 4
HG_HEAD_DIM = 128
LN_EPS = 1e-5
RMS_EPS = 1e-6

SUBLANES = 8
KEY_TILE = 256
SB_Q, SB_V, SB_GATE = 0, 1, 2
HG_Q, HG_I, HG_GATE = 0, 1, 2
NEGLIGIBLE_LOG2 = 200.0
LOG2E = 1.4426950408889634
VMEM_LIMIT = 56 * 1024 * 1024


def _sigmoid(x):
    return 1.0 / (1.0 + jnp.exp2(x * (-LOG2E)))


def _dot(a, b):
    return jnp.dot(a, b, preferred_element_type=F32)


def _dot_nt(a, b):
    return lax.dot_general(a, b, (((1,), (1,)), ((), ())), preferred_element_type=F32)


def _dot_tn(a, b):
    return lax.dot_general(a, b, (((0,), (0,)), ((), ())), preferred_element_type=F32)


def _split_bf16(x, parts):
    out = []
    for _ in range(parts - 1):
        p = x.astype(BF16)
        out.append(p)
        x = x - p.astype(F32)
    out.append(x.astype(BF16))
    return out


def _proj_kernel(x_ref, w_ref, sb_ref, k_ref, hg_ref, fh_ref, ko_ref, vo_ref, *, nb, tm,
                 k_transposed):
    m = nb * tm
    x = x_ref[...].reshape(m, D_MODEL).astype(BF16)

    def col(c):
        return _dot(x, w_ref[:, c * GROUP_W:(c + 1) * GROUP_W].astype(BF16))

    def put(ref, val, slot=0):
        ref[:, :, slot * GROUP_W:(slot + 1) * GROUP_W] = (
            val.reshape(nb, tm, GROUP_W).astype(ref.dtype))

    def put_heads(ref, val):
        if k_transposed:
            val_t = val.T
            for h in range(SB_HEADS):
                ref[0, 0, h] = val_t[h * SB_HEAD_DIM:(h + 1) * SB_HEAD_DIM, :]
            return val_t
        for b in range(nb):
            for h in range(SB_HEADS):
                ref[0, b, h] = val[b * tm:(b + 1) * tm, h * SB_HEAD_DIM:(h + 1) * SB_HEAD_DIM]

    put(sb_ref, col(0) * (SB_HEAD_DIM ** -0.5 * LOG2E), SB_Q)
    k = col(1)
    kt = put_heads(ko_ref, k)
    if k_transposed:
        kt = kt.astype(BF16)
        for j in range(m // KEY_TILE):
            k_ref[0, j] = kt[:, j * KEY_TILE:(j + 1) * KEY_TILE]
    else:
        put(k_ref, k)
    v = col(2)
    put_heads(vo_ref, v)
    put(sb_ref, v, SB_V)
    ga = col(3)
    put(sb_ref, ga * _sigmoid(ga), SB_GATE)
    put(hg_ref, col(4), HG_Q)
    put(fh_ref, col(5))
    put(hg_ref, col(6), HG_I)
    put(hg_ref, col(7), HG_GATE)


def _project(x, w, *, nb, tm, k_transposed):
    B, T, _ = x.shape
    grid = (B // nb, T // tm)
    act = lambda dt: jax.ShapeDtypeStruct((B, T, GROUP_W), dt)
    act_spec = pl.BlockSpec((nb, tm, GROUP_W), lambda b, t: (b, t, 0))
    act3 = jax.ShapeDtypeStruct((B, T, 3 * GROUP_W), BF16)
    act3_spec = pl.BlockSpec((nb, tm, 3 * GROUP_W), lambda b, t: (b, t, 0))
    if k_transposed:
        assert nb == 1 and tm % KEY_TILE == 0
        k_shape = jax.ShapeDtypeStruct((B, T // KEY_TILE, GROUP_W, KEY_TILE), BF16)
        k_spec = pl.BlockSpec((1, tm // KEY_TILE, GROUP_W, KEY_TILE), lambda b, t: (b, t, 0, 0))
        kv_shape = jax.ShapeDtypeStruct((1, B, SB_HEADS, SB_HEAD_DIM, T), F32)
        kv_spec = pl.BlockSpec((1, 1, SB_HEADS, SB_HEAD_DIM, tm), lambda b, t: (0, b, 0, 0, t))
    else:
        k_shape, k_spec = act(BF16), act_spec
        kv_shape = jax.ShapeDtypeStruct((1, B, SB_HEADS, T, SB_HEAD_DIM), F32)
        kv_spec = pl.BlockSpec((1, nb, SB_HEADS, tm, SB_HEAD_DIM), lambda b, t: (0, b, 0, t, 0))
    outs = pl.pallas_call(
        functools.partial(_proj_kernel, nb=nb, tm=tm, k_transposed=k_transposed),
        grid=grid,
        in_specs=[pl.BlockSpec((nb, tm, D_MODEL), lambda b, t: (b, t, 0)),
                  pl.BlockSpec((D_MODEL, N_GROUPS * GROUP_W), lambda b, t: (0, 0),
                               pipeline_mode=pl.Buffered(1))],
        out_specs=[act3_spec, k_spec, act3_spec, act_spec, kv_spec, kv_spec],
        out_shape=[act3, k_shape, act3, act(F32), kv_shape, kv_shape],
        compiler_params=pltpu.CompilerParams(
            dimension_semantics=("arbitrary", "arbitrary"), vmem_limit_bytes=VMEM_LIMIT),
    )(x, w)
    if k_transposed:
        outs = list(outs[:4]) + [jnp.swapaxes(o, 3, 4) for o in outs[4:]]
    return outs


def _suffix_matrix(n):
    r = lax.broadcasted_iota(jnp.int32, (n, n), 0)
    c = lax.broadcasted_iota(jnp.int32, (n, n), 1)
    return jnp.where(r >= c, 1.0, 0.0).astype(BF16)


def _sb_weights(zs, suffix, runs, causal):
    incls = _sb_suffix_sums(zs, suffix, causal)
    ws = [_sb_weight(z, incl, run, causal) for z, incl, run in zip(zs, incls, runs)]
    return ws, [run + incl[:, 0:1] for run, incl in zip(runs, incls)]


def _sb_suffix_sums(zs, suffix, causal):
    drops = []
    for z in zs:
        drop = jnp.maximum(z, 0.0) + jnp.log2(1.0 + jnp.exp2(-jnp.abs(z)))
        if causal is not None:
            drop = jnp.where(causal, drop, 0.0)
        drops.append(drop.astype(BF16))
    return [_dot(drop, suffix) for drop in drops]


def _sb_weight(z, incl, run, causal):
    w = jnp.exp2(z - incl - run)
    if causal is not None:
        w = jnp.where(causal, w, 0.0)
    return w.astype(BF16)


def _sb_prompt_kernel(sb_ref, kt_ref, o_ref, z_ref, w_ref, run_ref, acc_ref, state_ref, *, heads):
    tq = KEY_TILE
    nq = sb_ref.shape[1]
    i = pl.program_id(1)
    suffix = _suffix_matrix(KEY_TILE)
    r = lax.broadcasted_iota(jnp.int32, (tq, KEY_TILE), 0)
    c = lax.broadcasted_iota(jnp.int32, (tq, KEY_TILE), 1)
    causal = c < r
    head_lanes = [slice(hh * SB_HEAD_DIM, (hh + 1) * SB_HEAD_DIM) for hh in range(heads)]

    def packed(slot, lanes):
        return slice(slot * GROUP_W + lanes.start, slot * GROUP_W + lanes.stop)

    def logits(bi, j, hh):
        return _dot(sb_ref[0, bi, :, packed(SB_Q, head_lanes[hh])],
                    kt_ref[0, j, head_lanes[hh], :])

    def add_weighted_values(slot, j):
        for hh in range(heads):
            acc_ref[hh] += _dot(w_ref[slot, hh], sb_ref[0, j, :, packed(SB_V, head_lanes[hh])])

    def finalize(bi):
        o = jnp.concatenate([acc_ref[hh] for hh in range(heads)], axis=1)
        gate = sb_ref[0, bi, :, SB_GATE * GROUP_W:(SB_GATE + 1) * GROUP_W].astype(F32)
        o_ref[0, pl.ds(pl.multiple_of(bi * tq, tq), tq), :] = (o * gate).astype(o_ref.dtype)

    def step(cur, masked, next_bi, next_j):
        mask = causal if masked else None
        zs = [z_ref[cur, hh] for hh in range(heads)]
        incls = _sb_suffix_sums(zs, suffix, mask)
        for hh in range(heads):
            z_ref[1 - cur, hh] = logits(next_bi, next_j, hh)
            run = jnp.zeros((tq, 1), F32) if masked else run_ref[hh]
            w_ref[1 - cur, hh] = _sb_weight(zs[hh], incls[hh], run, mask)
            run_ref[hh] = run + incls[hh][:, 0:1]

    def refill(slot, bi, j):
        for hh in range(heads):
            z_ref[slot, hh] = logits(bi, j, hh)

    def per_slot(slot, cond, fn):
        for parity in range(2):
            pl.when(jnp.logical_and(cond, slot == parity))(functools.partial(fn, parity))

    @pl.when(i == 0)
    def _():
        for hh in range(heads):
            z_ref[0, hh] = logits(0, 0, hh)
        w_ref[0] = jnp.zeros_like(w_ref[0])
        acc_ref[...] = jnp.zeros_like(acc_ref)
        state_ref[0] = 0
        state_ref[1] = 0

    first_slot = state_ref[0]
    pending_tile = state_ref[1]
    nxt = jnp.minimum(i + 1, nq - 1)

    def enter_block(parity):
        add_weighted_values(parity, pending_tile)
        finalize(jnp.maximum(i - 1, 0))
        acc_ref[...] = jnp.zeros_like(acc_ref)

    def first_block(parity):
        enter_block(parity)
        step(parity, True, nxt, nxt)

    def later_block(parity):
        enter_block(parity)
        step(parity, True, i, i - 1)
        add_weighted_values(1 - parity, i)
        step(1 - parity, False, nxt, nxt)

    per_slot(first_slot, i == 0, first_block)
    per_slot(first_slot, i > 0, later_block)
    tiles = jnp.where(i > 0, 2, 1)
    slot = (first_slot + tiles) % 2
    last_tile = jnp.maximum(i - 1, 0)

    def weight_left():
        lowest = run_ref[0]
        for hh in range(1, heads):
            lowest = jnp.minimum(lowest, run_ref[hh])
        return jnp.min(lowest) < NEGLIGIBLE_LOG2

    go_on = jnp.logical_and(last_tile > 0, weight_left())
    per_slot(slot, go_on, lambda parity: refill(parity, i, last_tile - 1))

    def sweep(carry):
        n, _ = carry
        j = last_tile - 1 - n

        def tile(parity):
            add_weighted_values(parity, j + 1)
            same_block = j > 0
            step(parity, False, jnp.where(same_block, i, nxt), jnp.where(same_block, j - 1, nxt))

        per_slot((slot + n) % 2, True, tile)
        return n + 1, jnp.logical_and(j > 0, weight_left())

    swept, _ = lax.while_loop(lambda carry: carry[1], sweep, (jnp.int32(0), go_on))
    slot = (slot + swept) % 2
    last_tile = last_tile - swept
    per_slot(slot, jnp.logical_and(swept > 0, last_tile > 0),
             lambda parity: refill(parity, nxt, nxt))
    state_ref[0] = slot
    state_ref[1] = last_tile

    @pl.when(i == nq - 1)
    def _():
        per_slot(slot, True, lambda parity: add_weighted_values(parity, last_tile))
        finalize(i)


def _sb_prompt(sb, kt):
    B, T, _ = sb.shape
    tq = KEY_TILE
    nkt = T // KEY_TILE
    sb4 = sb.reshape(B, nkt, KEY_TILE, 3 * GROUP_W)
    row_spec = pl.BlockSpec((1, T, GROUP_W), lambda b, i: (b, 0, 0))
    tile_spec = lambda shape: pl.BlockSpec((1,) + shape, lambda b, i: (b, 0, 0, 0))
    return pl.pallas_call(
        functools.partial(_sb_prompt_kernel, heads=SB_HEADS),
        grid=(B, T // tq),
        in_specs=[tile_spec((nkt, KEY_TILE, 3 * GROUP_W)), tile_spec((nkt, GROUP_W, KEY_TILE))],
        out_specs=row_spec,
        out_shape=jax.ShapeDtypeStruct((B, T, GROUP_W), BF16),
        scratch_shapes=[pltpu.VMEM((2, SB_HEADS, tq, KEY_TILE), F32),
                        pltpu.VMEM((2, SB_HEADS, tq, KEY_TILE), BF16),
                        pltpu.VMEM((SB_HEADS, tq, 1), F32),
                        pltpu.VMEM((SB_HEADS, tq, SB_HEAD_DIM), F32),
                        pltpu.SMEM((2,), jnp.int32)],
        compiler_params=pltpu.CompilerParams(
            dimension_semantics=("arbitrary", "arbitrary"), vmem_limit_bytes=VMEM_LIMIT),
    )(sb4, kt)


def _sb_decode_kernel(sb_ref, kn_ref, ck_ref, cv_ref, o_ref):
    tq = sb_ref.shape[1]
    past = ck_ref.shape[4]
    suffix = _suffix_matrix(KEY_TILE)
    suffix_new = _suffix_matrix(tq)
    r = lax.broadcasted_iota(jnp.int32, (tq, tq), 0)
    c = lax.broadcasted_iota(jnp.int32, (tq, tq), 1)
    causal = c < r
    heads = range(SB_HEADS)
    head_lanes = [slice(hh * SB_HEAD_DIM, (hh + 1) * SB_HEAD_DIM) for hh in heads]
    group = lambda slot: sb_ref[0, :, slot * GROUP_W:(slot + 1) * GROUP_W]
    q_all, v_all = group(SB_Q), group(SB_V)
    qs = [q_all[:, lanes] for lanes in head_lanes]
    runs = [jnp.zeros((tq, 1), F32)] * SB_HEADS
    zs = [_dot_nt(q, kn_ref[0, :, lanes]) for q, lanes in zip(qs, head_lanes)]
    ws, runs = _sb_weights(zs, suffix_new, runs, causal)
    accs = [_dot(w, v_all[:, lanes]) for w, lanes in zip(ws, head_lanes)]
    for j in reversed(range(past // KEY_TILE)):
        keys = slice(j * KEY_TILE, (j + 1) * KEY_TILE)
        zs = [_dot(q, ck_ref[0, 0, hh, :, keys].astype(BF16)) for q, hh in zip(qs, heads)]
        ws, runs = _sb_weights(zs, suffix, runs, None)
        accs = [acc + _dot_nt(w, cv_ref[0, 0, hh, :, keys].astype(BF16))
                for w, hh, acc in zip(ws, heads, accs)]
    o = jnp.concatenate(accs, axis=1) * group(SB_GATE).astype(F32)
    o_ref[0] = o.astype(o_ref.dtype)


def _sb_decode(sb, kn, cache_k, cache_v, layer):
    B, T, _ = kn.shape
    past = cache_k.shape[3]
    assert past % KEY_TILE == 0
    row_spec = lambda width: pl.BlockSpec((1, T, width), lambda b: (b, 0, 0))
    cache_spec = pl.BlockSpec((1, 1, SB_HEADS, SB_HEAD_DIM, past), lambda b: (layer, b, 0, 0, 0))
    cache_k = jnp.swapaxes(cache_k, 3, 4)
    cache_v = jnp.swapaxes(cache_v, 3, 4)
    return pl.pallas_call(
        _sb_decode_kernel,
        grid=(B,),
        in_specs=[row_spec(3 * GROUP_W), row_spec(GROUP_W), cache_spec, cache_spec],
        out_specs=row_spec(GROUP_W),
        out_shape=jax.ShapeDtypeStruct((B, T, GROUP_W), BF16),
        compiler_params=pltpu.CompilerParams(
            dimension_semantics=("arbitrary",), vmem_limit_bytes=VMEM_LIMIT),
    )(sb, kn, cache_k, cache_v)


def _hgrn_tables(C):
    nl = int(math.log2(C))
    assert 1 << nl == C
    n_small = min(nl, int(math.log2(SUBLANES)))
    t = np.arange(C)[:, None]
    j = np.arange(C)[None, :]
    blocks = []
    for l in range(n_small):
        half = 1 << l
        mid = (t >> (l + 1) << (l + 1)) + half
        right = (t & half) != 0
        blocks.append(np.where(right, (j >= mid) & (j <= t), (j > t) & (j < mid)))
    blocks.append(j <= t)
    prefix = np.concatenate(blocks, axis=0).astype(np.float32)
    x = t ^ j
    msb = np.floor(np.log2(np.maximum(x, 1))).astype(np.int32)
    level_of = np.where(t > j, msb, np.where(t == j, nl, -1)).astype(np.int32)
    return jnp.asarray(prefix, BF16), jnp.asarray(level_of), nl, n_small


def _hgrn_out_kernel(*refs, C, sub, nl, n_small, layer, has_s0, alpha):
    (hg_ref, fh_ref, pm_ref, lv_ref, lbl_ref, ng_ref, ma_ref, x_ref, wo_ref,
     lg_ref, lnb_ref) = refs[:11]
    if has_s0:
        s0_ref, y_ref, so_ref, st_ref, b_ref = refs[11:]
    else:
        y_ref, so_ref, st_ref, b_ref = refs[11:]
    t = pl.program_id(1)

    def packed(rows, slot):
        return hg_ref[0, rows, slot * GROUP_W:(slot + 1) * GROUP_W]

    group = min(sub, OUT_GROUP_CHUNKS)
    group_rows = [slice(g * group * C, (g + 1) * group * C) for g in range(sub // group)]
    w_attn = wo_ref[:GROUP_W, :].astype(BF16)
    w_hgrn = wo_ref[GROUP_W:, :].astype(BF16)
    out_attn = [_dot(ma_ref[0, rows, :], w_attn) for rows in group_rows]

    @pl.when(t == 0)
    def _():
        if has_s0:
            for h in range(HG_HEADS):
                st_ref[h] = s0_ref[0, 0, h].T
        else:
            st_ref[...] = jnp.zeros_like(st_ref)

    logits = lbl_ref[...]
    e = jnp.exp(logits - jnp.max(logits, axis=0, keepdims=True))
    lb = jnp.sum(e[:layer + 1], axis=0, keepdims=True) / jnp.sum(e, axis=0, keepdims=True)

    def head(a, h):
        return a[:, h * HG_HEAD_DIM:(h + 1) * HG_HEAD_DIM]

    def prepare(c):
        rows = slice(c * C, (c + 1) * C)
        f = lb + (1.0 - lb) * _sigmoid(fh_ref[0, rows, :])
        g = jnp.log2(f)
        kin = 1.0 - f
        qh = packed(rows, HG_Q).astype(F32)
        qs = qh * _sigmoid(qh)
        g_hi, g_lo = _split_bf16(g, 2)
        pre = _dot(pm_ref[:n_small * C, :], g_hi)
        cum = _dot(pm_ref[n_small * C:, :], jnp.concatenate([g_hi, g_lo], axis=1))
        b_ref[c] = cum[:, :GROUP_W] + cum[:, GROUP_W:]
        return dict(rows=rows, qs=qs, kin=kin, pre=pre, qs_b=qs.astype(BF16),
                    kin_b=kin.astype(BF16))

    def level_log(c, pre, l):
        if l < n_small:
            return pre[l * C:(l + 1) * C]
        half = 1 << l
        parts = []
        for lo in range(0, C, 2 * half):
            mid = lo + half
            last_left = b_ref[c, mid - 1:mid, :]
            parts.append(last_left - b_ref[c, lo:mid, :])
            parts.append(b_ref[c, mid:mid + half, :] - last_left)
        return jnp.concatenate(parts, axis=0)

    def operands(c, p, l):
        if l == nl:
            return p["qs_b"], p["kin_b"]
        e_l = jnp.exp2(level_log(c, p["pre"], l)).astype(BF16)
        return p["qs_b"] * e_l, p["kin_b"] * e_l

    def finish(c, p, scores):
        rows = p["rows"]
        b = b_ref[c]
        b_last = b_ref[c, C - 1:C, :]
        q_in = (p["qs"] * jnp.exp2(b)).astype(BF16)
        k_out = (p["kin"] * jnp.exp2(b_last - b)).astype(BF16)
        chunk_decay = jnp.exp2(b_last)
        i_b = packed(rows, HG_I)
        gh = packed(rows, HG_GATE).astype(F32)
        gate = ng_ref[...] * (gh * _sigmoid(gh))
        outs = []
        for h in range(HG_HEADS):
            st = st_ref[h]
            o = (_dot(scores[h].astype(BF16), head(i_b, h))
                 + _dot_nt(head(q_in, h), st.astype(BF16)))
            st_ref[h] = st * head(chunk_decay, h) + _dot_tn(head(i_b, h), head(k_out, h))
            outs.append(o * lax.rsqrt(jnp.mean(o * o, axis=-1, keepdims=True) + RMS_EPS))
        return (jnp.concatenate(outs, axis=1) * gate).astype(BF16)

    def project_out(g, mixed_hgrn):
        rows = group_rows[g]
        hid = alpha * x_ref[0, rows, :] + (out_attn[g] + _dot(mixed_hgrn, w_hgrn))
        mu = jnp.mean(hid, axis=-1, keepdims=True)
        cen = hid - mu
        var = jnp.mean(cen * cen, axis=-1, keepdims=True)
        y_ref[0, rows, :] = cen * lax.rsqrt(var + LN_EPS) * lg_ref[...] + lnb_ref[...]

    chunks = range(sub)
    preps = [prepare(c) for c in chunks]
    level_of = lv_ref[...]
    order = [(l, c) for l in [nl] + list(range(nl)) for c in chunks]
    scores = [[0.0] * HG_HEADS for _ in chunks]
    nxt = operands(order[0][1], preps[order[0][1]], order[0][0])
    for idx, (l, c) in enumerate(order):
        q_l, k_l = nxt
        if idx + 1 < len(order):
            l_n, c_n = order[idx + 1]
            nxt = operands(c_n, preps[c_n], l_n)
        hit = level_of == l
        for h in range(HG_HEADS):
            scores[c][h] = jnp.where(hit, _dot_nt(head(q_l, h), head(k_l, h)), scores[c][h])
    mixed = []
    for c in chunks:
        mixed.append(finish(c, preps[c], scores[c]))
        if len(mixed) == group:
            project_out(c // group, mixed[0] if group == 1 else jnp.concatenate(mixed, axis=0))
            mixed = []

    @pl.when(t == pl.num_programs(1) - 1)
    def _():
        for h in range(HG_HEADS):
            so_ref[0, 0, h] = st_ref[h].T


def _hgrn_out(hg, fh, lb_logits, norm_g, s0, layer, ma, x, w_out, ln_g, ln_b, *, C, sub, alpha):
    B, T, _ = fh.shape
    prefix, level_of, nl, n_small = _hgrn_tables(C)
    has_s0 = s0 is not None
    rows = sub * C
    row_spec = pl.BlockSpec((1, rows, GROUP_W), lambda b, t: (b, t, 0))
    row3_spec = pl.BlockSpec((1, rows, 3 * GROUP_W), lambda b, t: (b, t, 0))
    wide_spec = pl.BlockSpec((1, rows, D_MODEL), lambda b, t: (b, t, 0))
    const = lambda shape: pl.BlockSpec(shape, lambda b, t: (0,) * len(shape))
    state_shape = (1, 1, HG_HEADS, HG_HEAD_DIM, HG_HEAD_DIM)
    in_specs = [row3_spec, row_spec, const(prefix.shape), const(level_of.shape),
                const(lb_logits.shape), const((1, GROUP_W)), row_spec, wide_spec,
                const(w_out.shape), const((1, D_MODEL)), const((1, D_MODEL))]
    args = [hg, fh, prefix, level_of, lb_logits.astype(F32),
            norm_g.reshape(1, GROUP_W).astype(F32), ma, x, w_out,
            ln_g.reshape(1, D_MODEL).astype(F32), ln_b.reshape(1, D_MODEL).astype(F32)]
    if has_s0:
        in_specs.append(pl.BlockSpec(state_shape, lambda b, t: (layer, b, 0, 0, 0)))
        args.append(s0)
    return pl.pallas_call(
        functools.partial(_hgrn_out_kernel, C=C, sub=sub, nl=nl, n_small=n_small, layer=layer,
                          has_s0=has_s0, alpha=alpha),
        grid=(B, T // rows),
        in_specs=in_specs,
        out_specs=[wide_spec, pl.BlockSpec(state_shape, lambda b, t: (0, b, 0, 0, 0))],
        out_shape=[jax.ShapeDtypeStruct((B, T, D_MODEL), F32),
                   jax.ShapeDtypeStruct((1, B, HG_HEADS, HG_HEAD_DIM, HG_HEAD_DIM), F32)],
        scratch_shapes=[pltpu.VMEM((HG_HEADS, HG_HEAD_DIM, HG_HEAD_DIM), F32),
                        pltpu.VMEM((sub, C, GROUP_W), F32)],
        compiler_params=pltpu.CompilerParams(
            dimension_semantics=("arbitrary", "arbitrary"), vmem_limit_bytes=VMEM_LIMIT),
    )(*args)


PROMPT_ROWS = 512
PROMPT_CHUNK = 128
OUT_GROUP_CHUNKS = 2
PROMPT_CHUNKS_PER_STEP = 4


def _layer(x, cache_k, cache_v, state_s, layer, w_in, w_out, lb_logits, norm_g, ln_g, ln_b, alpha):
    B, T, _ = x.shape
    decode = cache_k is not None
    if decode:
        sb, k, hg, fh, k_out, v_out = _project(x, w_in, nb=B, tm=T, k_transposed=False)
        ma = _sb_decode(sb, k, cache_k, cache_v, layer)
        y, s_out = _hgrn_out(hg, fh, lb_logits, norm_g, state_s, layer, ma, x, w_out, ln_g, ln_b,
                             C=T, sub=1, alpha=alpha)
    else:
        sb, kt, hg, fh, k_out, v_out = _project(
            x, w_in, nb=1, tm=min(PROMPT_ROWS, T), k_transposed=True)
        ma = _sb_prompt(sb, kt)
        chunk = min(PROMPT_CHUNK, T)
        y, s_out = _hgrn_out(hg, fh, lb_logits, norm_g, None, layer, ma, x, w_out, ln_g, ln_b,
                             C=chunk, sub=min(PROMPT_CHUNKS_PER_STEP, T // chunk), alpha=alpha)
    return y, k_out, v_out, s_out


def kernel(x_prompt, x_sample, cache_k, cache_v, state_s, w_in, w_out, lb_logits, hgrn_norm_g,
           ln_g, ln_b):
    depth = w_in.shape[0]
    alpha = (2 * depth) ** 0.25
    yp, ys = x_prompt, x_sample
    per_layer = []
    for l in range(depth):
        common = (l, w_in[l], w_out[l], lb_logits, hgrn_norm_g[l], ln_g[l], ln_b[l], alpha)
        yp, kp, vp, sp = _layer(yp, None, None, None, *common)
        ys, kn, vn, sn = _layer(ys, cache_k, cache_v, state_s, *common)
        per_layer.append((kp, vp, sp, kn, vn, sn))
    stack = lambda i: (per_layer[0][i] if depth == 1
                       else jnp.concatenate([p[i] for p in per_layer], axis=0))
    return (yp, ys, stack(0), stack(1), stack(2), stack(3), stack(4), stack(5))
```

```python
import functools
import math

import numpy as np
import jax
import jax.numpy as jnp
from jax import lax
from jax.experimental import pallas as pl
from jax.experimental.pallas import tpu as pltpu

F32 = jnp.float32
BF16 = jnp.bfloat16

D_MODEL = 1024
GROUP_W = 512
N_GROUPS = 8
SB_HEADS = 8
SB_HEAD_DIM = 64
HG_HEADS = 4
HG_HEAD_DIM = 128
LN_EPS = 1e-5
RMS_EPS = 1e-6

SUBLANES = 8
KEY_TILE = 256
SB_Q, SB_V, SB_GATE = 0, 1, 2
HG_Q, HG_I, HG_GATE = 0, 1, 2
NEGLIGIBLE_LOG2 = 200.0
LOG2E = 1.4426950408889634
VMEM_LIMIT = 56 * 1024 * 1024


def _sigmoid(x):
    return 1.0 / (1.0 + jnp.exp2(x * (-LOG2E)))


def _dot(a, b):
    return jnp.dot(a, b, preferred_element_type=F32)


def _dot_nt(a, b):
    return lax.dot_general(a, b, (((1,), (1,)), ((), ())), preferred_element_type=F32)


def _dot_tn(a, b):
    return lax.dot_general(a, b, (((0,), (0,)), ((), ())), preferred_element_type=F32)


def _split_bf16(x, parts):
    out = []
    for _ in range(parts - 1):
        p = x.astype(BF16)
        out.append(p)
        x = x - p.astype(F32)
    out.append(x.astype(BF16))
    return out


def _proj_kernel(x_ref, w_ref, sb_ref, k_ref, hg_ref, fh_ref, ko_ref, vo_ref, *, nb, tm,
                 k_transposed):
    m = nb * tm
    x = x_ref[...].reshape(m, D_MODEL).astype(BF16)

    def col(c):
        return _dot(x, w_ref[:, c * GROUP_W:(c + 1) * GROUP_W].astype(BF16))

    def put(ref, val, slot=0):
        ref[:, :, slot * GROUP_W:(slot + 1) * GROUP_W] = (
            val.reshape(nb, tm, GROUP_W).astype(ref.dtype))

    def put_heads(ref, val):
        if k_transposed:
            val_t = val.T
            for h in range(SB_HEADS):
                ref[0, 0, h] = val_t[h * SB_HEAD_DIM:(h + 1) * SB_HEAD_DIM, :]
            return val_t
        for b in range(nb):
            for h in range(SB_HEADS):
                ref[0, b, h] = val[b * tm:(b + 1) * tm, h * SB_HEAD_DIM:(h + 1) * SB_HEAD_DIM]

    put(sb_ref, col(0) * (SB_HEAD_DIM ** -0.5 * LOG2E), SB_Q)
    k = col(1)
    kt = put_heads(ko_ref, k)
    if k_transposed:
        kt = kt.astype(BF16)
        for j in range(m // KEY_TILE):
            k_ref[0, j] = kt[:, j * KEY_TILE:(j + 1) * KEY_TILE]
    else:
        put(k_ref, k)
    v = col(2)
    put_heads(vo_ref, v)
    put(sb_ref, v, SB_V)
    ga = col(3)
    put(sb_ref, ga * _sigmoid(ga), SB_GATE)
    put(hg_ref, col(4), HG_Q)
    put(fh_ref, col(5))
    put(hg_ref, col(6), HG_I)
    put(hg_ref, col(7), HG_GATE)


def _project(x, w, *, nb, tm, k_transposed):
    B, T, _ = x.shape
    grid = (B // nb, T // tm)
    act = lambda dt: jax.ShapeDtypeStruct((B, T, GROUP_W), dt)
    act_spec = pl.BlockSpec((nb, tm, GROUP_W), lambda b, t: (b, t, 0))
    act3 = jax.ShapeDtypeStruct((B, T, 3 * GROUP_W), BF16)
    act3_spec = pl.BlockSpec((nb, tm, 3 * GROUP_W), lambda b, t: (b, t, 0))
    if k_transposed:
        assert nb == 1 and tm % KEY_TILE == 0
        k_shape = jax.ShapeDtypeStruct((B, T // KEY_TILE, GROUP_W, KEY_TILE), BF16)
        k_spec = pl.BlockSpec((1, tm // KEY_TILE, GROUP_W, KEY_TILE), lambda b, t: (b, t, 0, 0))
        kv_shape = jax.ShapeDtypeStruct((1, B, SB_HEADS, SB_HEAD_DIM, T), F32)
        kv_spec = pl.BlockSpec((1, 1, SB_HEADS, SB_HEAD_DIM, tm), lambda b, t: (0, b, 0, 0, t))
    else:
        k_shape, k_spec = act(BF16), act_spec
        kv_shape = jax.ShapeDtypeStruct((1, B, SB_HEADS, T, SB_HEAD_DIM), F32)
        kv_spec = pl.BlockSpec((1, nb, SB_HEADS, tm, SB_HEAD_DIM), lambda b, t: (0, b, 0, t, 0))
    outs = pl.pallas_call(
        functools.partial(_proj_kernel, nb=nb, tm=tm, k_transposed=k_transposed),
        grid=grid,
        in_specs=[pl.BlockSpec((nb, tm, D_MODEL), lambda b, t: (b, t, 0)),
                  pl.BlockSpec((D_MODEL, N_GROUPS * GROUP_W), lambda b, t: (0, 0),
                               pipeline_mode=pl.Buffered(1))],
        out_specs=[act3_spec, k_spec, act3_spec, act_spec, kv_spec, kv_spec],
        out_shape=[act3, k_shape, act3, act(F32), kv_shape, kv_shape],
        compiler_params=pltpu.CompilerParams(
            dimension_semantics=("arbitrary", "arbitrary"), vmem_limit_bytes=VMEM_LIMIT),
    )(x, w)
    if k_transposed:
        outs = list(outs[:4]) + [jnp.swapaxes(o, 3, 4) for o in outs[4:]]
    return outs


def _suffix_matrix(n):
    r = lax.broadcasted_iota(jnp.int32, (n, n), 0)
    c = lax.broadcasted_iota(jnp.int32, (n, n), 1)
    return jnp.where(r >= c, 1.0, 0.0).astype(BF16)


def _sb_suffix_sums(zs, suffix, causal):
    drops = []
    for z in zs:
        drop = jnp.maximum(z, 0.0) + jnp.log2(1.0 + jnp.exp2(-jnp.abs(z)))
        if causal is not None:
            drop = jnp.where(causal, drop, 0.0)
        drops.append(drop.astype(BF16))
    return [_dot(drop, suffix) for drop in drops]


def _sb_weight(z, incl, run, causal):
    w = jnp.exp2(z - incl - run)
    if causal is not None:
        w = jnp.where(causal, w, 0.0)
    return w.astype(BF16)


def _sb_prompt_kernel(sb_ref, kt_ref, o_ref, z_ref, w_ref, run_ref, acc_ref, state_ref, *, heads):
    tq = KEY_TILE
    nq = sb_ref.shape[1]
    i = pl.program_id(1)
    suffix = _suffix_matrix(KEY_TILE)
    r = lax.broadcasted_iota(jnp.int32, (tq, KEY_TILE), 0)
    c = lax.broadcasted_iota(jnp.int32, (tq, KEY_TILE), 1)
    causal = c < r
    head_lanes = [slice(hh * SB_HEAD_DIM, (hh + 1) * SB_HEAD_DIM) for hh in range(heads)]

    def packed(slot, lanes):
        return slice(slot * GROUP_W + lanes.start, slot * GROUP_W + lanes.stop)

    def logits(bi, j, hh):
        return _dot(sb_ref[0, bi, :, packed(SB_Q, head_lanes[hh])],
                    kt_ref[0, j, head_lanes[hh], :])

    def add_weighted_values(slot, j):
        for hh in range(heads):
            acc_ref[hh] += _dot(w_ref[slot, hh], sb_ref[0, j, :, packed(SB_V, head_lanes[hh])])

    def finalize(bi):
        o = jnp.concatenate([acc_ref[hh] for hh in range(heads)], axis=1)
        gate = sb_ref[0, bi, :, SB_GATE * GROUP_W:(SB_GATE + 1) * GROUP_W].astype(F32)
        o_ref[0, pl.ds(pl.multiple_of(bi * tq, tq), tq), :] = (o * gate).astype(o_ref.dtype)

    def step(cur, masked, next_bi, next_j):
        mask = causal if masked else None
        zs = [z_ref[cur, hh] for hh in range(heads)]
        incls = _sb_suffix_sums(zs, suffix, mask)
        for hh in range(heads):
            z_ref[1 - cur, hh] = logits(next_bi, next_j, hh)
            run = jnp.zeros((tq, 1), F32) if masked else run_ref[hh]
            w_ref[1 - cur, hh] = _sb_weight(zs[hh], incls[hh], run, mask)
            run_ref[hh] = run + incls[hh][:, 0:1]

    def refill(slot, bi, j):
        for hh in range(heads):
            z_ref[slot, hh] = logits(bi, j, hh)

    def per_slot(slot, cond, fn):
        for parity in range(2):
            pl.when(jnp.logical_and(cond, slot == parity))(functools.partial(fn, parity))

    @pl.when(i == 0)
    def _():
        for hh in range(heads):
            z_ref[0, hh] = logits(0, 0, hh)
        w_ref[0] = jnp.zeros_like(w_ref[0])
        acc_ref[...] = jnp.zeros_like(acc_ref)
        state_ref[0] = 0
        state_ref[1] = 0

    first_slot = state_ref[0]
    pending_tile = state_ref[1]
    nxt = jnp.minimum(i + 1, nq - 1)

    def enter_block(parity):
        add_weighted_values(parity, pending_tile)
        finalize(jnp.maximum(i - 1, 0))
        acc_ref[...] = jnp.zeros_like(acc_ref)

    def first_block(parity):
        enter_block(parity)
        step(parity, True, nxt, nxt)

    def later_block(parity):
        enter_block(parity)
        step(parity, True, i, i - 1)
        add_weighted_values(1 - parity, i)
        step(1 - parity, False, nxt, nxt)

    per_slot(first_slot, i == 0, first_block)
    per_slot(first_slot, i > 0, later_block)
    tiles = jnp.where(i > 0, 2, 1)
    slot = (first_slot + tiles) % 2
    last_tile = jnp.maximum(i - 1, 0)

    def weight_left():
        lowest = run_ref[0]
        for hh in range(1, heads):
            lowest = jnp.minimum(lowest, run_ref[hh])
        return jnp.min(lowest) < NEGLIGIBLE_LOG2

    go_on = jnp.logical_and(last_tile > 0, weight_left())
    per_slot(slot, go_on, lambda parity: refill(parity, i, last_tile - 1))

    def sweep(carry):
        n, _ = carry
        j = last_tile - 1 - n

        def tile(parity):
            add_weighted_values(parity, j + 1)
            same_block = j > 0
            step(parity, False, jnp.where(same_block, i, nxt), jnp.where(same_block, j - 1, nxt))

        per_slot((slot + n) % 2, True, tile)
        return n + 1, jnp.logical_and(j > 0, weight_left())

    swept, _ = lax.while_loop(lambda carry: carry[1], sweep, (jnp.int32(0), go_on))
    slot = (slot + swept) % 2
    last_tile = last_tile - swept
    per_slot(slot, jnp.logical_and(swept > 0, last_tile > 0),
             lambda parity: refill(parity, nxt, nxt))
    state_ref[0] = slot
    state_ref[1] = last_tile

    @pl.when(i == nq - 1)
    def _():
        per_slot(slot, True, lambda parity: add_weighted_values(parity, last_tile))
        finalize(i)


def _sb_prompt(sb, kt):
    B, T, _ = sb.shape
    tq = KEY_TILE
    nkt = T // KEY_TILE
    sb4 = sb.reshape(B, nkt, KEY_TILE, 3 * GROUP_W)
    row_spec = pl.BlockSpec((1, T, GROUP_W), lambda b, i: (b, 0, 0))
    tile_spec = lambda shape: pl.BlockSpec((1,) + shape, lambda b, i: (b, 0, 0, 0))
    return pl.pallas_call(
        functools.partial(_sb_prompt_kernel, heads=SB_HEADS),
        grid=(B, T // tq),
        in_specs=[tile_spec((nkt, KEY_TILE, 3 * GROUP_W)), tile_spec((nkt, GROUP_W, KEY_TILE))],
        out_specs=row_spec,
        out_shape=jax.ShapeDtypeStruct((B, T, GROUP_W), BF16),
        scratch_shapes=[pltpu.VMEM((2, SB_HEADS, tq, KEY_TILE), F32),
                        pltpu.VMEM((2, SB_HEADS, tq, KEY_TILE), BF16),
                        pltpu.VMEM((SB_HEADS, tq, 1), F32),
                        pltpu.VMEM((SB_HEADS, tq, SB_HEAD_DIM), F32),
                        pltpu.SMEM((2,), jnp.int32)],
        compiler_params=pltpu.CompilerParams(
            dimension_semantics=("arbitrary", "arbitrary"), vmem_limit_bytes=VMEM_LIMIT),
    )(sb4, kt)


def _sb_decode_kernel(sb_ref, kn_ref, ck_ref, cv_ref, o_ref):
    tq = sb_ref.shape[1]
    past = ck_ref.shape[4]
    suffix = _suffix_matrix(KEY_TILE)
    suffix_new = _suffix_matrix(tq)
    r = lax.broadcasted_iota(jnp.int32, (tq, tq), 0)
    c = lax.broadcasted_iota(jnp.int32, (tq, tq), 1)
    causal = c < r
    heads = range(SB_HEADS)
    head_lanes = [slice(hh * SB_HEAD_DIM, (hh + 1) * SB_HEAD_DIM) for hh in heads]
    group = lambda slot: sb_ref[0, :, slot * GROUP_W:(slot + 1) * GROUP_W]
    q_all, v_all = group(SB_Q), group(SB_V)
    qs = [q_all[:, lanes] for lanes in head_lanes]
    cache_keys = [slice(j * KEY_TILE, (j + 1) * KEY_TILE) for j in reversed(range(past // KEY_TILE))]
    zs = [[_dot_nt(q, kn_ref[0, :, lanes]) for q, lanes in zip(qs, head_lanes)]]
    zs += [[_dot(q, ck_ref[0, 0, hh, :, keys].astype(BF16)) for q, hh in zip(qs, heads)]
           for keys in cache_keys]
    masks = [causal] + [None] * len(cache_keys)
    incls = [_sb_suffix_sums(z, suffix_new if mask is not None else suffix, mask)
             for z, mask in zip(zs, masks)]
    runs = [jnp.zeros((tq, 1), F32)] * SB_HEADS
    ws = []
    for z, incl, mask in zip(zs, incls, masks):
        ws.append([_sb_weight(z[hh], incl[hh], runs[hh], mask) for hh in heads])
        runs = [runs[hh] + incl[hh][:, 0:1] for hh in heads]
    accs = [_dot(w, v_all[:, lanes]) for w, lanes in zip(ws[0], head_lanes)]
    for w_tile, keys in zip(ws[1:], cache_keys):
        accs = [acc + _dot_nt(w, cv_ref[0, 0, hh, :, keys].astype(BF16))
                for w, hh, acc in zip(w_tile, heads, accs)]
    o = jnp.concatenate(accs, axis=1) * group(SB_GATE).astype(F32)
    o_ref[0] = o.astype(o_ref.dtype)


def _sb_decode(sb, kn, cache_k, cache_v, layer):
    B, T, _ = kn.shape
    past = cache_k.shape[3]
    assert past % KEY_TILE == 0
    row_spec = lambda width: pl.BlockSpec((1, T, width), lambda b: (b, 0, 0))
    cache_spec = pl.BlockSpec((1, 1, SB_HEADS, SB_HEAD_DIM, past), lambda b: (layer, b, 0, 0, 0))
    cache_k = jnp.swapaxes(cache_k, 3, 4)
    cache_v = jnp.swapaxes(cache_v, 3, 4)
    return pl.pallas_call(
        _sb_decode_kernel,
        grid=(B,),
        in_specs=[row_spec(3 * GROUP_W), row_spec(GROUP_W), cache_spec, cache_spec],
        out_specs=row_spec(GROUP_W),
        out_shape=jax.ShapeDtypeStruct((B, T, GROUP_W), BF16),
        compiler_params=pltpu.CompilerParams(
            dimension_semantics=("arbitrary",), vmem_limit_bytes=VMEM_LIMIT),
    )(sb, kn, cache_k, cache_v)


def _hgrn_tables(C):
    nl = int(math.log2(C))
    assert 1 << nl == C
    n_small = min(nl, int(math.log2(SUBLANES)))
    t = np.arange(C)[:, None]
    j = np.arange(C)[None, :]
    blocks = []
    for l in range(n_small):
        half = 1 << l
        mid = (t >> (l + 1) << (l + 1)) + half
        right = (t & half) != 0
        blocks.append(np.where(right, (j >= mid) & (j <= t), (j > t) & (j < mid)))
    blocks.append(j <= t)
    prefix = np.concatenate(blocks, axis=0).astype(np.float32)
    x = t ^ j
    msb = np.floor(np.log2(np.maximum(x, 1))).astype(np.int32)
    level_of = np.where(t > j, msb, np.where(t == j, nl, -1)).astype(np.int32)
    return jnp.asarray(prefix, BF16), jnp.asarray(level_of), nl, n_small


def _hgrn_out_kernel(*refs, C, sub, nl, n_small, layer, has_s0, alpha):
    (hg_ref, fh_ref, pm_ref, lv_ref, lbl_ref, ng_ref, ma_ref, x_ref, wo_ref,
     lg_ref, lnb_ref) = refs[:11]
    if has_s0:
        s0_ref, y_ref, so_ref, st_ref, b_ref = refs[11:]
    else:
        y_ref, so_ref, st_ref, b_ref = refs[11:]
    t = pl.program_id(1)

    def packed(rows, slot):
        return hg_ref[0, rows, slot * GROUP_W:(slot + 1) * GROUP_W]

    group = min(sub, OUT_GROUP_CHUNKS)
    group_rows = [slice(g * group * C, (g + 1) * group * C) for g in range(sub // group)]
    w_attn = wo_ref[:GROUP_W, :].astype(BF16)
    w_hgrn = wo_ref[GROUP_W:, :].astype(BF16)
    out_attn = [_dot(ma_ref[0, rows, :], w_attn) for rows in group_rows]

    @pl.when(t == 0)
    def _():
        if has_s0:
            for h in range(HG_HEADS):
                st_ref[h] = s0_ref[0, 0, h].T
        else:
            st_ref[...] = jnp.zeros_like(st_ref)

    logits = lbl_ref[...]
    e = jnp.exp(logits - jnp.max(logits, axis=0, keepdims=True))
    lb = jnp.sum(e[:layer + 1], axis=0, keepdims=True) / jnp.sum(e, axis=0, keepdims=True)

    def head(a, h):
        return a[:, h * HG_HEAD_DIM:(h + 1) * HG_HEAD_DIM]

    def prepare(c):
        rows = slice(c * C, (c + 1) * C)
        f = lb + (1.0 - lb) * _sigmoid(fh_ref[0, rows, :])
        g = jnp.log2(f)
        kin = 1.0 - f
        qh = packed(rows, HG_Q).astype(F32)
        qs = qh * _sigmoid(qh)
        g_hi, g_lo = _split_bf16(g, 2)
        pre = _dot(pm_ref[:n_small * C, :], g_hi)
        cum = _dot(pm_ref[n_small * C:, :], jnp.concatenate([g_hi, g_lo], axis=1))
        b_ref[c] = cum[:, :GROUP_W] + cum[:, GROUP_W:]
        return dict(rows=rows, qs=qs, kin=kin, pre=pre, qs_b=qs.astype(BF16),
                    kin_b=kin.astype(BF16))

    def level_log(c, pre, l):
        if l < n_small:
            return pre[l * C:(l + 1) * C]
        half = 1 << l
        parts = []
        for lo in range(0, C, 2 * half):
            mid = lo + half
            last_left = b_ref[c, mid - 1:mid, :]
            parts.append(last_left - b_ref[c, lo:mid, :])
            parts.append(b_ref[c, mid:mid + half, :] - last_left)
        return jnp.concatenate(parts, axis=0)

    def operands(c, p, l):
        if l == nl:
            return p["qs_b"], p["kin_b"]
        e_l = jnp.exp2(level_log(c, p["pre"], l)).astype(BF16)
        return p["qs_b"] * e_l, p["kin_b"] * e_l

    def finish(c, p, scores):
        rows = p["rows"]
        b = b_ref[c]
        b_last = b_ref[c, C - 1:C, :]
        q_in = (p["qs"] * jnp.exp2(b)).astype(BF16)
        k_out = (p["kin"] * jnp.exp2(b_last - b)).astype(BF16)
        chunk_decay = jnp.exp2(b_last)
        i_b = packed(rows, HG_I)
        gh = packed(rows, HG_GATE).astype(F32)
        gate = ng_ref[...] * (gh * _sigmoid(gh))
        outs = []
        for h in range(HG_HEADS):
            st = st_ref[h]
            o = (_dot(scores[h].astype(BF16), head(i_b, h))
                 + _dot_nt(head(q_in, h), st.astype(BF16)))
            st_ref[h] = st * head(chunk_decay, h) + _dot_tn(head(i_b, h), head(k_out, h))
            outs.append(o * lax.rsqrt(jnp.mean(o * o, axis=-1, keepdims=True) + RMS_EPS))
        return (jnp.concatenate(outs, axis=1) * gate).astype(BF16)

    def project_out(g, mixed_hgrn):
        rows = group_rows[g]
        hid = alpha * x_ref[0, rows, :] + (out_attn[g] + _dot(mixed_hgrn, w_hgrn))
        mu = jnp.mean(hid, axis=-1, keepdims=True)
        cen = hid - mu
        var = jnp.mean(cen * cen, axis=-1, keepdims=True)
        y_ref[0, rows, :] = cen * lax.rsqrt(var + LN_EPS) * lg_ref[...] + lnb_ref[...]

    chunks = range(sub)
    preps = [prepare(c) for c in chunks]
    level_of = lv_ref[...]
    order = [(l, c) for l in [nl] + list(range(nl)) for c in chunks]
    scores = [[0.0] * HG_HEADS for _ in chunks]
    nxt = operands(order[0][1], preps[order[0][1]], order[0][0])
    for idx, (l, c) in enumerate(order):
        q_l, k_l = nxt
        if idx + 1 < len(order):
            l_n, c_n = order[idx + 1]
            nxt = operands(c_n, preps[c_n], l_n)
        hit = level_of == l
        for h in range(HG_HEADS):
            scores[c][h] = jnp.where(hit, _dot_nt(head(q_l, h), head(k_l, h)), scores[c][h])
    mixed = []
    for c in chunks:
        mixed.append(finish(c, preps[c], scores[c]))
        if len(mixed) == group:
            project_out(c // group, mixed[0] if group == 1 else jnp.concatenate(mixed, axis=0))
            mixed = []

    @pl.when(t == pl.num_programs(1) - 1)
    def _():
        for h in range(HG_HEADS):
            so_ref[0, 0, h] = st_ref[h].T


def _hgrn_out(hg, fh, lb_logits, norm_g, s0, layer, ma, x, w_out, ln_g, ln_b, *, C, sub, alpha):
    B, T, _ = fh.shape
    prefix, level_of, nl, n_small = _hgrn_tables(C)
    has_s0 = s0 is not None
    rows = sub * C
    row_spec = pl.BlockSpec((1, rows, GROUP_W), lambda b, t: (b, t, 0))
    row3_spec = pl.BlockSpec((1, rows, 3 * GROUP_W), lambda b, t: (b, t, 0))
    wide_spec = pl.BlockSpec((1, rows, D_MODEL), lambda b, t: (b, t, 0))
    const = lambda shape: pl.BlockSpec(shape, lambda b, t: (0,) * len(shape))
    state_shape = (1, 1, HG_HEADS, HG_HEAD_DIM, HG_HEAD_DIM)
    in_specs = [row3_spec, row_spec, const(prefix.shape), const(level_of.shape),
                const(lb_logits.shape), const((1, GROUP_W)), row_spec, wide_spec,
                const(w_out.shape), const((1, D_MODEL)), const((1, D_MODEL))]
    args = [hg, fh, prefix, level_of, lb_logits.astype(F32),
            norm_g.reshape(1, GROUP_W).astype(F32), ma, x, w_out,
            ln_g.reshape(1, D_MODEL).astype(F32), ln_b.reshape(1, D_MODEL).astype(F32)]
    if has_s0:
        in_specs.append(pl.BlockSpec(state_shape, lambda b, t: (layer, b, 0, 0, 0)))
        args.append(s0)
    return pl.pallas_call(
        functools.partial(_hgrn_out_kernel, C=C, sub=sub, nl=nl, n_small=n_small, layer=layer,
                          has_s0=has_s0, alpha=alpha),
        grid=(B, T // rows),
        in_specs=in_specs,
        out_specs=[wide_spec, pl.BlockSpec(state_shape, lambda b, t: (0, b, 0, 0, 0))],
        out_shape=[jax.ShapeDtypeStruct((B, T, D_MODEL), F32),
                   jax.ShapeDtypeStruct((1, B, HG_HEADS, HG_HEAD_DIM, HG_HEAD_DIM), F32)],
        scratch_shapes=[pltpu.VMEM((HG_HEADS, HG_HEAD_DIM, HG_HEAD_DIM), F32),
                        pltpu.VMEM((sub, C, GROUP_W), F32)],
        compiler_params=pltpu.CompilerParams(
            dimension_semantics=("arbitrary", "arbitrary"), vmem_limit_bytes=VMEM_LIMIT),
    )(*args)


PROMPT_ROWS = 512
PROMPT_CHUNK = 128
OUT_GROUP_CHUNKS = 2
PROMPT_CHUNKS_PER_STEP = 8


def _layer(x, cache_k, cache_v, state_s, layer, w_in, w_out, lb_logits, norm_g, ln_g, ln_b, alpha):
    B, T, _ = x.shape
    decode = cache_k is not None
    if decode:
        sb, k, hg, fh, k_out, v_out = _project(x, w_in, nb=B, tm=T, k_transposed=False)
        ma = _sb_decode(sb, k, cache_k, cache_v, layer)
        y, s_out = _hgrn_out(hg, fh, lb_logits, norm_g, state_s, layer, ma, x, w_out, ln_g, ln_b,
                             C=T, sub=1, alpha=alpha)
    else:
        sb, kt, hg, fh, k_out, v_out = _project(
            x, w_in, nb=1, tm=min(PROMPT_ROWS, T), k_transposed=True)
        ma = _sb_prompt(sb, kt)
        chunk = min(PROMPT_CHUNK, T)
        y, s_out = _hgrn_out(hg, fh, lb_logits, norm_g, None, layer, ma, x, w_out, ln_g, ln_b,
                             C=chunk, sub=min(PROMPT_CHUNKS_PER_STEP, T // chunk), alpha=alpha)
    return y, k_out, v_out, s_out


def kernel(x_prompt, x_sample, cache_k, cache_v, state_s, w_in, w_out, lb_logits, hgrn_norm_g,
           ln_g, ln_b):
    depth = w_in.shape[0]
    alpha = (2 * depth) ** 0.25
    yp, ys = x_prompt, x_sample
    per_layer = []
    for l in range(depth):
        common = (l, w_in[l], w_out[l], lb_logits, hgrn_norm_g[l], ln_g[l], ln_b[l], alpha)
        yp, kp, vp, sp = _layer(yp, None, None, None, *common)
        ys, kn, vn, sn = _layer(ys, cache_k, cache_v, state_s, *common)
        per_layer.append((kp, vp, sp, kn, vn, sn))
    stack = lambda i: (per_layer[0][i] if depth == 1
                       else jnp.concatenate([p[i] for p in per_layer], axis=0))
    return (yp, ys, stack(0), stack(1), stack(2), stack(3), stack(4), stack(5))
```

```python
import functools
import math

import numpy as np
import jax
import jax.numpy as jnp
from jax import lax
from jax.experimental import pallas as pl
from jax.experimental.pallas import tpu as pltpu

F32 = jnp.float32
BF16 = jnp.bfloat16

D_MODEL = 1024
GROUP_W = 512
N_GROUPS = 8
SB_HEADS = 8
SB_HEAD_DIM = 64
HG_HEADS = 4
HG_HEAD_DIM = 128
LN_EPS = 1e-5
RMS_EPS = 1e-6

SUBLANES = 8
KEY_TILE = 256
SB_Q, SB_V, SB_GATE = 0, 1, 2
HG_Q, HG_I, HG_GATE = 0, 1, 2
BOUNDED_DECAY_LOG2 = 100.0
NEGLIGIBLE_LOG2 = 200.0
LOG2E = 1.4426950408889634
VMEM_LIMIT = 56 * 1024 * 1024


def _sigmoid(x):
    return 1.0 / (1.0 + jnp.exp2(x * (-LOG2E)))


def _dot(a, b):
    return jnp.dot(a, b, preferred_element_type=F32)


def _dot_nt(a, b):
    return lax.dot_general(a, b, (((1,), (1,)), ((), ())), preferred_element_type=F32)


def _dot_tn(a, b):
    return lax.dot_general(a, b, (((0,), (0,)), ((), ())), preferred_element_type=F32)


def _split_bf16(x, parts):
    out = []
    for _ in range(parts - 1):
        p = x.astype(BF16)
        out.append(p)
        x = x - p.astype(F32)
    out.append(x.astype(BF16))
    return out


def _proj_kernel(x_ref, w_ref, sb_ref, k_ref, hg_ref, fh_ref, ko_ref, vo_ref, *, nb, tm,
                 k_transposed):
    m = nb * tm
    x = x_ref[...].reshape(m, D_MODEL).astype(BF16)

    def col(c):
        return _dot(x, w_ref[:, c * GROUP_W:(c + 1) * GROUP_W].astype(BF16))

    def put(ref, val, slot=0):
        ref[:, :, slot * GROUP_W:(slot + 1) * GROUP_W] = (
            val.reshape(nb, tm, GROUP_W).astype(ref.dtype))

    def put_heads(ref, val):
        if k_transposed:
            val_t = val.T
            for h in range(SB_HEADS):
                ref[0, 0, h] = val_t[h * SB_HEAD_DIM:(h + 1) * SB_HEAD_DIM, :]
            return val_t
        for b in range(nb):
            for h in range(SB_HEADS):
                ref[0, b, h] = val[b * tm:(b + 1) * tm, h * SB_HEAD_DIM:(h + 1) * SB_HEAD_DIM]

    put(sb_ref, col(0) * (SB_HEAD_DIM ** -0.5 * LOG2E), SB_Q)
    k = col(1)
    kt = put_heads(ko_ref, k)
    if k_transposed:
        kt = kt.astype(BF16)
        for j in range(m // KEY_TILE):
            k_ref[0, j] = kt[:, j * KEY_TILE:(j + 1) * KEY_TILE]
    else:
        put(k_ref, k)
    v = col(2)
    put_heads(vo_ref, v)
    put(sb_ref, v, SB_V)
    ga = col(3)
    put(sb_ref, ga * _sigmoid(ga), SB_GATE)
    put(hg_ref, col(4), HG_Q)
    put(fh_ref, col(5))
    put(hg_ref, col(6), HG_I)
    put(hg_ref, col(7), HG_GATE)


def _project(x, w, *, nb, tm, k_transposed):
    B, T, _ = x.shape
    grid = (B // nb, T // tm)
    act = lambda dt: jax.ShapeDtypeStruct((B, T, GROUP_W), dt)
    act_spec = pl.BlockSpec((nb, tm, GROUP_W), lambda b, t: (b, t, 0))
    act3 = jax.ShapeDtypeStruct((B, T, 3 * GROUP_W), BF16)
    act3_spec = pl.BlockSpec((nb, tm, 3 * GROUP_W), lambda b, t: (b, t, 0))
    if k_transposed:
        assert nb == 1 and tm % KEY_TILE == 0
        k_shape = jax.ShapeDtypeStruct((B, T // KEY_TILE, GROUP_W, KEY_TILE), BF16)
        k_spec = pl.BlockSpec((1, tm // KEY_TILE, GROUP_W, KEY_TILE), lambda b, t: (b, t, 0, 0))
        kv_shape = jax.ShapeDtypeStruct((1, B, SB_HEADS, SB_HEAD_DIM, T), F32)
        kv_spec = pl.BlockSpec((1, 1, SB_HEADS, SB_HEAD_DIM, tm), lambda b, t: (0, b, 0, 0, t))
    else:
        k_shape, k_spec = act(BF16), act_spec
        kv_shape = jax.ShapeDtypeStruct((1, B, SB_HEADS, T, SB_HEAD_DIM), F32)
        kv_spec = pl.BlockSpec((1, nb, SB_HEADS, tm, SB_HEAD_DIM), lambda b, t: (0, b, 0, t, 0))
    outs = pl.pallas_call(
        functools.partial(_proj_kernel, nb=nb, tm=tm, k_transposed=k_transposed),
        grid=grid,
        in_specs=[pl.BlockSpec((nb, tm, D_MODEL), lambda b, t: (b, t, 0)),
                  pl.BlockSpec((D_MODEL, N_GROUPS * GROUP_W), lambda b, t: (0, 0),
                               pipeline_mode=pl.Buffered(1))],
        out_specs=[act3_spec, k_spec, act3_spec, act_spec, kv_spec, kv_spec],
        out_shape=[act3, k_shape, act3, act(F32), kv_shape, kv_shape],
        compiler_params=pltpu.CompilerParams(
            dimension_semantics=("arbitrary", "arbitrary"), vmem_limit_bytes=VMEM_LIMIT),
    )(x, w)
    if k_transposed:
        outs = list(outs[:4]) + [jnp.swapaxes(o, 3, 4) for o in outs[4:]]
    return outs


def _suffix_matrix(n):
    r = lax.broadcasted_iota(jnp.int32, (n, n), 0)
    c = lax.broadcasted_iota(jnp.int32, (n, n), 1)
    return jnp.where(r >= c, 1.0, 0.0).astype(BF16)


def _sb_suffix_sums(zs, suffix, causal):
    drops = []
    for z in zs:
        drop = jnp.maximum(z, 0.0) + jnp.log2(1.0 + jnp.exp2(-jnp.abs(z)))
        if causal is not None:
            drop = jnp.where(causal, drop, 0.0)
        drops.append(drop.astype(BF16))
    return [_dot(drop, suffix) for drop in drops]


def _sb_weight(z, incl, run, causal):
    w = jnp.exp2(z - incl - run)
    if causal is not None:
        w = jnp.where(causal, w, 0.0)
    return w.astype(BF16)


def _sb_prompt_kernel(sb_ref, kt_ref, o_ref, z_ref, w_ref, run_ref, acc_ref, state_ref, *, heads):
    tq = KEY_TILE
    nq = sb_ref.shape[1]
    i = pl.program_id(1)
    suffix = _suffix_matrix(KEY_TILE)
    r = lax.broadcasted_iota(jnp.int32, (tq, KEY_TILE), 0)
    c = lax.broadcasted_iota(jnp.int32, (tq, KEY_TILE), 1)
    causal = c < r
    head_lanes = [slice(hh * SB_HEAD_DIM, (hh + 1) * SB_HEAD_DIM) for hh in range(heads)]

    def packed(slot, lanes):
        return slice(slot * GROUP_W + lanes.start, slot * GROUP_W + lanes.stop)

    def logits(bi, j, hh):
        return _dot(sb_ref[0, bi, :, packed(SB_Q, head_lanes[hh])],
                    kt_ref[0, j, head_lanes[hh], :])

    def add_weighted_values(slot, j):
        for hh in range(heads):
            acc_ref[hh] += _dot(w_ref[slot, hh], sb_ref[0, j, :, packed(SB_V, head_lanes[hh])])

    def finalize(bi):
        o = jnp.concatenate([acc_ref[hh] for hh in range(heads)], axis=1)
        gate = sb_ref[0, bi, :, SB_GATE * GROUP_W:(SB_GATE + 1) * GROUP_W].astype(F32)
        o_ref[0, pl.ds(pl.multiple_of(bi * tq, tq), tq), :] = (o * gate).astype(o_ref.dtype)

    def step(cur, masked, next_bi, next_j):
        mask = causal if masked else None
        zs = [z_ref[cur, hh] for hh in range(heads)]
        incls = _sb_suffix_sums(zs, suffix, mask)
        for hh in range(heads):
            z_ref[1 - cur, hh] = logits(next_bi, next_j, hh)
            run = jnp.zeros((tq, 1), F32) if masked else run_ref[hh]
            w_ref[1 - cur, hh] = _sb_weight(zs[hh], incls[hh], run, mask)
            run_ref[hh] = run + incls[hh][:, 0:1]

    def refill(slot, bi, j):
        for hh in range(heads):
            z_ref[slot, hh] = logits(bi, j, hh)

    def per_slot(slot, cond, fn):
        for parity in range(2):
            pl.when(jnp.logical_and(cond, slot == parity))(functools.partial(fn, parity))

    @pl.when(i == 0)
    def _():
        for hh in range(heads):
            z_ref[0, hh] = logits(0, 0, hh)
        w_ref[0] = jnp.zeros_like(w_ref[0])
        acc_ref[...] = jnp.zeros_like(acc_ref)
        state_ref[0] = 0
        state_ref[1] = 0

    first_slot = state_ref[0]
    pending_tile = state_ref[1]
    nxt = jnp.minimum(i + 1, nq - 1)

    def enter_block(parity):
        add_weighted_values(parity, pending_tile)
        finalize(jnp.maximum(i - 1, 0))
        acc_ref[...] = jnp.zeros_like(acc_ref)

    def first_block(parity):
        enter_block(parity)
        step(parity, True, nxt, nxt)

    def later_block(parity):
        enter_block(parity)
        step(parity, True, i, i - 1)
        add_weighted_values(1 - parity, i)
        step(1 - parity, False, nxt, nxt)

    per_slot(first_slot, i == 0, first_block)
    per_slot(first_slot, i > 0, later_block)
    tiles = jnp.where(i > 0, 2, 1)
    slot = (first_slot + tiles) % 2
    last_tile = jnp.maximum(i - 1, 0)

    def weight_left():
        lowest = run_ref[0]
        for hh in range(1, heads):
            lowest = jnp.minimum(lowest, run_ref[hh])
        return jnp.min(lowest) < NEGLIGIBLE_LOG2

    go_on = jnp.logical_and(last_tile > 0, weight_left())
    per_slot(slot, go_on, lambda parity: refill(parity, i, last_tile - 1))

    def sweep(carry):
        n, _ = carry
        j = last_tile - 1 - n

        def tile(parity):
            add_weighted_values(parity, j + 1)
            same_block = j > 0
            step(parity, False, jnp.where(same_block, i, nxt), jnp.where(same_block, j - 1, nxt))

        per_slot((slot + n) % 2, True, tile)
        return n + 1, jnp.logical_and(j > 0, weight_left())

    swept, _ = lax.while_loop(lambda carry: carry[1], sweep, (jnp.int32(0), go_on))
    slot = (slot + swept) % 2
    last_tile = last_tile - swept
    per_slot(slot, jnp.logical_and(swept > 0, last_tile > 0),
             lambda parity: refill(parity, nxt, nxt))
    state_ref[0] = slot
    state_ref[1] = last_tile

    @pl.when(i == nq - 1)
    def _():
        per_slot(slot, True, lambda parity: add_weighted_values(parity, last_tile))
        finalize(i)


def _sb_prompt(sb, kt):
    B, T, _ = sb.shape
    tq = KEY_TILE
    nkt = T // KEY_TILE
    sb4 = sb.reshape(B, nkt, KEY_TILE, 3 * GROUP_W)
    row_spec = pl.BlockSpec((1, T, GROUP_W), lambda b, i: (b, 0, 0))
    tile_spec = lambda shape: pl.BlockSpec((1,) + shape, lambda b, i: (b, 0, 0, 0))
    return pl.pallas_call(
        functools.partial(_sb_prompt_kernel, heads=SB_HEADS),
        grid=(B, T // tq),
        in_specs=[tile_spec((nkt, KEY_TILE, 3 * GROUP_W)), tile_spec((nkt, GROUP_W, KEY_TILE))],
        out_specs=row_spec,
        out_shape=jax.ShapeDtypeStruct((B, T, GROUP_W), BF16),
        scratch_shapes=[pltpu.VMEM((2, SB_HEADS, tq, KEY_TILE), F32),
                        pltpu.VMEM((2, SB_HEADS, tq, KEY_TILE), BF16),
                        pltpu.VMEM((SB_HEADS, tq, 1), F32),
                        pltpu.VMEM((SB_HEADS, tq, SB_HEAD_DIM), F32),
                        pltpu.SMEM((2,), jnp.int32)],
        compiler_params=pltpu.CompilerParams(
            dimension_semantics=("arbitrary", "arbitrary"), vmem_limit_bytes=VMEM_LIMIT),
    )(sb4, kt)


def _sb_decode_kernel(sb_ref, kn_ref, ck_ref, cv_ref, o_ref):
    tq = sb_ref.shape[1]
    past = ck_ref.shape[4]
    suffix = _suffix_matrix(KEY_TILE)
    suffix_new = _suffix_matrix(tq)
    r = lax.broadcasted_iota(jnp.int32, (tq, tq), 0)
    c = lax.broadcasted_iota(jnp.int32, (tq, tq), 1)
    causal = c < r
    heads = range(SB_HEADS)
    head_lanes = [slice(hh * SB_HEAD_DIM, (hh + 1) * SB_HEAD_DIM) for hh in heads]
    group = lambda slot: sb_ref[0, :, slot * GROUP_W:(slot + 1) * GROUP_W]
    q_all, v_all = group(SB_Q), group(SB_V)
    qs = [q_all[:, lanes] for lanes in head_lanes]
    cache_keys = [slice(j * KEY_TILE, (j + 1) * KEY_TILE) for j in reversed(range(past // KEY_TILE))]
    zs = [[_dot_nt(q, kn_ref[0, :, lanes]) for q, lanes in zip(qs, head_lanes)]]
    zs += [[_dot(q, ck_ref[0, 0, hh, :, keys].astype(BF16)) for q, hh in zip(qs, heads)]
           for keys in cache_keys]
    masks = [causal] + [None] * len(cache_keys)
    incls = [_sb_suffix_sums(z, suffix_new if mask is not None else suffix, mask)
             for z, mask in zip(zs, masks)]
    runs = [jnp.zeros((tq, 1), F32)] * SB_HEADS
    ws = []
    for z, incl, mask in zip(zs, incls, masks):
        ws.append([_sb_weight(z[hh], incl[hh], runs[hh], mask) for hh in heads])
        runs = [runs[hh] + incl[hh][:, 0:1] for hh in heads]
    accs = [_dot(w, v_all[:, lanes]) for w, lanes in zip(ws[0], head_lanes)]
    for w_tile, keys in zip(ws[1:], cache_keys):
        accs = [acc + _dot_nt(w, cv_ref[0, 0, hh, :, keys].astype(BF16))
                for w, hh, acc in zip(w_tile, heads, accs)]
    o = jnp.concatenate(accs, axis=1) * group(SB_GATE).astype(F32)
    o_ref[0] = o.astype(o_ref.dtype)


def _sb_decode(sb, kn, cache_k, cache_v, layer):
    B, T, _ = kn.shape
    past = cache_k.shape[3]
    assert past % KEY_TILE == 0
    row_spec = lambda width: pl.BlockSpec((1, T, width), lambda b: (b, 0, 0))
    cache_spec = pl.BlockSpec((1, 1, SB_HEADS, SB_HEAD_DIM, past), lambda b: (layer, b, 0, 0, 0))
    cache_k = jnp.swapaxes(cache_k, 3, 4)
    cache_v = jnp.swapaxes(cache_v, 3, 4)
    return pl.pallas_call(
        _sb_decode_kernel,
        grid=(B,),
        in_specs=[row_spec(3 * GROUP_W), row_spec(GROUP_W), cache_spec, cache_spec],
        out_specs=row_spec(GROUP_W),
        out_shape=jax.ShapeDtypeStruct((B, T, GROUP_W), BF16),
        compiler_params=pltpu.CompilerParams(
            dimension_semantics=("arbitrary",), vmem_limit_bytes=VMEM_LIMIT),
    )(sb, kn, cache_k, cache_v)


def _hgrn_tables(C):
    nl = int(math.log2(C))
    assert 1 << nl == C
    n_small = min(nl, int(math.log2(SUBLANES)))
    t = np.arange(C)[:, None]
    j = np.arange(C)[None, :]
    blocks = []
    for l in range(n_small):
        half = 1 << l
        mid = (t >> (l + 1) << (l + 1)) + half
        right = (t & half) != 0
        blocks.append(np.where(right, (j >= mid) & (j <= t), (j > t) & (j < mid)))
    blocks.append(j <= t)
    prefix = np.concatenate(blocks, axis=0).astype(np.float32)
    x = t ^ j
    msb = np.floor(np.log2(np.maximum(x, 1))).astype(np.int32)
    level_of = np.where(t > j, msb, np.where(t == j, nl, -1)).astype(np.int32)
    return jnp.asarray(prefix, BF16), jnp.asarray(level_of), nl, n_small


def _hgrn_out_kernel(*refs, C, sub, nl, n_small, layer, has_s0, alpha):
    (hg_ref, fh_ref, pm_ref, lv_ref, lbl_ref, ng_ref, ma_ref, x_ref, wo_ref,
     lg_ref, lnb_ref) = refs[:11]
    if has_s0:
        s0_ref, y_ref, so_ref, st_ref, b_ref, sc_ref = refs[11:]
    else:
        y_ref, so_ref, st_ref, b_ref, sc_ref = refs[11:]
    t = pl.program_id(1)

    def packed(rows, slot):
        return hg_ref[0, rows, slot * GROUP_W:(slot + 1) * GROUP_W]

    group = min(sub, OUT_GROUP_CHUNKS)
    group_rows = [slice(g * group * C, (g + 1) * group * C) for g in range(sub // group)]
    w_attn = wo_ref[:GROUP_W, :].astype(BF16)
    w_hgrn = wo_ref[GROUP_W:, :].astype(BF16)
    out_attn = [_dot(ma_ref[0, rows, :], w_attn) for rows in group_rows]

    @pl.when(t == 0)
    def _():
        if has_s0:
            for h in range(HG_HEADS):
                st_ref[h] = s0_ref[0, 0, h].T
        else:
            st_ref[...] = jnp.zeros_like(st_ref)

    logits = lbl_ref[...]
    e = jnp.exp(logits - jnp.max(logits, axis=0, keepdims=True))
    lb = jnp.sum(e[:layer + 1], axis=0, keepdims=True) / jnp.sum(e, axis=0, keepdims=True)

    def head(a, h):
        return a[:, h * HG_HEAD_DIM:(h + 1) * HG_HEAD_DIM]

    def prepare(c):
        rows = slice(c * C, (c + 1) * C)
        f = lb + (1.0 - lb) * _sigmoid(fh_ref[0, rows, :])
        g = jnp.log2(f)
        kin = 1.0 - f
        qh = packed(rows, HG_Q).astype(F32)
        qs = qh * _sigmoid(qh)
        g_hi, g_lo = _split_bf16(g, 2)
        cum = _dot(pm_ref[n_small * C:, :], jnp.concatenate([g_hi, g_lo], axis=1))
        b_ref[c] = cum[:, :GROUP_W] + cum[:, GROUP_W:]
        return dict(rows=rows, qs=qs, kin=kin, g_hi=g_hi)

    def level_log(c, pre, l):
        if l < n_small:
            return pre[l * C:(l + 1) * C]
        half = 1 << l
        parts = []
        for lo in range(0, C, 2 * half):
            mid = lo + half
            last_left = b_ref[c, mid - 1:mid, :]
            parts.append(last_left - b_ref[c, lo:mid, :])
            parts.append(b_ref[c, mid:mid + half, :] - last_left)
        return jnp.concatenate(parts, axis=0)

    def operands(c, p, l):
        if l == nl:
            return p["qs_b"], p["kin_b"]
        e_l = jnp.exp2(level_log(c, p["pre"], l)).astype(BF16)
        return p["qs_b"] * e_l, p["kin_b"] * e_l

    def scores_by_levels():
        ops = [dict(pre=_dot(pm_ref[:n_small * C, :], p["g_hi"]),
                    qs_b=p["qs"].astype(BF16), kin_b=p["kin"].astype(BF16)) for p in preps]
        level_of = lv_ref[...]
        order = [(l, c) for l in [nl] + list(range(nl)) for c in chunks]
        scores = [[0.0] * HG_HEADS for _ in chunks]
        nxt = operands(order[0][1], ops[order[0][1]], order[0][0])
        for idx, (l, c) in enumerate(order):
            q_l, k_l = nxt
            if idx + 1 < len(order):
                l_n, c_n = order[idx + 1]
                nxt = operands(c_n, ops[c_n], l_n)
            hit = level_of == l
            for h in range(HG_HEADS):
                scores[c][h] = jnp.where(hit, _dot_nt(head(q_l, h), head(k_l, h)), scores[c][h])
        for c in chunks:
            for h in range(HG_HEADS):
                sc_ref[c, h] = scores[c][h]

    def scores_from_chunk_start():
        on_or_below = lv_ref[...] >= 0
        for c, p in enumerate(preps):
            b = b_ref[c]
            q_dec = (p["qs"] * jnp.exp2(b)).astype(BF16)
            k_grow = (p["kin"] * jnp.exp2(-b)).astype(BF16)
            for h in range(HG_HEADS):
                sc_ref[c, h] = jnp.where(on_or_below, _dot_nt(head(q_dec, h), head(k_grow, h)), 0.0)

    def finish(c, p):
        rows = p["rows"]
        b = b_ref[c]
        b_last = b_ref[c, C - 1:C, :]
        q_in = (p["qs"] * jnp.exp2(b)).astype(BF16)
        k_out = (p["kin"] * jnp.exp2(b_last - b)).astype(BF16)
        chunk_decay = jnp.exp2(b_last)
        i_b = packed(rows, HG_I)
        gh = packed(rows, HG_GATE).astype(F32)
        gate = ng_ref[...] * (gh * _sigmoid(gh))
        outs = []
        for h in range(HG_HEADS):
            st = st_ref[h]
            o = (_dot(sc_ref[c, h].astype(BF16), head(i_b, h))
                 + _dot_nt(head(q_in, h), st.astype(BF16)))
            st_ref[h] = st * head(chunk_decay, h) + _dot_tn(head(i_b, h), head(k_out, h))
            outs.append(o * lax.rsqrt(jnp.mean(o * o, axis=-1, keepdims=True) + RMS_EPS))
        return (jnp.concatenate(outs, axis=1) * gate).astype(BF16)

    def project_out(g, mixed_hgrn):
        rows = group_rows[g]
        hid = alpha * x_ref[0, rows, :] + (out_attn[g] + _dot(mixed_hgrn, w_hgrn))
        mu = jnp.mean(hid, axis=-1, keepdims=True)
        cen = hid - mu
        var = jnp.mean(cen * cen, axis=-1, keepdims=True)
        y_ref[0, rows, :] = cen * lax.rsqrt(var + LN_EPS) * lg_ref[...] + lnb_ref[...]

    chunks = range(sub)
    preps = [prepare(c) for c in chunks]
    lowest = b_ref[0, C - 1:C, :]
    for c in range(1, sub):
        lowest = jnp.minimum(lowest, b_ref[c, C - 1:C, :])
    bounded = jnp.min(lowest) > -BOUNDED_DECAY_LOG2
    pl.when(bounded)(scores_from_chunk_start)
    pl.when(jnp.logical_not(bounded))(scores_by_levels)
    mixed = []
    for c in chunks:
        mixed.append(finish(c, preps[c]))
        if len(mixed) == group:
            project_out(c // group, mixed[0] if group == 1 else jnp.concatenate(mixed, axis=0))
            mixed = []

    @pl.when(t == pl.num_programs(1) - 1)
    def _():
        for h in range(HG_HEADS):
            so_ref[0, 0, h] = st_ref[h].T


def _hgrn_out(hg, fh, lb_logits, norm_g, s0, layer, ma, x, w_out, ln_g, ln_b, *, C, sub, alpha):
    B, T, _ = fh.shape
    prefix, level_of, nl, n_small = _hgrn_tables(C)
    has_s0 = s0 is not None
    rows = sub * C
    row_spec = pl.BlockSpec((1, rows, GROUP_W), lambda b, t: (b, t, 0))
    row3_spec = pl.BlockSpec((1, rows, 3 * GROUP_W), lambda b, t: (b, t, 0))
    wide_spec = pl.BlockSpec((1, rows, D_MODEL), lambda b, t: (b, t, 0))
    const = lambda shape: pl.BlockSpec(shape, lambda b, t: (0,) * len(shape))
    state_shape = (1, 1, HG_HEADS, HG_HEAD_DIM, HG_HEAD_DIM)
    in_specs = [row3_spec, row_spec, const(prefix.shape), const(level_of.shape),
                const(lb_logits.shape), const((1, GROUP_W)), row_spec, wide_spec,
                const(w_out.shape), const((1, D_MODEL)), const((1, D_MODEL))]
    args = [hg, fh, prefix, level_of, lb_logits.astype(F32),
            norm_g.reshape(1, GROUP_W).astype(F32), ma, x, w_out,
            ln_g.reshape(1, D_MODEL).astype(F32), ln_b.reshape(1, D_MODEL).astype(F32)]
    if has_s0:
        in_specs.append(pl.BlockSpec(state_shape, lambda b, t: (layer, b, 0, 0, 0)))
        args.append(s0)
    return pl.pallas_call(
        functools.partial(_hgrn_out_kernel, C=C, sub=sub, nl=nl, n_small=n_small, layer=layer,
                          has_s0=has_s0, alpha=alpha),
        grid=(B, T // rows),
        in_specs=in_specs,
        out_specs=[wide_spec, pl.BlockSpec(state_shape, lambda b, t: (0, b, 0, 0, 0))],
        out_shape=[jax.ShapeDtypeStruct((B, T, D_MODEL), F32),
                   jax.ShapeDtypeStruct((1, B, HG_HEADS, HG_HEAD_DIM, HG_HEAD_DIM), F32)],
        scratch_shapes=[pltpu.VMEM((HG_HEADS, HG_HEAD_DIM, HG_HEAD_DIM), F32),
                        pltpu.VMEM((sub, C, GROUP_W), F32),
                        pltpu.VMEM((sub, HG_HEADS, C, C), F32)],
        compiler_params=pltpu.CompilerParams(
            dimension_semantics=("arbitrary", "arbitrary"), vmem_limit_bytes=VMEM_LIMIT),
    )(*args)


PROMPT_ROWS = 512
PROMPT_CHUNK = 128
OUT_GROUP_CHUNKS = 2
PROMPT_CHUNKS_PER_STEP = 8


def _layer(x, cache_k, cache_v, state_s, layer, w_in, w_out, lb_logits, norm_g, ln_g, ln_b, alpha):
    B, T, _ = x.shape
    decode = cache_k is not None
    if decode:
        sb, k, hg, fh, k_out, v_out = _project(x, w_in, nb=B, tm=T, k_transposed=False)
        ma = _sb_decode(sb, k, cache_k, cache_v, layer)
        y, s_out = _hgrn_out(hg, fh, lb_logits, norm_g, state_s, layer, ma, x, w_out, ln_g, ln_b,
                             C=T, sub=1, alpha=alpha)
    else:
        sb, kt, hg, fh, k_out, v_out = _project(
            x, w_in, nb=1, tm=min(PROMPT_ROWS, T), k_transposed=True)
        ma = _sb_prompt(sb, kt)
        chunk = min(PROMPT_CHUNK, T)
        y, s_out = _hgrn_out(hg, fh, lb_logits, norm_g, None, layer, ma, x, w_out, ln_g, ln_b,
                             C=chunk, sub=min(PROMPT_CHUNKS_PER_STEP, T // chunk), alpha=alpha)
    return y, k_out, v_out, s_out


def kernel(x_prompt, x_sample, cache_k, cache_v, state_s, w_in, w_out, lb_logits, hgrn_norm_g,
           ln_g, ln_b):
    depth = w_in.shape[0]
    alpha = (2 * depth) ** 0.25
    yp, ys = x_prompt, x_sample
    per_layer = []
    for l in range(depth):
        common = (l, w_in[l], w_out[l], lb_logits, hgrn_norm_g[l], ln_g[l], ln_b[l], alpha)
        yp, kp, vp, sp = _layer(yp, None, None, None, *common)
        ys, kn, vn, sn = _layer(ys, cache_k, cache_v, state_s, *common)
        per_layer.append((kp, vp, sp, kn, vn, sn))
    stack = lambda i: (per_layer[0][i] if depth == 1
                       else jnp.concatenate([p[i] for p in per_layer], axis=0))
    return (yp, ys, stack(0), stack(1), stack(2), stack(3), stack(4), stack(5))
```

```python
import functools
import math

import numpy as np
import jax
import jax.numpy as jnp
from jax import lax
from jax.experimental import pallas as pl
from jax.experimental.pallas import tpu as pltpu

F32 = jnp.float32
BF16 = jnp.bfloat16

D_MODEL = 1024
GROUP_W = 512
N_GROUPS = 8
SB_HEADS = 8
SB_HEAD_DIM = 64
HG_HEADS = 4
HG_HEAD_DIM = 128
LN_EPS = 1e-5
RMS_EPS = 1e-6

SUBLANES = 8
KEY_TILE = 256
SB_Q, SB_V, SB_GATE = 0, 1, 2
HG_Q, HG_I, HG_GATE = 0, 1, 2
BOUNDED_DECAY_LOG2 = 100.0
NEGLIGIBLE_LOG2 = 200.0
LOG2E = 1.4426950408889634
VMEM_LIMIT = 56 * 1024 * 1024


def _sigmoid(x):
    return 1.0 / (1.0 + jnp.exp2(x * (-LOG2E)))


def _dot(a, b):
    return jnp.dot(a, b, preferred_element_type=F32)


def _dot_nt(a, b):
    return lax.dot_general(a, b, (((1,), (1,)), ((), ())), preferred_element_type=F32)


def _dot_tn(a, b):
    return lax.dot_general(a, b, (((0,), (0,)), ((), ())), preferred_element_type=F32)


def _split_bf16(x, parts):
    out = []
    for _ in range(parts - 1):
        p = x.astype(BF16)
        out.append(p)
        x = x - p.astype(F32)
    out.append(x.astype(BF16))
    return out


def _proj_kernel(x_ref, w_ref, sb_ref, k_ref, hg_ref, fh_ref, ko_ref, vo_ref, *, nb, tm,
                 k_transposed):
    m = nb * tm
    x = x_ref[...].reshape(m, D_MODEL).astype(BF16)

    def col(c):
        return _dot(x, w_ref[:, c * GROUP_W:(c + 1) * GROUP_W].astype(BF16))

    def put(ref, val, slot=0):
        ref[:, :, slot * GROUP_W:(slot + 1) * GROUP_W] = (
            val.reshape(nb, tm, GROUP_W).astype(ref.dtype))

    def put_heads(ref, val):
        if k_transposed:
            val_t = val.T
            for h in range(SB_HEADS):
                ref[0, 0, h] = val_t[h * SB_HEAD_DIM:(h + 1) * SB_HEAD_DIM, :]
            return val_t
        for b in range(nb):
            for h in range(SB_HEADS):
                ref[0, b, h] = val[b * tm:(b + 1) * tm, h * SB_HEAD_DIM:(h + 1) * SB_HEAD_DIM]

    put(sb_ref, col(0) * (SB_HEAD_DIM ** -0.5 * LOG2E), SB_Q)
    k = col(1)
    kt = put_heads(ko_ref, k)
    if k_transposed:
        kt = kt.astype(BF16)
        for j in range(m // KEY_TILE):
            k_ref[0, j] = kt[:, j * KEY_TILE:(j + 1) * KEY_TILE]
    else:
        put(k_ref, k)
    v = col(2)
    put_heads(vo_ref, v)
    put(sb_ref, v, SB_V)
    ga = col(3)
    put(sb_ref, ga * _sigmoid(ga), SB_GATE)
    put(hg_ref, col(4), HG_Q)
    put(fh_ref, col(5))
    put(hg_ref, col(6), HG_I)
    put(hg_ref, col(7), HG_GATE)


def _project(x, w, *, nb, tm, k_transposed):
    B, T, _ = x.shape
    grid = (B // nb, T // tm)
    act = lambda dt: jax.ShapeDtypeStruct((B, T, GROUP_W), dt)
    act_spec = pl.BlockSpec((nb, tm, GROUP_W), lambda b, t: (b, t, 0))
    act3 = jax.ShapeDtypeStruct((B, T, 3 * GROUP_W), BF16)
    act3_spec = pl.BlockSpec((nb, tm, 3 * GROUP_W), lambda b, t: (b, t, 0))
    if k_transposed:
        assert nb == 1 and tm % KEY_TILE == 0
        k_shape = jax.ShapeDtypeStruct((B, T // KEY_TILE, GROUP_W, KEY_TILE), BF16)
        k_spec = pl.BlockSpec((1, tm // KEY_TILE, GROUP_W, KEY_TILE), lambda b, t: (b, t, 0, 0))
        kv_shape = jax.ShapeDtypeStruct((1, B, SB_HEADS, SB_HEAD_DIM, T), F32)
        kv_spec = pl.BlockSpec((1, 1, SB_HEADS, SB_HEAD_DIM, tm), lambda b, t: (0, b, 0, 0, t))
    else:
        k_shape, k_spec = act(BF16), act_spec
        kv_shape = jax.ShapeDtypeStruct((1, B, SB_HEADS, T, SB_HEAD_DIM), F32)
        kv_spec = pl.BlockSpec((1, nb, SB_HEADS, tm, SB_HEAD_DIM), lambda b, t: (0, b, 0, t, 0))
    outs = pl.pallas_call(
        functools.partial(_proj_kernel, nb=nb, tm=tm, k_transposed=k_transposed),
        grid=grid,
        in_specs=[pl.BlockSpec((nb, tm, D_MODEL), lambda b, t: (b, t, 0)),
                  pl.BlockSpec((D_MODEL, N_GROUPS * GROUP_W), lambda b, t: (0, 0),
                               pipeline_mode=pl.Buffered(1))],
        out_specs=[act3_spec, k_spec, act3_spec, act_spec, kv_spec, kv_spec],
        out_shape=[act3, k_shape, act3, act(F32), kv_shape, kv_shape],
        compiler_params=pltpu.CompilerParams(
            dimension_semantics=("arbitrary", "arbitrary"), vmem_limit_bytes=VMEM_LIMIT),
    )(x, w)
    if k_transposed:
        outs = list(outs[:4]) + [jnp.swapaxes(o, 3, 4) for o in outs[4:]]
    return outs


def _suffix_matrix(n):
    r = lax.broadcasted_iota(jnp.int32, (n, n), 0)
    c = lax.broadcasted_iota(jnp.int32, (n, n), 1)
    return jnp.where(r >= c, 1.0, 0.0).astype(BF16)


def _sb_suffix_sums(zs, suffix, causal):
    drops = []
    for z in zs:
        drop = jnp.maximum(z, 0.0) + jnp.log2(1.0 + jnp.exp2(-jnp.abs(z)))
        if causal is not None:
            drop = jnp.where(causal, drop, 0.0)
        drops.append(drop.astype(BF16))
    return [_dot(drop, suffix) for drop in drops]


def _sb_weight(z, incl, run, causal):
    w = jnp.exp2(z - incl if run is None else z - incl - run)
    if causal is not None:
        w = jnp.where(causal, w, 0.0)
    return w.astype(BF16)


def _sb_prompt_kernel(sb_ref, kt_ref, o_ref, z_ref, w_ref, run_ref, acc_ref, state_ref, *, heads):
    tq = KEY_TILE
    nq = sb_ref.shape[1]
    i = pl.program_id(1)
    suffix = _suffix_matrix(KEY_TILE)
    r = lax.broadcasted_iota(jnp.int32, (tq, KEY_TILE), 0)
    c = lax.broadcasted_iota(jnp.int32, (tq, KEY_TILE), 1)
    causal = c < r
    head_lanes = [slice(hh * SB_HEAD_DIM, (hh + 1) * SB_HEAD_DIM) for hh in range(heads)]

    def packed(slot, lanes):
        return slice(slot * GROUP_W + lanes.start, slot * GROUP_W + lanes.stop)

    def logits(bi, j, hh):
        return _dot(sb_ref[0, bi, :, packed(SB_Q, head_lanes[hh])],
                    kt_ref[0, j, head_lanes[hh], :])

    def add_weighted_values(slot, j):
        for hh in range(heads):
            acc_ref[hh] += _dot(w_ref[slot, hh], sb_ref[0, j, :, packed(SB_V, head_lanes[hh])])

    def finalize(bi):
        o = jnp.concatenate([acc_ref[hh] for hh in range(heads)], axis=1)
        gate = sb_ref[0, bi, :, SB_GATE * GROUP_W:(SB_GATE + 1) * GROUP_W].astype(F32)
        o_ref[0, pl.ds(pl.multiple_of(bi * tq, tq), tq), :] = (o * gate).astype(o_ref.dtype)

    def step(cur, masked, next_bi, next_j):
        mask = causal if masked else None
        zs = [z_ref[cur, hh] for hh in range(heads)]
        incls = _sb_suffix_sums(zs, suffix, mask)
        runs = [None if masked else run_ref[hh] for hh in range(heads)]
        for hh in range(heads):
            total = incls[hh][:, 0:1]
            run_ref[hh] = total if masked else runs[hh] + total
        if not masked:
            lowest = run_ref[0]
            for hh in range(1, heads):
                lowest = jnp.minimum(lowest, run_ref[hh])
            state_ref[2] = (jnp.min(lowest) < NEGLIGIBLE_LOG2).astype(jnp.int32)
        for hh in range(heads):
            z_ref[1 - cur, hh] = logits(next_bi, next_j, hh)
            w_ref[1 - cur, hh] = _sb_weight(zs[hh], incls[hh], runs[hh], mask)

    def refill(slot, bi, j):
        for hh in range(heads):
            z_ref[slot, hh] = logits(bi, j, hh)

    def per_slot(slot, cond, fn):
        for parity in range(2):
            pl.when(jnp.logical_and(cond, slot == parity))(functools.partial(fn, parity))

    @pl.when(i == 0)
    def _():
        for hh in range(heads):
            z_ref[0, hh] = logits(0, 0, hh)
        w_ref[0] = jnp.zeros_like(w_ref[0])
        acc_ref[...] = jnp.zeros_like(acc_ref)
        state_ref[0] = 0
        state_ref[1] = 0
        state_ref[2] = 0

    first_slot = state_ref[0]
    pending_tile = state_ref[1]
    nxt = jnp.minimum(i + 1, nq - 1)

    def enter_block(parity):
        add_weighted_values(parity, pending_tile)
        finalize(jnp.maximum(i - 1, 0))
        acc_ref[...] = jnp.zeros_like(acc_ref)

    def first_block(parity):
        enter_block(parity)
        step(parity, True, nxt, nxt)

    def later_block(parity):
        enter_block(parity)
        step(parity, True, i, i - 1)
        add_weighted_values(1 - parity, i)
        step(1 - parity, False, nxt, nxt)

    per_slot(first_slot, i == 0, first_block)
    per_slot(first_slot, i > 0, later_block)
    tiles = jnp.where(i > 0, 2, 1)
    slot = (first_slot + tiles) % 2
    last_tile = jnp.maximum(i - 1, 0)

    def weight_left():
        return state_ref[2] != 0

    go_on = jnp.logical_and(last_tile > 0, weight_left())
    per_slot(slot, go_on, lambda parity: refill(parity, i, last_tile - 1))

    def sweep(carry):
        n, _ = carry
        j = last_tile - 1 - n

        def tile(parity):
            add_weighted_values(parity, j + 1)
            same_block = j > 0
            step(parity, False, jnp.where(same_block, i, nxt), jnp.where(same_block, j - 1, nxt))

        per_slot((slot + n) % 2, True, tile)
        return n + 1, jnp.logical_and(j > 0, weight_left())

    swept, _ = lax.while_loop(lambda carry: carry[1], sweep, (jnp.int32(0), go_on))
    slot = (slot + swept) % 2
    last_tile = last_tile - swept
    per_slot(slot, jnp.logical_and(swept > 0, last_tile > 0),
             lambda parity: refill(parity, nxt, nxt))
    state_ref[0] = slot
    state_ref[1] = last_tile

    @pl.when(i == nq - 1)
    def _():
        per_slot(slot, True, lambda parity: add_weighted_values(parity, last_tile))
        finalize(i)


def _sb_prompt(sb, kt):
    B, T, _ = sb.shape
    tq = KEY_TILE
    nkt = T // KEY_TILE
    sb4 = sb.reshape(B, nkt, KEY_TILE, 3 * GROUP_W)
    row_spec = pl.BlockSpec((1, T, GROUP_W), lambda b, i: (b, 0, 0))
    tile_spec = lambda shape: pl.BlockSpec((1,) + shape, lambda b, i: (b, 0, 0, 0))
    return pl.pallas_call(
        functools.partial(_sb_prompt_kernel, heads=SB_HEADS),
        grid=(B, T // tq),
        in_specs=[tile_spec((nkt, KEY_TILE, 3 * GROUP_W)), tile_spec((nkt, GROUP_W, KEY_TILE))],
        out_specs=row_spec,
        out_shape=jax.ShapeDtypeStruct((B, T, GROUP_W), BF16),
        scratch_shapes=[pltpu.VMEM((2, SB_HEADS, tq, KEY_TILE), F32),
                        pltpu.VMEM((2, SB_HEADS, tq, KEY_TILE), BF16),
                        pltpu.VMEM((SB_HEADS, tq, 1), F32),
                        pltpu.VMEM((SB_HEADS, tq, SB_HEAD_DIM), F32),
                        pltpu.SMEM((3,), jnp.int32)],
        compiler_params=pltpu.CompilerParams(
            dimension_semantics=("arbitrary", "arbitrary"), vmem_limit_bytes=VMEM_LIMIT),
    )(sb4, kt)


def _sb_decode_kernel(sb_ref, kn_ref, ck_ref, cv_ref, o_ref):
    tq = sb_ref.shape[1]
    past = ck_ref.shape[4]
    suffix = _suffix_matrix(KEY_TILE)
    suffix_new = _suffix_matrix(tq)
    r = lax.broadcasted_iota(jnp.int32, (tq, tq), 0)
    c = lax.broadcasted_iota(jnp.int32, (tq, tq), 1)
    causal = c < r
    heads = range(SB_HEADS)
    head_lanes = [slice(hh * SB_HEAD_DIM, (hh + 1) * SB_HEAD_DIM) for hh in heads]
    group = lambda slot: sb_ref[0, :, slot * GROUP_W:(slot + 1) * GROUP_W]
    q_all, v_all = group(SB_Q), group(SB_V)
    qs = [q_all[:, lanes] for lanes in head_lanes]
    cache_keys = [slice(j * KEY_TILE, (j + 1) * KEY_TILE) for j in reversed(range(past // KEY_TILE))]
    zs = [[_dot_nt(q, kn_ref[0, :, lanes]) for q, lanes in zip(qs, head_lanes)]]
    zs += [[_dot(q, ck_ref[0, 0, hh, :, keys].astype(BF16)) for q, hh in zip(qs, heads)]
           for keys in cache_keys]
    masks = [causal] + [None] * len(cache_keys)
    incls = [_sb_suffix_sums(z, suffix_new if mask is not None else suffix, mask)
             for z, mask in zip(zs, masks)]
    runs = [None] * SB_HEADS
    ws = []
    for z, incl, mask in zip(zs, incls, masks):
        ws.append([_sb_weight(z[hh], incl[hh], runs[hh], mask) for hh in heads])
        runs = [incl[hh][:, 0:1] if runs[hh] is None else runs[hh] + incl[hh][:, 0:1]
                for hh in heads]
    accs = [_dot(w, v_all[:, lanes]) for w, lanes in zip(ws[0], head_lanes)]
    for w_tile, keys in zip(ws[1:], cache_keys):
        accs = [acc + _dot_nt(w, cv_ref[0, 0, hh, :, keys].astype(BF16))
                for w, hh, acc in zip(w_tile, heads, accs)]
    o = jnp.concatenate(accs, axis=1) * group(SB_GATE).astype(F32)
    o_ref[0] = o.astype(o_ref.dtype)


def _sb_decode(sb, kn, cache_k, cache_v, layer):
    B, T, _ = kn.shape
    past = cache_k.shape[3]
    assert past % KEY_TILE == 0
    row_spec = lambda width: pl.BlockSpec((1, T, width), lambda b: (b, 0, 0))
    cache_spec = pl.BlockSpec((1, 1, SB_HEADS, SB_HEAD_DIM, past), lambda b: (layer, b, 0, 0, 0))
    cache_k = jnp.swapaxes(cache_k, 3, 4)
    cache_v = jnp.swapaxes(cache_v, 3, 4)
    return pl.pallas_call(
        _sb_decode_kernel,
        grid=(B,),
        in_specs=[row_spec(3 * GROUP_W), row_spec(GROUP_W), cache_spec, cache_spec],
        out_specs=row_spec(GROUP_W),
        out_shape=jax.ShapeDtypeStruct((B, T, GROUP_W), BF16),
        compiler_params=pltpu.CompilerParams(
            dimension_semantics=("arbitrary",), vmem_limit_bytes=VMEM_LIMIT),
    )(sb, kn, cache_k, cache_v)


def _hgrn_tables(C):
    nl = int(math.log2(C))
    assert 1 << nl == C
    n_small = min(nl, int(math.log2(SUBLANES)))
    t = np.arange(C)[:, None]
    j = np.arange(C)[None, :]
    blocks = []
    for l in range(n_small):
        half = 1 << l
        mid = (t >> (l + 1) << (l + 1)) + half
        right = (t & half) != 0
        blocks.append(np.where(right, (j >= mid) & (j <= t), (j > t) & (j < mid)))
    blocks.append(j <= t)
    prefix = np.concatenate(blocks, axis=0).astype(np.float32)
    x = t ^ j
    msb = np.floor(np.log2(np.maximum(x, 1))).astype(np.int32)
    level_of = np.where(t > j, msb, np.where(t == j, nl, -1)).astype(np.int32)
    return jnp.asarray(prefix, BF16), jnp.asarray(level_of), nl, n_small


def _hgrn_out_kernel(*refs, C, sub, streams, nl, n_small, layer, has_s0, alpha):
    (hg_ref, fh_ref, pm_ref, lv_ref, lbl_ref, ng_ref, ma_ref, x_ref, wo_ref,
     lg_ref, lnb_ref) = refs[:11]
    if has_s0:
        s0_ref, y_ref, so_ref, st_ref, b_ref, sc_ref = refs[11:]
    else:
        y_ref, so_ref, st_ref, b_ref, sc_ref = refs[11:]
    t = pl.program_id(1)

    def packed(rows, slot):
        return hg_ref[0, rows, slot * GROUP_W:(slot + 1) * GROUP_W]

    group = sub if streams > 1 else min(sub, OUT_GROUP_CHUNKS)
    group_rows = [slice(g * group * C, (g + 1) * group * C) for g in range(sub // group)]
    w_attn = wo_ref[:GROUP_W, :].astype(BF16)
    w_hgrn = wo_ref[GROUP_W:, :].astype(BF16)
    out_attn = [_dot(ma_ref[0, rows, :], w_attn) for rows in group_rows]

    @pl.when(t == 0)
    def _():
        if has_s0:
            for s in range(streams):
                for h in range(HG_HEADS):
                    st_ref[s, h] = s0_ref[0, s, h].T
        else:
            st_ref[...] = jnp.zeros_like(st_ref)

    logits = lbl_ref[...]
    e = jnp.exp(logits - jnp.max(logits, axis=0, keepdims=True))
    lb = jnp.sum(e[:layer + 1], axis=0, keepdims=True) / jnp.sum(e, axis=0, keepdims=True)

    def head(a, h):
        return a[:, h * HG_HEAD_DIM:(h + 1) * HG_HEAD_DIM]

    def prepare(c):
        rows = slice(c * C, (c + 1) * C)
        f = lb + (1.0 - lb) * _sigmoid(fh_ref[0, rows, :])
        g = jnp.log2(f)
        kin = 1.0 - f
        qh = packed(rows, HG_Q).astype(F32)
        qs = qh * _sigmoid(qh)
        g_hi, g_lo = _split_bf16(g, 2)
        cum = _dot(pm_ref[n_small * C:, :], jnp.concatenate([g_hi, g_lo], axis=1))
        b = cum[:, :GROUP_W] + cum[:, GROUP_W:]
        b_ref[c] = b
        q_dec = (qs * jnp.exp2(b)).astype(BF16)
        return dict(rows=rows, qs=qs, kin=kin, g_hi=g_hi, q_dec=q_dec)

    def level_log(c, pre, l):
        if l < n_small:
            return pre[l * C:(l + 1) * C]
        half = 1 << l
        parts = []
        for lo in range(0, C, 2 * half):
            mid = lo + half
            last_left = b_ref[c, mid - 1:mid, :]
            parts.append(last_left - b_ref[c, lo:mid, :])
            parts.append(b_ref[c, mid:mid + half, :] - last_left)
        return jnp.concatenate(parts, axis=0)

    def operands(c, p, l):
        if l == nl:
            return p["qs_b"], p["kin_b"]
        e_l = jnp.exp2(level_log(c, p["pre"], l)).astype(BF16)
        return p["qs_b"] * e_l, p["kin_b"] * e_l

    def scores_by_levels():
        ops = [dict(pre=_dot(pm_ref[:n_small * C, :], p["g_hi"]),
                    qs_b=p["qs"].astype(BF16), kin_b=p["kin"].astype(BF16)) for p in preps]
        level_of = lv_ref[...]
        order = [(l, c) for l in [nl] + list(range(nl)) for c in chunks]
        scores = [[0.0] * HG_HEADS for _ in chunks]
        nxt = operands(order[0][1], ops[order[0][1]], order[0][0])
        for idx, (l, c) in enumerate(order):
            q_l, k_l = nxt
            if idx + 1 < len(order):
                l_n, c_n = order[idx + 1]
                nxt = operands(c_n, ops[c_n], l_n)
            hit = level_of == l
            for h in range(HG_HEADS):
                scores[c][h] = jnp.where(hit, _dot_nt(head(q_l, h), head(k_l, h)), scores[c][h])
        for c in chunks:
            for h in range(HG_HEADS):
                sc_ref[c, h] = scores[c][h]

    def scores_from_chunk_start():
        on_or_below = lv_ref[...] >= 0
        for c, p in enumerate(preps):
            k_grow = (p["kin"] * jnp.exp2(-b_ref[c])).astype(BF16)
            for h in range(HG_HEADS):
                sc_ref[c, h] = jnp.where(
                    on_or_below, _dot_nt(head(p["q_dec"], h), head(k_grow, h)), 0.0)

    def finish(c, p):
        rows = p["rows"]
        stream = c if streams > 1 else 0
        b = b_ref[c]
        b_last = b_ref[c, C - 1:C, :]
        k_out = (p["kin"] * jnp.exp2(b_last - b)).astype(BF16)
        chunk_decay = jnp.exp2(b_last)
        i_b = packed(rows, HG_I)
        gh = packed(rows, HG_GATE).astype(F32)
        gate = ng_ref[...] * (gh * _sigmoid(gh))
        outs = []
        for h in range(HG_HEADS):
            st = st_ref[stream, h]
            o = (_dot(sc_ref[c, h].astype(BF16), head(i_b, h))
                 + _dot_nt(head(p["q_dec"], h), st.astype(BF16)))
            st_ref[stream, h] = (st * head(chunk_decay, h)
                                 + _dot_tn(head(i_b, h), head(k_out, h)))
            outs.append(o * lax.rsqrt(jnp.mean(o * o, axis=-1, keepdims=True) + RMS_EPS))
        return (jnp.concatenate(outs, axis=1) * gate).astype(BF16)

    def project_out(g, mixed_hgrn):
        rows = group_rows[g]
        hid = alpha * x_ref[0, rows, :] + (out_attn[g] + _dot(mixed_hgrn, w_hgrn))
        mu = jnp.mean(hid, axis=-1, keepdims=True)
        cen = hid - mu
        var = jnp.mean(cen * cen, axis=-1, keepdims=True)
        y_ref[0, rows, :] = cen * lax.rsqrt(var + LN_EPS) * lg_ref[...] + lnb_ref[...]

    chunks = range(sub)
    preps = [prepare(c) for c in chunks]
    lowest = b_ref[0, C - 1:C, :]
    for c in range(1, sub):
        lowest = jnp.minimum(lowest, b_ref[c, C - 1:C, :])
    bounded = jnp.min(lowest) > -BOUNDED_DECAY_LOG2
    pl.when(bounded)(scores_from_chunk_start)
    pl.when(jnp.logical_not(bounded))(scores_by_levels)
    mixed = []
    for c in chunks:
        mixed.append(finish(c, preps[c]))
        if len(mixed) == group:
            project_out(c // group, mixed[0] if group == 1 else jnp.concatenate(mixed, axis=0))
            mixed = []

    @pl.when(t == pl.num_programs(1) - 1)
    def _():
        for s in range(streams):
            for h in range(HG_HEADS):
                so_ref[0, s, h] = st_ref[s, h].T


def _hgrn_out(hg, fh, lb_logits, norm_g, s0, layer, ma, x, w_out, ln_g, ln_b, *, C, sub, alpha,
              streams_per_step=1):
    n_streams, T, _ = fh.shape
    streams = streams_per_step
    if streams > 1:
        assert T == C and sub == streams and n_streams % streams == 0
        fold = lambda a: a.reshape(n_streams // streams, streams * T, a.shape[-1])
        hg, fh, ma, x = fold(hg), fold(fh), fold(ma), fold(x)
    B, T, _ = fh.shape
    prefix, level_of, nl, n_small = _hgrn_tables(C)
    has_s0 = s0 is not None
    rows = sub * C
    row_spec = pl.BlockSpec((1, rows, GROUP_W), lambda b, t: (b, t, 0))
    row3_spec = pl.BlockSpec((1, rows, 3 * GROUP_W), lambda b, t: (b, t, 0))
    wide_spec = pl.BlockSpec((1, rows, D_MODEL), lambda b, t: (b, t, 0))
    const = lambda shape: pl.BlockSpec(shape, lambda b, t: (0,) * len(shape))
    state_shape = (1, streams, HG_HEADS, HG_HEAD_DIM, HG_HEAD_DIM)
    in_specs = [row3_spec, row_spec, const(prefix.shape), const(level_of.shape),
                const(lb_logits.shape), const((1, GROUP_W)), row_spec, wide_spec,
                const(w_out.shape), const((1, D_MODEL)), const((1, D_MODEL))]
    args = [hg, fh, prefix, level_of, lb_logits.astype(F32),
            norm_g.reshape(1, GROUP_W).astype(F32), ma, x, w_out,
            ln_g.reshape(1, D_MODEL).astype(F32), ln_b.reshape(1, D_MODEL).astype(F32)]
    if has_s0:
        in_specs.append(pl.BlockSpec(state_shape, lambda b, t: (layer, b, 0, 0, 0)))
        args.append(s0)
    y, s_out = pl.pallas_call(
        functools.partial(_hgrn_out_kernel, C=C, sub=sub, streams=streams, nl=nl, n_small=n_small,
                          layer=layer, has_s0=has_s0, alpha=alpha),
        grid=(B, T // rows),
        in_specs=in_specs,
        out_specs=[wide_spec, pl.BlockSpec(state_shape, lambda b, t: (0, b, 0, 0, 0))],
        out_shape=[jax.ShapeDtypeStruct((B, T, D_MODEL), F32),
                   jax.ShapeDtypeStruct((1, n_streams, HG_HEADS, HG_HEAD_DIM, HG_HEAD_DIM), F32)],
        scratch_shapes=[pltpu.VMEM((streams, HG_HEADS, HG_HEAD_DIM, HG_HEAD_DIM), F32),
                        pltpu.VMEM((sub, C, GROUP_W), F32),
                        pltpu.VMEM((sub, HG_HEADS, C, C), F32)],
        compiler_params=pltpu.CompilerParams(
            dimension_semantics=("arbitrary", "arbitrary"), vmem_limit_bytes=VMEM_LIMIT),
    )(*args)
    return y.reshape(n_streams, -1, D_MODEL), s_out


PROMPT_ROWS = 512
PROMPT_CHUNK = 128
OUT_GROUP_CHUNKS = 4
PROMPT_CHUNKS_PER_STEP = 8


def _layer(x, cache_k, cache_v, state_s, layer, w_in, w_out, lb_logits, norm_g, ln_g, ln_b, alpha):
    B, T, _ = x.shape
    decode = cache_k is not None
    if decode:
        sb, k, hg, fh, k_out, v_out = _project(x, w_in, nb=B, tm=T, k_transposed=False)
        ma = _sb_decode(sb, k, cache_k, cache_v, layer)
        y, s_out = _hgrn_out(hg, fh, lb_logits, norm_g, state_s, layer, ma, x, w_out, ln_g, ln_b,
                             C=T, sub=B, alpha=alpha, streams_per_step=B)
    else:
        sb, kt, hg, fh, k_out, v_out = _project(
            x, w_in, nb=1, tm=min(PROMPT_ROWS, T), k_transposed=True)
        ma = _sb_prompt(sb, kt)
        chunk = min(PROMPT_CHUNK, T)
        y, s_out = _hgrn_out(hg, fh, lb_logits, norm_g, None, layer, ma, x, w_out, ln_g, ln_b,
                             C=chunk, sub=min(PROMPT_CHUNKS_PER_STEP, T // chunk), alpha=alpha)
    return y, k_out, v_out, s_out


def kernel(x_prompt, x_sample, cache_k, cache_v, state_s, w_in, w_out, lb_logits, hgrn_norm_g,
           ln_g, ln_b):
    depth = w_in.shape[0]
    alpha = (2 * depth) ** 0.25
    yp, ys = x_prompt, x_sample
    per_layer = []
    for l in range(depth):
        common = (l, w_in[l], w_out[l], lb_logits, hgrn_norm_g[l], ln_g[l], ln_b[l], alpha)
        yp, kp, vp, sp = _layer(yp, None, None, None, *common)
        ys, kn, vn, sn = _layer(ys, cache_k, cache_v, state_s, *common)
        per_layer.append((kp, vp, sp, kn, vn, sn))
    stack = lambda i: (per_layer[0][i] if depth == 1
                       else jnp.concatenate([p[i] for p in per_layer], axis=0))
    return (yp, ys, stack(0), stack(1), stack(2), stack(3), stack(4), stack(5))
```

```python
import functools
import math

import numpy as np
import jax
import jax.numpy as jnp
from jax import lax
from jax.experimental import pallas as pl
from jax.experimental.pallas import tpu as pltpu

F32 = jnp.float32
BF16 = jnp.bfloat16

D_MODEL = 1024
GROUP_W = 512
N_GROUPS = 8
SB_HEADS = 8
SB_HEAD_DIM = 64
HG_HEADS = 4
HG_HEAD_DIM = 128
LN_EPS = 1e-5
RMS_EPS = 1e-6

SUBLANES = 8
KEY_TILE = 256
SB_Q, SB_V, SB_GATE = 0, 1, 2
HG_Q, HG_I, HG_GATE = 0, 1, 2
BOUNDED_DECAY_LOG2 = 100.0
NEGLIGIBLE_LOG2 = 200.0
LOG2E = 1.4426950408889634
VMEM_LIMIT = 56 * 1024 * 1024


def _sigmoid(x):
    return 1.0 / (1.0 + jnp.exp2(x * (-LOG2E)))


def _dot(a, b):
    return jnp.dot(a, b, preferred_element_type=F32)


def _dot_nt(a, b):
    return lax.dot_general(a, b, (((1,), (1,)), ((), ())), preferred_element_type=F32)


def _dot_tn(a, b):
    return lax.dot_general(a, b, (((0,), (0,)), ((), ())), preferred_element_type=F32)


def _split_bf16(x, parts):
    out = []
    for _ in range(parts - 1):
        p = x.astype(BF16)
        out.append(p)
        x = x - p.astype(F32)
    out.append(x.astype(BF16))
    return out


def _proj_kernel(x_ref, w_ref, sb_ref, k_ref, hg_ref, fh_ref, ko_ref, vo_ref, *, nb, tm,
                 k_transposed):
    m = nb * tm
    x = x_ref[...].reshape(m, D_MODEL).astype(BF16)

    def col(c):
        return _dot(x, w_ref[:, c * GROUP_W:(c + 1) * GROUP_W].astype(BF16))

    def put(ref, val, slot=0):
        ref[:, :, slot * GROUP_W:(slot + 1) * GROUP_W] = (
            val.reshape(nb, tm, GROUP_W).astype(ref.dtype))

    def put_heads(ref, val):
        if k_transposed:
            val_t = val.T
            for h in range(SB_HEADS):
                ref[0, 0, h] = val_t[h * SB_HEAD_DIM:(h + 1) * SB_HEAD_DIM, :]
            return val_t
        for b in range(nb):
            for h in range(SB_HEADS):
                ref[0, b, h] = val[b * tm:(b + 1) * tm, h * SB_HEAD_DIM:(h + 1) * SB_HEAD_DIM]

    put(sb_ref, col(0) * (SB_HEAD_DIM ** -0.5 * LOG2E), SB_Q)
    k = col(1)
    kt = put_heads(ko_ref, k)
    if k_transposed:
        kt = kt.astype(BF16)
        for j in range(m // KEY_TILE):
            k_ref[0, j] = kt[:, j * KEY_TILE:(j + 1) * KEY_TILE]
    else:
        put(k_ref, k)
    v = col(2)
    put_heads(vo_ref, v)
    put(sb_ref, v, SB_V)
    ga = col(3)
    put(sb_ref, ga * _sigmoid(ga), SB_GATE)
    put(hg_ref, col(4), HG_Q)
    put(fh_ref, col(5))
    put(hg_ref, col(6), HG_I)
    put(hg_ref, col(7), HG_GATE)


def _project(x, w, *, nb, tm, k_transposed):
    B, T, _ = x.shape
    grid = (B // nb, T // tm)
    act = lambda dt: jax.ShapeDtypeStruct((B, T, GROUP_W), dt)
    act_spec = pl.BlockSpec((nb, tm, GROUP_W), lambda b, t: (b, t, 0))
    act3 = jax.ShapeDtypeStruct((B, T, 3 * GROUP_W), BF16)
    act3_spec = pl.BlockSpec((nb, tm, 3 * GROUP_W), lambda b, t: (b, t, 0))
    if k_transposed:
        assert nb == 1 and tm % KEY_TILE == 0
        k_shape = jax.ShapeDtypeStruct((B, T // KEY_TILE, GROUP_W, KEY_TILE), BF16)
        k_spec = pl.BlockSpec((1, tm // KEY_TILE, GROUP_W, KEY_TILE), lambda b, t: (b, t, 0, 0))
        kv_shape = jax.ShapeDtypeStruct((1, B, SB_HEADS, SB_HEAD_DIM, T), F32)
        kv_spec = pl.BlockSpec((1, 1, SB_HEADS, SB_HEAD_DIM, tm), lambda b, t: (0, b, 0, 0, t))
    else:
        k_shape, k_spec = act(BF16), act_spec
        kv_shape = jax.ShapeDtypeStruct((1, B, SB_HEADS, T, SB_HEAD_DIM), F32)
        kv_spec = pl.BlockSpec((1, nb, SB_HEADS, tm, SB_HEAD_DIM), lambda b, t: (0, b, 0, t, 0))
    outs = pl.pallas_call(
        functools.partial(_proj_kernel, nb=nb, tm=tm, k_transposed=k_transposed),
        grid=grid,
        in_specs=[pl.BlockSpec((nb, tm, D_MODEL), lambda b, t: (b, t, 0)),
                  pl.BlockSpec((D_MODEL, N_GROUPS * GROUP_W), lambda b, t: (0, 0),
                               pipeline_mode=pl.Buffered(1))],
        out_specs=[act3_spec, k_spec, act3_spec, act_spec, kv_spec, kv_spec],
        out_shape=[act3, k_shape, act3, act(F32), kv_shape, kv_shape],
        compiler_params=pltpu.CompilerParams(
            dimension_semantics=("arbitrary", "arbitrary"), vmem_limit_bytes=VMEM_LIMIT),
    )(x, w)
    if k_transposed:
        outs = list(outs[:4]) + [jnp.swapaxes(o, 3, 4) for o in outs[4:]]
    return outs


def _suffix_matrix(n):
    r = lax.broadcasted_iota(jnp.int32, (n, n), 0)
    c = lax.broadcasted_iota(jnp.int32, (n, n), 1)
    return jnp.where(r >= c, 1.0, 0.0).astype(BF16)


def _sb_suffix_sums(zs, suffix, causal):
    drops = []
    for z in zs:
        drop = jnp.maximum(z, 0.0) + jnp.log2(1.0 + jnp.exp2(-jnp.abs(z)))
        if causal is not None:
            drop = jnp.where(causal, drop, 0.0)
        drops.append(drop.astype(BF16))
    return [_dot(drop, suffix) for drop in drops]


def _sb_weight(z, incl, run, causal):
    w = jnp.exp2(z - incl if run is None else z - incl - run)
    if causal is not None:
        w = jnp.where(causal, w, 0.0)
    return w.astype(BF16)


def _sb_prompt_kernel(sb_ref, kt_ref, o_ref, z_ref, w_ref, run_ref, acc_ref, state_ref, *, heads):
    tq = KEY_TILE
    nq = sb_ref.shape[1]
    suffix = _suffix_matrix(KEY_TILE)
    r = lax.broadcasted_iota(jnp.int32, (tq, KEY_TILE), 0)
    c = lax.broadcasted_iota(jnp.int32, (tq, KEY_TILE), 1)
    causal = c < r
    head_lanes = [slice(hh * SB_HEAD_DIM, (hh + 1) * SB_HEAD_DIM) for hh in range(heads)]

    def packed(slot, lanes):
        return slice(slot * GROUP_W + lanes.start, slot * GROUP_W + lanes.stop)

    def logits(bi, j, hh):
        return _dot(sb_ref[0, bi, :, packed(SB_Q, head_lanes[hh])],
                    kt_ref[0, j, head_lanes[hh], :])

    def add_weighted_values(slot, j):
        for hh in range(heads):
            acc_ref[hh] += _dot(w_ref[slot, hh], sb_ref[0, j, :, packed(SB_V, head_lanes[hh])])

    def finalize(bi):
        o = jnp.concatenate([acc_ref[hh] for hh in range(heads)], axis=1)
        gate = sb_ref[0, bi, :, SB_GATE * GROUP_W:(SB_GATE + 1) * GROUP_W].astype(F32)
        o_ref[0, pl.ds(pl.multiple_of(bi * tq, tq), tq), :] = (o * gate).astype(o_ref.dtype)

    def step(cur, masked, next_bi, next_j):
        mask = causal if masked else None
        zs = [z_ref[cur, hh] for hh in range(heads)]
        incls = _sb_suffix_sums(zs, suffix, mask)
        runs = [None if masked else run_ref[hh] for hh in range(heads)]
        for hh in range(heads):
            total = incls[hh][:, 0:1]
            run_ref[hh] = total if masked else runs[hh] + total
        if not masked:
            lowest = run_ref[0]
            for hh in range(1, heads):
                lowest = jnp.minimum(lowest, run_ref[hh])
            state_ref[2] = (jnp.min(lowest) < NEGLIGIBLE_LOG2).astype(jnp.int32)
        for hh in range(heads):
            z_ref[1 - cur, hh] = logits(next_bi, next_j, hh)
            w_ref[1 - cur, hh] = _sb_weight(zs[hh], incls[hh], runs[hh], mask)

    def refill(slot, bi, j):
        for hh in range(heads):
            z_ref[slot, hh] = logits(bi, j, hh)

    def per_slot(slot, cond, fn):
        for parity in range(2):
            pl.when(jnp.logical_and(cond, slot == parity))(functools.partial(fn, parity))

    for hh in range(heads):
        z_ref[0, hh] = logits(0, 0, hh)
    w_ref[0] = jnp.zeros_like(w_ref[0])
    acc_ref[...] = jnp.zeros_like(acc_ref)
    state_ref[0] = 0
    state_ref[1] = 0
    state_ref[2] = 0

    def block(i, carry):
        first_slot = state_ref[0]
        pending_tile = state_ref[1]
        nxt = jnp.minimum(i + 1, nq - 1)

        def enter_block(parity):
            add_weighted_values(parity, pending_tile)
            finalize(jnp.maximum(i - 1, 0))
            acc_ref[...] = jnp.zeros_like(acc_ref)

        def first_block(parity):
            enter_block(parity)
            step(parity, True, nxt, nxt)

        def later_block(parity):
            enter_block(parity)
            step(parity, True, i, i - 1)
            add_weighted_values(1 - parity, i)
            step(1 - parity, False, nxt, nxt)

        per_slot(first_slot, i == 0, first_block)
        per_slot(first_slot, i > 0, later_block)
        tiles = jnp.where(i > 0, 2, 1)
        slot = (first_slot + tiles) % 2
        last_tile = jnp.maximum(i - 1, 0)

        def weight_left():
            return state_ref[2] != 0

        go_on = jnp.logical_and(last_tile > 0, weight_left())
        per_slot(slot, go_on, lambda parity: refill(parity, i, last_tile - 1))

        def sweep(carry):
            n, _ = carry
            j = last_tile - 1 - n

            def tile(parity):
                add_weighted_values(parity, j + 1)
                same_block = j > 0
                step(parity, False, jnp.where(same_block, i, nxt),
                     jnp.where(same_block, j - 1, nxt))

            per_slot((slot + n) % 2, True, tile)
            return n + 1, jnp.logical_and(j > 0, weight_left())

        swept, _ = lax.while_loop(lambda carry: carry[1], sweep, (jnp.int32(0), go_on))
        slot = (slot + swept) % 2
        last_tile = last_tile - swept
        per_slot(slot, jnp.logical_and(swept > 0, last_tile > 0),
                 lambda parity: refill(parity, nxt, nxt))
        state_ref[0] = slot
        state_ref[1] = last_tile
        return carry

    lax.fori_loop(0, nq, block, 0)
    last_slot, last_tile = state_ref[0], state_ref[1]
    per_slot(last_slot, True, lambda parity: add_weighted_values(parity, last_tile))
    finalize(nq - 1)


def _sb_prompt(sb, kt):
    B, T, _ = sb.shape
    tq = KEY_TILE
    nkt = T // KEY_TILE
    sb4 = sb.reshape(B, nkt, KEY_TILE, 3 * GROUP_W)
    row_spec = pl.BlockSpec((1, T, GROUP_W), lambda b: (b, 0, 0))
    tile_spec = lambda shape: pl.BlockSpec((1,) + shape, lambda b: (b, 0, 0, 0))
    return pl.pallas_call(
        functools.partial(_sb_prompt_kernel, heads=SB_HEADS),
        grid=(B,),
        in_specs=[tile_spec((nkt, KEY_TILE, 3 * GROUP_W)), tile_spec((nkt, GROUP_W, KEY_TILE))],
        out_specs=row_spec,
        out_shape=jax.ShapeDtypeStruct((B, T, GROUP_W), BF16),
        scratch_shapes=[pltpu.VMEM((2, SB_HEADS, tq, KEY_TILE), F32),
                        pltpu.VMEM((2, SB_HEADS, tq, KEY_TILE), BF16),
                        pltpu.VMEM((SB_HEADS, tq, 1), F32),
                        pltpu.VMEM((SB_HEADS, tq, SB_HEAD_DIM), F32),
                        pltpu.SMEM((3,), jnp.int32)],
        compiler_params=pltpu.CompilerParams(
            dimension_semantics=("arbitrary",), vmem_limit_bytes=VMEM_LIMIT),
    )(sb4, kt)


def _sb_decode_kernel(sb_ref, kn_ref, ck_ref, cv_ref, o_ref):
    tq = sb_ref.shape[1]
    past = ck_ref.shape[4]
    suffix = _suffix_matrix(KEY_TILE)
    suffix_new = _suffix_matrix(tq)
    r = lax.broadcasted_iota(jnp.int32, (tq, tq), 0)
    c = lax.broadcasted_iota(jnp.int32, (tq, tq), 1)
    causal = c < r
    heads = range(SB_HEADS)
    head_lanes = [slice(hh * SB_HEAD_DIM, (hh + 1) * SB_HEAD_DIM) for hh in heads]
    group = lambda slot: sb_ref[0, :, slot * GROUP_W:(slot + 1) * GROUP_W]
    q_all, v_all = group(SB_Q), group(SB_V)
    qs = [q_all[:, lanes] for lanes in head_lanes]
    cache_keys = [slice(j * KEY_TILE, (j + 1) * KEY_TILE) for j in reversed(range(past // KEY_TILE))]
    zs = [[_dot_nt(q, kn_ref[0, :, lanes]) for q, lanes in zip(qs, head_lanes)]]
    zs += [[_dot(q, ck_ref[0, 0, hh, :, keys].astype(BF16)) for q, hh in zip(qs, heads)]
           for keys in cache_keys]
    masks = [causal] + [None] * len(cache_keys)
    incls = [_sb_suffix_sums(z, suffix_new if mask is not None else suffix, mask)
             for z, mask in zip(zs, masks)]
    runs = [None] * SB_HEADS
    ws = []
    for z, incl, mask in zip(zs, incls, masks):
        ws.append([_sb_weight(z[hh], incl[hh], runs[hh], mask) for hh in heads])
        runs = [incl[hh][:, 0:1] if runs[hh] is None else runs[hh] + incl[hh][:, 0:1]
                for hh in heads]
    accs = [_dot(w, v_all[:, lanes]) for w, lanes in zip(ws[0], head_lanes)]
    for w_tile, keys in zip(ws[1:], cache_keys):
        accs = [acc + _dot_nt(w, cv_ref[0, 0, hh, :, keys].astype(BF16))
                for w, hh, acc in zip(w_tile, heads, accs)]
    o = jnp.concatenate(accs, axis=1) * group(SB_GATE).astype(F32)
    o_ref[0] = o.astype(o_ref.dtype)


def _sb_decode(sb, kn, cache_k, cache_v, layer):
    B, T, _ = kn.shape
    past = cache_k.shape[3]
    assert past % KEY_TILE == 0
    row_spec = lambda width: pl.BlockSpec((1, T, width), lambda b: (b, 0, 0))
    cache_spec = pl.BlockSpec((1, 1, SB_HEADS, SB_HEAD_DIM, past), lambda b: (layer, b, 0, 0, 0))
    cache_k = jnp.swapaxes(cache_k, 3, 4)
    cache_v = jnp.swapaxes(cache_v, 3, 4)
    return pl.pallas_call(
        _sb_decode_kernel,
        grid=(B,),
        in_specs=[row_spec(3 * GROUP_W), row_spec(GROUP_W), cache_spec, cache_spec],
        out_specs=row_spec(GROUP_W),
        out_shape=jax.ShapeDtypeStruct((B, T, GROUP_W), BF16),
        compiler_params=pltpu.CompilerParams(
            dimension_semantics=("arbitrary",), vmem_limit_bytes=VMEM_LIMIT),
    )(sb, kn, cache_k, cache_v)


def _hgrn_tables(C):
    nl = int(math.log2(C))
    assert 1 << nl == C
    n_small = min(nl, int(math.log2(SUBLANES)))
    t = np.arange(C)[:, None]
    j = np.arange(C)[None, :]
    blocks = []
    for l in range(n_small):
        half = 1 << l
        mid = (t >> (l + 1) << (l + 1)) + half
        right = (t & half) != 0
        blocks.append(np.where(right, (j >= mid) & (j <= t), (j > t) & (j < mid)))
    blocks.append(j <= t)
    prefix = np.concatenate(blocks, axis=0).astype(np.float32)
    x = t ^ j
    msb = np.floor(np.log2(np.maximum(x, 1))).astype(np.int32)
    level_of = np.where(t > j, msb, np.where(t == j, nl, -1)).astype(np.int32)
    return jnp.asarray(prefix, BF16), jnp.asarray(level_of), nl, n_small


def _hgrn_out_kernel(*refs, C, sub, streams, nl, n_small, layer, has_s0, alpha):
    (hg_ref, fh_ref, pm_ref, lv_ref, lbl_ref, ng_ref, ma_ref, x_ref, wo_ref,
     lg_ref, lnb_ref) = refs[:11]
    if has_s0:
        s0_ref, y_ref, so_ref, st_ref, b_ref, sc_ref = refs[11:]
    else:
        y_ref, so_ref, st_ref, b_ref, sc_ref = refs[11:]
    t = pl.program_id(1)

    def packed(rows, slot):
        return hg_ref[0, rows, slot * GROUP_W:(slot + 1) * GROUP_W]

    group = sub if streams > 1 else min(sub, OUT_GROUP_CHUNKS)
    group_rows = [slice(g * group * C, (g + 1) * group * C) for g in range(sub // group)]
    w_attn = wo_ref[:GROUP_W, :].astype(BF16)
    w_hgrn = wo_ref[GROUP_W:, :].astype(BF16)
    out_attn = [_dot(ma_ref[0, rows, :], w_attn) for rows in group_rows]

    @pl.when(t == 0)
    def _():
        if has_s0:
            for s in range(streams):
                for h in range(HG_HEADS):
                    st_ref[s, h] = s0_ref[0, s, h].T
        else:
            st_ref[...] = jnp.zeros_like(st_ref)

    logits = lbl_ref[...]
    e = jnp.exp(logits - jnp.max(logits, axis=0, keepdims=True))
    lb = jnp.sum(e[:layer + 1], axis=0, keepdims=True) / jnp.sum(e, axis=0, keepdims=True)

    def head(a, h):
        return a[:, h * HG_HEAD_DIM:(h + 1) * HG_HEAD_DIM]

    def prepare(c):
        rows = slice(c * C, (c + 1) * C)
        f = lb + (1.0 - lb) * _sigmoid(fh_ref[0, rows, :])
        g = jnp.log2(f)
        kin = 1.0 - f
        qh = packed(rows, HG_Q).astype(F32)
        qs = qh * _sigmoid(qh)
        g_hi, g_lo = _split_bf16(g, 2)
        cum = _dot(pm_ref[n_small * C:, :], jnp.concatenate([g_hi, g_lo], axis=1))
        b = cum[:, :GROUP_W] + cum[:, GROUP_W:]
        b_ref[c] = b
        q_dec = (qs * jnp.exp2(b)).astype(BF16)
        return dict(rows=rows, qs=qs, kin=kin, g_hi=g_hi, q_dec=q_dec)

    def level_log(c, pre, l):
        if l < n_small:
            return pre[l * C:(l + 1) * C]
        half = 1 << l
        parts = []
        for lo in range(0, C, 2 * half):
            mid = lo + half
            last_left = b_ref[c, mid - 1:mid, :]
            parts.append(last_left - b_ref[c, lo:mid, :])
            parts.append(b_ref[c, mid:mid + half, :] - last_left)
        return jnp.concatenate(parts, axis=0)

    def operands(c, p, l):
        if l == nl:
            return p["qs_b"], p["kin_b"]
        e_l = jnp.exp2(level_log(c, p["pre"], l)).astype(BF16)
        return p["qs_b"] * e_l, p["kin_b"] * e_l

    def scores_by_levels():
        ops = [dict(pre=_dot(pm_ref[:n_small * C, :], p["g_hi"]),
                    qs_b=p["qs"].astype(BF16), kin_b=p["kin"].astype(BF16)) for p in preps]
        level_of = lv_ref[...]
        order = [(l, c) for l in [nl] + list(range(nl)) for c in chunks]
        scores = [[0.0] * HG_HEADS for _ in chunks]
        nxt = operands(order[0][1], ops[order[0][1]], order[0][0])
        for idx, (l, c) in enumerate(order):
            q_l, k_l = nxt
            if idx + 1 < len(order):
                l_n, c_n = order[idx + 1]
                nxt = operands(c_n, ops[c_n], l_n)
            hit = level_of == l
            for h in range(HG_HEADS):
                scores[c][h] = jnp.where(hit, _dot_nt(head(q_l, h), head(k_l, h)), scores[c][h])
        for c in chunks:
            for h in range(HG_HEADS):
                sc_ref[c, h] = scores[c][h]

    def scores_from_chunk_start():
        on_or_below = lv_ref[...] >= 0
        for c, p in enumerate(preps):
            k_grow = (p["kin"] * jnp.exp2(-b_ref[c])).astype(BF16)
            for h in range(HG_HEADS):
                sc_ref[c, h] = jnp.where(
                    on_or_below, _dot_nt(head(p["q_dec"], h), head(k_grow, h)), 0.0)

    def finish(c, p):
        rows = p["rows"]
        stream = c if streams > 1 else 0
        b = b_ref[c]
        b_last = b_ref[c, C - 1:C, :]
        k_out = (p["kin"] * jnp.exp2(b_last - b)).astype(BF16)
        chunk_decay = jnp.exp2(b_last)
        i_b = packed(rows, HG_I)
        gh = packed(rows, HG_GATE).astype(F32)
        gate = ng_ref[...] * (gh * _sigmoid(gh))
        outs = []
        for h in range(HG_HEADS):
            st = st_ref[stream, h]
            o = (_dot(sc_ref[c, h].astype(BF16), head(i_b, h))
                 + _dot_nt(head(p["q_dec"], h), st.astype(BF16)))
            st_ref[stream, h] = (st * head(chunk_decay, h)
                                 + _dot_tn(head(i_b, h), head(k_out, h)))
            outs.append(o * lax.rsqrt(jnp.mean(o * o, axis=-1, keepdims=True) + RMS_EPS))
        return (jnp.concatenate(outs, axis=1) * gate).astype(BF16)

    def project_out(g, mixed_hgrn):
        rows = group_rows[g]
        hid = alpha * x_ref[0, rows, :] + (out_attn[g] + _dot(mixed_hgrn, w_hgrn))
        mu = jnp.mean(hid, axis=-1, keepdims=True)
        cen = hid - mu
        var = jnp.mean(cen * cen, axis=-1, keepdims=True)
        y_ref[0, rows, :] = cen * lax.rsqrt(var + LN_EPS) * lg_ref[...] + lnb_ref[...]

    chunks = range(sub)
    preps = [prepare(c) for c in chunks]
    lowest = b_ref[0, C - 1:C, :]
    for c in range(1, sub):
        lowest = jnp.minimum(lowest, b_ref[c, C - 1:C, :])
    bounded = jnp.min(lowest) > -BOUNDED_DECAY_LOG2
    pl.when(bounded)(scores_from_chunk_start)
    pl.when(jnp.logical_not(bounded))(scores_by_levels)
    mixed = []
    for c in chunks:
        mixed.append(finish(c, preps[c]))
        if len(mixed) == group:
            project_out(c // group, mixed[0] if group == 1 else jnp.concatenate(mixed, axis=0))
            mixed = []

    @pl.when(t == pl.num_programs(1) - 1)
    def _():
        for s in range(streams):
            for h in range(HG_HEADS):
                so_ref[0, s, h] = st_ref[s, h].T


def _hgrn_out(hg, fh, lb_logits, norm_g, s0, layer, ma, x, w_out, ln_g, ln_b, *, C, sub, alpha,
              streams_per_step=1):
    n_streams, T, _ = fh.shape
    streams = streams_per_step
    if streams > 1:
        assert T == C and sub == streams and n_streams % streams == 0
        fold = lambda a: a.reshape(n_streams // streams, streams * T, a.shape[-1])
        hg, fh, ma, x = fold(hg), fold(fh), fold(ma), fold(x)
    B, T, _ = fh.shape
    prefix, level_of, nl, n_small = _hgrn_tables(C)
    has_s0 = s0 is not None
    rows = sub * C
    row_spec = pl.BlockSpec((1, rows, GROUP_W), lambda b, t: (b, t, 0))
    row3_spec = pl.BlockSpec((1, rows, 3 * GROUP_W), lambda b, t: (b, t, 0))
    wide_spec = pl.BlockSpec((1, rows, D_MODEL), lambda b, t: (b, t, 0))
    const = lambda shape: pl.BlockSpec(shape, lambda b, t: (0,) * len(shape))
    state_shape = (1, streams, HG_HEADS, HG_HEAD_DIM, HG_HEAD_DIM)
    in_specs = [row3_spec, row_spec, const(prefix.shape), const(level_of.shape),
                const(lb_logits.shape), const((1, GROUP_W)), row_spec, wide_spec,
                const(w_out.shape), const((1, D_MODEL)), const((1, D_MODEL))]
    args = [hg, fh, prefix, level_of, lb_logits.astype(F32),
            norm_g.reshape(1, GROUP_W).astype(F32), ma, x, w_out,
            ln_g.reshape(1, D_MODEL).astype(F32), ln_b.reshape(1, D_MODEL).astype(F32)]
    if has_s0:
        in_specs.append(pl.BlockSpec(state_shape, lambda b, t: (layer, b, 0, 0, 0)))
        args.append(s0)
    y, s_out = pl.pallas_call(
        functools.partial(_hgrn_out_kernel, C=C, sub=sub, streams=streams, nl=nl, n_small=n_small,
                          layer=layer, has_s0=has_s0, alpha=alpha),
        grid=(B, T // rows),
        in_specs=in_specs,
        out_specs=[wide_spec, pl.BlockSpec(state_shape, lambda b, t: (0, b, 0, 0, 0))],
        out_shape=[jax.ShapeDtypeStruct((B, T, D_MODEL), F32),
                   jax.ShapeDtypeStruct((1, n_streams, HG_HEADS, HG_HEAD_DIM, HG_HEAD_DIM), F32)],
        scratch_shapes=[pltpu.VMEM((streams, HG_HEADS, HG_HEAD_DIM, HG_HEAD_DIM), F32),
                        pltpu.VMEM((sub, C, GROUP_W), F32),
                        pltpu.VMEM((sub, HG_HEADS, C, C), F32)],
        compiler_params=pltpu.CompilerParams(
            dimension_semantics=("arbitrary", "arbitrary"), vmem_limit_bytes=VMEM_LIMIT),
    )(*args)
    return y.reshape(n_streams, -1, D_MODEL), s_out


PROMPT_ROWS = 512
PROMPT_CHUNK = 128
OUT_GROUP_CHUNKS = 4
PROMPT_CHUNKS_PER_STEP = 8


def _layer(x, cache_k, cache_v, state_s, layer, w_in, w_out, lb_logits, norm_g, ln_g, ln_b, alpha):
    B, T, _ = x.shape
    decode = cache_k is not None
    if decode:
        sb, k, hg, fh, k_out, v_out = _project(x, w_in, nb=B, tm=T, k_transposed=False)
        ma = _sb_decode(sb, k, cache_k, cache_v, layer)
        y, s_out = _hgrn_out(hg, fh, lb_logits, norm_g, state_s, layer, ma, x, w_out, ln_g, ln_b,
                             C=T, sub=B, alpha=alpha, streams_per_step=B)
    else:
        sb, kt, hg, fh, k_out, v_out = _project(
            x, w_in, nb=1, tm=min(PROMPT_ROWS, T), k_transposed=True)
        ma = _sb_prompt(sb, kt)
        chunk = min(PROMPT_CHUNK, T)
        y, s_out = _hgrn_out(hg, fh, lb_logits, norm_g, None, layer, ma, x, w_out, ln_g, ln_b,
                             C=chunk, sub=min(PROMPT_CHUNKS_PER_STEP, T // chunk), alpha=alpha)
    return y, k_out, v_out, s_out


def kernel(x_prompt, x_sample, cache_k, cache_v, state_s, w_in, w_out, lb_logits, hgrn_norm_g,
           ln_g, ln_b):
    depth = w_in.shape[0]
    alpha = (2 * depth) ** 0.25
    yp, ys = x_prompt, x_sample
    per_layer = []
    for l in range(depth):
        common = (l, w_in[l], w_out[l], lb_logits, hgrn_norm_g[l], ln_g[l], ln_b[l], alpha)
        yp, kp, vp, sp = _layer(yp, None, None, None, *common)
        ys, kn, vn, sn = _layer(ys, cache_k, cache_v, state_s, *common)
        per_layer.append((kp, vp, sp, kn, vn, sn))
    stack = lambda i: (per_layer[0][i] if depth == 1
                       else jnp.concatenate([p[i] for p in per_layer], axis=0))
    return (yp, ys, stack(0), stack(1), stack(2), stack(3), stack(4), stack(5))
```

```python
import functools
import math

import numpy as np
import jax
import jax.numpy as jnp
from jax import lax
from jax.experimental import pallas as pl
from jax.experimental.pallas import tpu as pltpu

F32 = jnp.float32
BF16 = jnp.bfloat16

D_MODEL = 1024
GROUP_W = 512
N_GROUPS = 8
SB_HEADS = 8
SB_HEAD_DIM = 64
HG_HEADS = 4
HG_HEAD_DIM = 128
LN_EPS = 1e-5
RMS_EPS = 1e-6

SUBLANES = 8
KEY_TILE = 256
SB_Q, SB_V, SB_GATE = 0, 1, 2
HG_Q, HG_I, HG_GATE = 0, 1, 2
BOUNDED_DECAY_LOG2 = 100.0
NEGLIGIBLE_LOG2 = float("inf")
LOG2E = 1.4426950408889634
VMEM_LIMIT = 56 * 1024 * 1024


def _sigmoid(x):
    return 1.0 / (1.0 + jnp.exp2(x * (-LOG2E)))


def _dot(a, b):
    return jnp.dot(a, b, preferred_element_type=F32)


def _dot_nt(a, b):
    return lax.dot_general(a, b, (((1,), (1,)), ((), ())), preferred_element_type=F32)


def _dot_tn(a, b):
    return lax.dot_general(a, b, (((0,), (0,)), ((), ())), preferred_element_type=F32)


def _split_bf16(x, parts):
    out = []
    for _ in range(parts - 1):
        p = x.astype(BF16)
        out.append(p)
        x = x - p.astype(F32)
    out.append(x.astype(BF16))
    return out


def _proj_kernel(x_ref, w_ref, sb_ref, k_ref, hg_ref, fh_ref, ko_ref, vo_ref, *, nb, tm,
                 k_transposed):
    m = nb * tm
    x = x_ref[...].reshape(m, D_MODEL).astype(BF16)

    def col(c):
        return _dot(x, w_ref[:, c * GROUP_W:(c + 1) * GROUP_W].astype(BF16))

    def put(ref, val, slot=0):
        ref[:, :, slot * GROUP_W:(slot + 1) * GROUP_W] = (
            val.reshape(nb, tm, GROUP_W).astype(ref.dtype))

    def put_heads(ref, val):
        if k_transposed:
            val_t = val.T
            for h in range(SB_HEADS):
                ref[0, 0, h] = val_t[h * SB_HEAD_DIM:(h + 1) * SB_HEAD_DIM, :]
            return val_t
        for b in range(nb):
            for h in range(SB_HEADS):
                ref[0, b, h] = val[b * tm:(b + 1) * tm, h * SB_HEAD_DIM:(h + 1) * SB_HEAD_DIM]

    put(sb_ref, col(0) * (SB_HEAD_DIM ** -0.5 * LOG2E), SB_Q)
    k = col(1)
    kt = put_heads(ko_ref, k)
    if k_transposed:
        kt = kt.astype(BF16)
        for j in range(m // KEY_TILE):
            k_ref[0, j] = kt[:, j * KEY_TILE:(j + 1) * KEY_TILE]
    else:
        put(k_ref, k)
    v = col(2)
    put_heads(vo_ref, v)
    put(sb_ref, v, SB_V)
    ga = col(3)
    put(sb_ref, ga * _sigmoid(ga), SB_GATE)
    put(hg_ref, col(4), HG_Q)
    put(fh_ref, col(5))
    put(hg_ref, col(6), HG_I)
    put(hg_ref, col(7), HG_GATE)


def _project(x, w, *, nb, tm, k_transposed):
    B, T, _ = x.shape
    grid = (B // nb, T // tm)
    act = lambda dt: jax.ShapeDtypeStruct((B, T, GROUP_W), dt)
    act_spec = pl.BlockSpec((nb, tm, GROUP_W), lambda b, t: (b, t, 0))
    act3 = jax.ShapeDtypeStruct((B, T, 3 * GROUP_W), BF16)
    act3_spec = pl.BlockSpec((nb, tm, 3 * GROUP_W), lambda b, t: (b, t, 0))
    if k_transposed:
        assert nb == 1 and tm % KEY_TILE == 0
        k_shape = jax.ShapeDtypeStruct((B, T // KEY_TILE, GROUP_W, KEY_TILE), BF16)
        k_spec = pl.BlockSpec((1, tm // KEY_TILE, GROUP_W, KEY_TILE), lambda b, t: (b, t, 0, 0))
        kv_shape = jax.ShapeDtypeStruct((1, B, SB_HEADS, SB_HEAD_DIM, T), F32)
        kv_spec = pl.BlockSpec((1, 1, SB_HEADS, SB_HEAD_DIM, tm), lambda b, t: (0, b, 0, 0, t))
    else:
        k_shape, k_spec = act(BF16), act_spec
        kv_shape = jax.ShapeDtypeStruct((1, B, SB_HEADS, T, SB_HEAD_DIM), F32)
        kv_spec = pl.BlockSpec((1, nb, SB_HEADS, tm, SB_HEAD_DIM), lambda b, t: (0, b, 0, t, 0))
    outs = pl.pallas_call(
        functools.partial(_proj_kernel, nb=nb, tm=tm, k_transposed=k_transposed),
        grid=grid,
        in_specs=[pl.BlockSpec((nb, tm, D_MODEL), lambda b, t: (b, t, 0)),
                  pl.BlockSpec((D_MODEL, N_GROUPS * GROUP_W), lambda b, t: (0, 0),
                               pipeline_mode=pl.Buffered(1))],
        out_specs=[act3_spec, k_spec, act3_spec, act_spec, kv_spec, kv_spec],
        out_shape=[act3, k_shape, act3, act(F32), kv_shape, kv_shape],
        compiler_params=pltpu.CompilerParams(
            dimension_semantics=("arbitrary", "arbitrary"), vmem_limit_bytes=VMEM_LIMIT),
    )(x, w)
    if k_transposed:
        outs = list(outs[:4]) + [jnp.swapaxes(o, 3, 4) for o in outs[4:]]
    return outs


def _suffix_matrix(n):
    r = lax.broadcasted_iota(jnp.int32, (n, n), 0)
    c = lax.broadcasted_iota(jnp.int32, (n, n), 1)
    return jnp.where(r >= c, 1.0, 0.0).astype(BF16)


def _sb_suffix_sums(zs, suffix, causal):
    drops = []
    for z in zs:
        drop = jnp.maximum(z, 0.0) + jnp.log2(1.0 + jnp.exp2(-jnp.abs(z)))
        if causal is not None:
            drop = jnp.where(causal, drop, 0.0)
        drops.append(drop.astype(BF16))
    return [_dot(drop, suffix) for drop in drops]


def _sb_weight(z, incl, run, causal):
    w = jnp.exp2(z - incl if run is None else z - incl - run)
    if causal is not None:
        w = jnp.where(causal, w, 0.0)
    return w.astype(BF16)


def _sb_prompt_kernel(sb_ref, kt_ref, o_ref, z_ref, w_ref, run_ref, acc_ref, state_ref, *, heads):
    tq = KEY_TILE
    nq = sb_ref.shape[1]
    suffix = _suffix_matrix(KEY_TILE)
    r = lax.broadcasted_iota(jnp.int32, (tq, KEY_TILE), 0)
    c = lax.broadcasted_iota(jnp.int32, (tq, KEY_TILE), 1)
    causal = c < r
    head_lanes = [slice(hh * SB_HEAD_DIM, (hh + 1) * SB_HEAD_DIM) for hh in range(heads)]

    def packed(slot, lanes):
        return slice(slot * GROUP_W + lanes.start, slot * GROUP_W + lanes.stop)

    def logits(bi, j, hh):
        return _dot(sb_ref[0, bi, :, packed(SB_Q, head_lanes[hh])],
                    kt_ref[0, j, head_lanes[hh], :])

    def add_weighted_values(slot, j):
        for hh in range(heads):
            acc_ref[hh] += _dot(w_ref[slot, hh], sb_ref[0, j, :, packed(SB_V, head_lanes[hh])])

    def finalize(bi):
        o = jnp.concatenate([acc_ref[hh] for hh in range(heads)], axis=1)
        gate = sb_ref[0, bi, :, SB_GATE * GROUP_W:(SB_GATE + 1) * GROUP_W].astype(F32)
        o_ref[0, pl.ds(pl.multiple_of(bi * tq, tq), tq), :] = (o * gate).astype(o_ref.dtype)

    def step(cur, masked, next_bi, next_j):
        mask = causal if masked else None
        zs = [z_ref[cur, hh] for hh in range(heads)]
        incls = _sb_suffix_sums(zs, suffix, mask)
        runs = [None if masked else run_ref[hh] for hh in range(heads)]
        for hh in range(heads):
            total = incls[hh][:, 0:1]
            run_ref[hh] = total if masked else runs[hh] + total
        if not masked:
            lowest = run_ref[0]
            for hh in range(1, heads):
                lowest = jnp.minimum(lowest, run_ref[hh])
            state_ref[2] = (jnp.min(lowest) < NEGLIGIBLE_LOG2).astype(jnp.int32)
        for hh in range(heads):
            z_ref[1 - cur, hh] = logits(next_bi, next_j, hh)
            w_ref[1 - cur, hh] = _sb_weight(zs[hh], incls[hh], runs[hh], mask)

    def refill(slot, bi, j):
        for hh in range(heads):
            z_ref[slot, hh] = logits(bi, j, hh)

    def per_slot(slot, cond, fn):
        for parity in range(2):
            pl.when(jnp.logical_and(cond, slot == parity))(functools.partial(fn, parity))

    for hh in range(heads):
        z_ref[0, hh] = logits(0, 0, hh)
    w_ref[0] = jnp.zeros_like(w_ref[0])
    acc_ref[...] = jnp.zeros_like(acc_ref)
    state_ref[0] = 0
    state_ref[1] = 0
    state_ref[2] = 0

    def block(i, carry):
        first_slot = state_ref[0]
        pending_tile = state_ref[1]
        nxt = jnp.minimum(i + 1, nq - 1)

        def enter_block(parity):
            add_weighted_values(parity, pending_tile)
            finalize(jnp.maximum(i - 1, 0))
            acc_ref[...] = jnp.zeros_like(acc_ref)

        def first_block(parity):
            enter_block(parity)
            step(parity, True, nxt, nxt)

        def later_block(parity):
            enter_block(parity)
            step(parity, True, i, i - 1)
            add_weighted_values(1 - parity, i)
            step(1 - parity, False, nxt, nxt)

        per_slot(first_slot, i == 0, first_block)
        per_slot(first_slot, i > 0, later_block)
        tiles = jnp.where(i > 0, 2, 1)
        slot = (first_slot + tiles) % 2
        last_tile = jnp.maximum(i - 1, 0)

        def weight_left():
            return state_ref[2] != 0

        go_on = jnp.logical_and(last_tile > 0, weight_left())
        per_slot(slot, go_on, lambda parity: refill(parity, i, last_tile - 1))

        def sweep(carry):
            n, _ = carry
            j = last_tile - 1 - n

            def tile(parity):
                add_weighted_values(parity, j + 1)
                same_block = j > 0
                step(parity, False, jnp.where(same_block, i, nxt),
                     jnp.where(same_block, j - 1, nxt))

            per_slot((slot + n) % 2, True, tile)
            return n + 1, jnp.logical_and(j > 0, weight_left())

        swept, _ = lax.while_loop(lambda carry: carry[1], sweep, (jnp.int32(0), go_on))
        slot = (slot + swept) % 2
        last_tile = last_tile - swept
        per_slot(slot, jnp.logical_and(swept > 0, last_tile > 0),
                 lambda parity: refill(parity, nxt, nxt))
        state_ref[0] = slot
        state_ref[1] = last_tile
        return carry

    lax.fori_loop(0, nq, block, 0)
    last_slot, last_tile = state_ref[0], state_ref[1]
    per_slot(last_slot, True, lambda parity: add_weighted_values(parity, last_tile))
    finalize(nq - 1)


def _sb_prompt(sb, kt):
    B, T, _ = sb.shape
    tq = KEY_TILE
    nkt = T // KEY_TILE
    sb4 = sb.reshape(B, nkt, KEY_TILE, 3 * GROUP_W)
    row_spec = pl.BlockSpec((1, T, GROUP_W), lambda b: (b, 0, 0))
    tile_spec = lambda shape: pl.BlockSpec((1,) + shape, lambda b: (b, 0, 0, 0))
    return pl.pallas_call(
        functools.partial(_sb_prompt_kernel, heads=SB_HEADS),
        grid=(B,),
        in_specs=[tile_spec((nkt, KEY_TILE, 3 * GROUP_W)), tile_spec((nkt, GROUP_W, KEY_TILE))],
        out_specs=row_spec,
        out_shape=jax.ShapeDtypeStruct((B, T, GROUP_W), BF16),
        scratch_shapes=[pltpu.VMEM((2, SB_HEADS, tq, KEY_TILE), F32),
                        pltpu.VMEM((2, SB_HEADS, tq, KEY_TILE), BF16),
                        pltpu.VMEM((SB_HEADS, tq, 1), F32),
                        pltpu.VMEM((SB_HEADS, tq, SB_HEAD_DIM), F32),
                        pltpu.SMEM((3,), jnp.int32)],
        compiler_params=pltpu.CompilerParams(
            dimension_semantics=("arbitrary",), vmem_limit_bytes=VMEM_LIMIT),
    )(sb4, kt)


def _sb_decode_kernel(sb_ref, kn_ref, ck_ref, cv_ref, o_ref):
    tq = sb_ref.shape[1]
    past = ck_ref.shape[4]
    suffix = _suffix_matrix(KEY_TILE)
    suffix_new = _suffix_matrix(tq)
    r = lax.broadcasted_iota(jnp.int32, (tq, tq), 0)
    c = lax.broadcasted_iota(jnp.int32, (tq, tq), 1)
    causal = c < r
    heads = range(SB_HEADS)
    head_lanes = [slice(hh * SB_HEAD_DIM, (hh + 1) * SB_HEAD_DIM) for hh in heads]
    group = lambda slot: sb_ref[0, :, slot * GROUP_W:(slot + 1) * GROUP_W]
    q_all, v_all = group(SB_Q), group(SB_V)
    qs = [q_all[:, lanes] for lanes in head_lanes]
    cache_keys = [slice(j * KEY_TILE, (j + 1) * KEY_TILE) for j in reversed(range(past // KEY_TILE))]
    zs = [[_dot_nt(q, kn_ref[0, :, lanes]) for q, lanes in zip(qs, head_lanes)]]
    zs += [[_dot(q, ck_ref[0, 0, hh, :, keys].astype(BF16)) for q, hh in zip(qs, heads)]
           for keys in cache_keys]
    masks = [causal] + [None] * len(cache_keys)
    incls = [_sb_suffix_sums(z, suffix_new if mask is not None else suffix, mask)
             for z, mask in zip(zs, masks)]
    runs = [None] * SB_HEADS
    ws = []
    for z, incl, mask in zip(zs, incls, masks):
        ws.append([_sb_weight(z[hh], incl[hh], runs[hh], mask) for hh in heads])
        runs = [incl[hh][:, 0:1] if runs[hh] is None else runs[hh] + incl[hh][:, 0:1]
                for hh in heads]
    accs = [_dot(w, v_all[:, lanes]) for w, lanes in zip(ws[0], head_lanes)]
    for w_tile, keys in zip(ws[1:], cache_keys):
        accs = [acc + _dot_nt(w, cv_ref[0, 0, hh, :, keys].astype(BF16))
                for w, hh, acc in zip(w_tile, heads, accs)]
    o = jnp.concatenate(accs, axis=1) * group(SB_GATE).astype(F32)
    o_ref[0] = o.astype(o_ref.dtype)


def _sb_decode(sb, kn, cache_k, cache_v, layer):
    B, T, _ = kn.shape
    past = cache_k.shape[3]
    assert past % KEY_TILE == 0
    row_spec = lambda width: pl.BlockSpec((1, T, width), lambda b: (b, 0, 0))
    cache_spec = pl.BlockSpec((1, 1, SB_HEADS, SB_HEAD_DIM, past), lambda b: (layer, b, 0, 0, 0))
    cache_k = jnp.swapaxes(cache_k, 3, 4)
    cache_v = jnp.swapaxes(cache_v, 3, 4)
    return pl.pallas_call(
        _sb_decode_kernel,
        grid=(B,),
        in_specs=[row_spec(3 * GROUP_W), row_spec(GROUP_W), cache_spec, cache_spec],
        out_specs=row_spec(GROUP_W),
        out_shape=jax.ShapeDtypeStruct((B, T, GROUP_W), BF16),
        compiler_params=pltpu.CompilerParams(
            dimension_semantics=("arbitrary",), vmem_limit_bytes=VMEM_LIMIT),
    )(sb, kn, cache_k, cache_v)


def _hgrn_tables(C):
    nl = int(math.log2(C))
    assert 1 << nl == C
    n_small = min(nl, int(math.log2(SUBLANES)))
    t = np.arange(C)[:, None]
    j = np.arange(C)[None, :]
    blocks = []
    for l in range(n_small):
        half = 1 << l
        mid = (t >> (l + 1) << (l + 1)) + half
        right = (t & half) != 0
        blocks.append(np.where(right, (j >= mid) & (j <= t), (j > t) & (j < mid)))
    blocks.append(j <= t)
    prefix = np.concatenate(blocks, axis=0).astype(np.float32)
    x = t ^ j
    msb = np.floor(np.log2(np.maximum(x, 1))).astype(np.int32)
    level_of = np.where(t > j, msb, np.where(t == j, nl, -1)).astype(np.int32)
    return jnp.asarray(prefix, BF16), jnp.asarray(level_of), nl, n_small


def _hgrn_out_kernel(*refs, C, sub, streams, nl, n_small, layer, has_s0, alpha):
    (hg_ref, fh_ref, pm_ref, lv_ref, lbl_ref, ng_ref, ma_ref, x_ref, wo_ref,
     lg_ref, lnb_ref) = refs[:11]
    if has_s0:
        s0_ref, y_ref, so_ref, st_ref, b_ref, sc_ref = refs[11:]
    else:
        y_ref, so_ref, st_ref, b_ref, sc_ref = refs[11:]
    t = pl.program_id(1)

    def packed(rows, slot):
        return hg_ref[0, rows, slot * GROUP_W:(slot + 1) * GROUP_W]

    group = sub if streams > 1 else min(sub, OUT_GROUP_CHUNKS)
    group_rows = [slice(g * group * C, (g + 1) * group * C) for g in range(sub // group)]
    w_attn = wo_ref[:GROUP_W, :].astype(BF16)
    w_hgrn = wo_ref[GROUP_W:, :].astype(BF16)
    out_attn = [_dot(ma_ref[0, rows, :], w_attn) for rows in group_rows]

    @pl.when(t == 0)
    def _():
        if has_s0:
            for s in range(streams):
                for h in range(HG_HEADS):
                    st_ref[s, h] = s0_ref[0, s, h].T
        else:
            st_ref[...] = jnp.zeros_like(st_ref)

    logits = lbl_ref[...]
    e = jnp.exp(logits - jnp.max(logits, axis=0, keepdims=True))
    lb = jnp.sum(e[:layer + 1], axis=0, keepdims=True) / jnp.sum(e, axis=0, keepdims=True)

    def head(a, h):
        return a[:, h * HG_HEAD_DIM:(h + 1) * HG_HEAD_DIM]

    def prepare(c):
        rows = slice(c * C, (c + 1) * C)
        f = lb + (1.0 - lb) * _sigmoid(fh_ref[0, rows, :])
        g = jnp.log2(f)
        kin = 1.0 - f
        qh = packed(rows, HG_Q).astype(F32)
        qs = qh * _sigmoid(qh)
        g_hi, g_lo = _split_bf16(g, 2)
        cum = _dot(pm_ref[n_small * C:, :], jnp.concatenate([g_hi, g_lo], axis=1))
        b = cum[:, :GROUP_W] + cum[:, GROUP_W:]
        b_ref[c] = b
        q_dec = (qs * jnp.exp2(b)).astype(BF16)
        return dict(rows=rows, qs=qs, kin=kin, g_hi=g_hi, q_dec=q_dec)

    def level_log(c, pre, l):
        if l < n_small:
            return pre[l * C:(l + 1) * C]
        half = 1 << l
        parts = []
        for lo in range(0, C, 2 * half):
            mid = lo + half
            last_left = b_ref[c, mid - 1:mid, :]
            parts.append(last_left - b_ref[c, lo:mid, :])
            parts.append(b_ref[c, mid:mid + half, :] - last_left)
        return jnp.concatenate(parts, axis=0)

    def operands(c, p, l):
        if l == nl:
            return p["qs_b"], p["kin_b"]
        e_l = jnp.exp2(level_log(c, p["pre"], l)).astype(BF16)
        return p["qs_b"] * e_l, p["kin_b"] * e_l

    def scores_by_levels():
        ops = [dict(pre=_dot(pm_ref[:n_small * C, :], p["g_hi"]),
                    qs_b=p["qs"].astype(BF16), kin_b=p["kin"].astype(BF16)) for p in preps]
        level_of = lv_ref[...]
        order = [(l, c) for l in [nl] + list(range(nl)) for c in chunks]
        scores = [[0.0] * HG_HEADS for _ in chunks]
        nxt = operands(order[0][1], ops[order[0][1]], order[0][0])
        for idx, (l, c) in enumerate(order):
            q_l, k_l = nxt
            if idx + 1 < len(order):
                l_n, c_n = order[idx + 1]
                nxt = operands(c_n, ops[c_n], l_n)
            hit = level_of == l
            for h in range(HG_HEADS):
                scores[c][h] = jnp.where(hit, _dot_nt(head(q_l, h), head(k_l, h)), scores[c][h])
        for c in chunks:
            for h in range(HG_HEADS):
                sc_ref[c, h] = scores[c][h]

    def scores_from_chunk_start():
        on_or_below = lv_ref[...] >= 0
        for c, p in enumerate(preps):
            k_grow = (p["kin"] * jnp.exp2(-b_ref[c])).astype(BF16)
            for h in range(HG_HEADS):
                sc_ref[c, h] = jnp.where(
                    on_or_below, _dot_nt(head(p["q_dec"], h), head(k_grow, h)), 0.0)

    def finish(c, p):
        rows = p["rows"]
        stream = c if streams > 1 else 0
        b = b_ref[c]
        b_last = b_ref[c, C - 1:C, :]
        k_out = (p["kin"] * jnp.exp2(b_last - b)).astype(BF16)
        chunk_decay = jnp.exp2(b_last)
        i_b = packed(rows, HG_I)
        gh = packed(rows, HG_GATE).astype(F32)
        gate = ng_ref[...] * (gh * _sigmoid(gh))
        outs = []
        for h in range(HG_HEADS):
            st = st_ref[stream, h]
            o = (_dot(sc_ref[c, h].astype(BF16), head(i_b, h))
                 + _dot_nt(head(p["q_dec"], h), st.astype(BF16)))
            st_ref[stream, h] = (st * head(chunk_decay, h)
                                 + _dot_tn(head(i_b, h), head(k_out, h)))
            outs.append(o * lax.rsqrt(jnp.mean(o * o, axis=-1, keepdims=True) + RMS_EPS))
        return (jnp.concatenate(outs, axis=1) * gate).astype(BF16)

    def project_out(g, mixed_hgrn):
        rows = group_rows[g]
        hid = alpha * x_ref[0, rows, :] + (out_attn[g] + _dot(mixed_hgrn, w_hgrn))
        mu = jnp.mean(hid, axis=-1, keepdims=True)
        cen = hid - mu
        var = jnp.mean(cen * cen, axis=-1, keepdims=True)
        y_ref[0, rows, :] = cen * lax.rsqrt(var + LN_EPS) * lg_ref[...] + lnb_ref[...]

    chunks = range(sub)
    preps = [prepare(c) for c in chunks]
    lowest = b_ref[0, C - 1:C, :]
    for c in range(1, sub):
        lowest = jnp.minimum(lowest, b_ref[c, C - 1:C, :])
    bounded = jnp.min(lowest) > -BOUNDED_DECAY_LOG2
    pl.when(bounded)(scores_from_chunk_start)
    pl.when(jnp.logical_not(bounded))(scores_by_levels)
    mixed = []
    for c in chunks:
        mixed.append(finish(c, preps[c]))
        if len(mixed) == group:
            project_out(c // group, mixed[0] if group == 1 else jnp.concatenate(mixed, axis=0))
            mixed = []

    @pl.when(t == pl.num_programs(1) - 1)
    def _():
        for s in range(streams):
            for h in range(HG_HEADS):
                so_ref[0, s, h] = st_ref[s, h].T


def _hgrn_out(hg, fh, lb_logits, norm_g, s0, layer, ma, x, w_out, ln_g, ln_b, *, C, sub, alpha,
              streams_per_step=1):
    n_streams, T, _ = fh.shape
    streams = streams_per_step
    if streams > 1:
        assert T == C and sub == streams and n_streams % streams == 0
        fold = lambda a: a.reshape(n_streams // streams, streams * T, a.shape[-1])
        hg, fh, ma, x = fold(hg), fold(fh), fold(ma), fold(x)
    B, T, _ = fh.shape
    prefix, level_of, nl, n_small = _hgrn_tables(C)
    has_s0 = s0 is not None
    rows = sub * C
    row_spec = pl.BlockSpec((1, rows, GROUP_W), lambda b, t: (b, t, 0))
    row3_spec = pl.BlockSpec((1, rows, 3 * GROUP_W), lambda b, t: (b, t, 0))
    wide_spec = pl.BlockSpec((1, rows, D_MODEL), lambda b, t: (b, t, 0))
    const = lambda shape: pl.BlockSpec(shape, lambda b, t: (0,) * len(shape))
    state_shape = (1, streams, HG_HEADS, HG_HEAD_DIM, HG_HEAD_DIM)
    in_specs = [row3_spec, row_spec, const(prefix.shape), const(level_of.shape),
                const(lb_logits.shape), const((1, GROUP_W)), row_spec, wide_spec,
                const(w_out.shape), const((1, D_MODEL)), const((1, D_MODEL))]
    args = [hg, fh, prefix, level_of, lb_logits.astype(F32),
            norm_g.reshape(1, GROUP_W).astype(F32), ma, x, w_out,
            ln_g.reshape(1, D_MODEL).astype(F32), ln_b.reshape(1, D_MODEL).astype(F32)]
    if has_s0:
        in_specs.append(pl.BlockSpec(state_shape, lambda b, t: (layer, b, 0, 0, 0)))
        args.append(s0)
    y, s_out = pl.pallas_call(
        functools.partial(_hgrn_out_kernel, C=C, sub=sub, streams=streams, nl=nl, n_small=n_small,
                          layer=layer, has_s0=has_s0, alpha=alpha),
        grid=(B, T // rows),
        in_specs=in_specs,
        out_specs=[wide_spec, pl.BlockSpec(state_shape, lambda b, t: (0, b, 0, 0, 0))],
        out_shape=[jax.ShapeDtypeStruct((B, T, D_MODEL), F32),
                   jax.ShapeDtypeStruct((1, n_streams, HG_HEADS, HG_HEAD_DIM, HG_HEAD_DIM), F32)],
        scratch_shapes=[pltpu.VMEM((streams, HG_HEADS, HG_HEAD_DIM, HG_HEAD_DIM), F32),
                        pltpu.VMEM((sub, C, GROUP_W), F32),
                        pltpu.VMEM((sub, HG_HEADS, C, C), F32)],
        compiler_params=pltpu.CompilerParams(
            dimension_semantics=("arbitrary", "arbitrary"), vmem_limit_bytes=VMEM_LIMIT),
    )(*args)
    return y.reshape(n_streams, -1, D_MODEL), s_out


PROMPT_ROWS = 512
PROMPT_CHUNK = 128
OUT_GROUP_CHUNKS = 4
PROMPT_CHUNKS_PER_STEP = 8


def _layer(x, cache_k, cache_v, state_s, layer, w_in, w_out, lb_logits, norm_g, ln_g, ln_b, alpha):
    B, T, _ = x.shape
    decode = cache_k is not None
    if decode:
        sb, k, hg, fh, k_out, v_out = _project(x, w_in, nb=B, tm=T, k_transposed=False)
        ma = _sb_decode(sb, k, cache_k, cache_v, layer)
        y, s_out = _hgrn_out(hg, fh, lb_logits, norm_g, state_s, layer, ma, x, w_out, ln_g, ln_b,
                             C=T, sub=B, alpha=alpha, streams_per_step=B)
    else:
        sb, kt, hg, fh, k_out, v_out = _project(
            x, w_in, nb=1, tm=min(PROMPT_ROWS, T), k_transposed=True)
        ma = _sb_prompt(sb, kt)
        chunk = min(PROMPT_CHUNK, T)
        y, s_out = _hgrn_out(hg, fh, lb_logits, norm_g, None, layer, ma, x, w_out, ln_g, ln_b,
                             C=chunk, sub=min(PROMPT_CHUNKS_PER_STEP, T // chunk), alpha=alpha)
    return y, k_out, v_out, s_out


def kernel(x_prompt, x_sample, cache_k, cache_v, state_s, w_in, w_out, lb_logits, hgrn_norm_g,
           ln_g, ln_b):
    depth = w_in.shape[0]
    alpha = (2 * depth) ** 0.25
    yp, ys = x_prompt, x_sample
    per_layer = []
    for l in range(depth):
        common = (l, w_in[l], w_out[l], lb_logits, hgrn_norm_g[l], ln_g[l], ln_b[l], alpha)
        yp, kp, vp, sp = _layer(yp, None, None, None, *common)
        ys, kn, vn, sn = _layer(ys, cache_k, cache_v, state_s, *common)
        per_layer.append((kp, vp, sp, kn, vn, sn))
    stack = lambda i: (per_layer[0][i] if depth == 1
                       else jnp.concatenate([p[i] for p in per_layer], axis=0))
    return (yp, ys, stack(0), stack(1), stack(2), stack(3), stack(4), stack(5))
```

```python
import functools
import math

import numpy as np
import jax
import jax.numpy as jnp
from jax import lax
from jax.experimental import pallas as pl
from jax.experimental.pallas import tpu as pltpu

F32 = jnp.float32
BF16 = jnp.bfloat16

D_MODEL = 1024
GROUP_W = 512
N_GROUPS = 8
SB_HEADS = 8
SB_HEAD_DIM = 64
HG_HEADS = 4
HG_HEAD_DIM = 128
LN_EPS = 1e-5
RMS_EPS = 1e-6

SUBLANES = 8
KEY_TILE = 256
SB_Q, SB_V, SB_GATE = 0, 1, 2
HG_Q, HG_I, HG_GATE = 0, 1, 2
BOUNDED_DECAY_LOG2 = 100.0
NEGLIGIBLE_LOG2 = 200.0
LOG2E = 1.4426950408889634
VMEM_LIMIT = 56 * 1024 * 1024


def _sigmoid(x):
    return 1.0 / (1.0 + jnp.exp2(x * (-LOG2E)))


def _dot(a, b):
    return jnp.dot(a, b, preferred_element_type=F32)


def _dot_nt(a, b):
    return lax.dot_general(a, b, (((1,), (1,)), ((), ())), preferred_element_type=F32)


def _dot_tn(a, b):
    return lax.dot_general(a, b, (((0,), (0,)), ((), ())), preferred_element_type=F32)


def _split_bf16(x, parts):
    out = []
    for _ in range(parts - 1):
        p = x.astype(BF16)
        out.append(p)
        x = x - p.astype(F32)
    out.append(x.astype(BF16))
    return out


def _proj_hgrn_kernel(x_ref, w_ref, hg_ref, fh_ref):
    x = x_ref[0].astype(BF16)

    def col(c):
        return _dot(x, w_ref[:, c * GROUP_W:(c + 1) * GROUP_W].astype(BF16))

    for slot, c in ((HG_Q, 0), (HG_I, 2), (HG_GATE, 3)):
        hg_ref[0, :, slot * GROUP_W:(slot + 1) * GROUP_W] = col(c).astype(hg_ref.dtype)
    fh_ref[0] = col(1)


def _project_hgrn(x, w, *, tm):
    B, T, _ = x.shape
    return pl.pallas_call(
        _proj_hgrn_kernel,
        grid=(B, T // tm),
        in_specs=[pl.BlockSpec((1, tm, D_MODEL), lambda b, t: (b, t, 0)),
                  pl.BlockSpec((D_MODEL, 4 * GROUP_W), lambda b, t: (0, 1),
                               pipeline_mode=pl.Buffered(1))],
        out_specs=[pl.BlockSpec((1, tm, 3 * GROUP_W), lambda b, t: (b, t, 0)),
                   pl.BlockSpec((1, tm, GROUP_W), lambda b, t: (b, t, 0))],
        out_shape=[jax.ShapeDtypeStruct((B, T, 3 * GROUP_W), BF16),
                   jax.ShapeDtypeStruct((B, T, GROUP_W), F32)],
        compiler_params=pltpu.CompilerParams(
            dimension_semantics=("arbitrary", "arbitrary"), vmem_limit_bytes=VMEM_LIMIT),
    )(x, w)


def _proj_kernel(x_ref, w_ref, sb_ref, k_ref, hg_ref, fh_ref, ko_ref, vo_ref, *, nb, tm,
                 k_transposed):
    m = nb * tm
    x = x_ref[...].reshape(m, D_MODEL).astype(BF16)

    def col(c):
        return _dot(x, w_ref[:, c * GROUP_W:(c + 1) * GROUP_W].astype(BF16))

    def put(ref, val, slot=0):
        ref[:, :, slot * GROUP_W:(slot + 1) * GROUP_W] = (
            val.reshape(nb, tm, GROUP_W).astype(ref.dtype))

    def put_heads(ref, val):
        if k_transposed:
            val_t = val.T
            for h in range(SB_HEADS):
                ref[0, 0, h] = val_t[h * SB_HEAD_DIM:(h + 1) * SB_HEAD_DIM, :]
            return val_t
        for b in range(nb):
            for h in range(SB_HEADS):
                ref[0, b, h] = val[b * tm:(b + 1) * tm, h * SB_HEAD_DIM:(h + 1) * SB_HEAD_DIM]

    put(sb_ref, col(0) * (SB_HEAD_DIM ** -0.5 * LOG2E), SB_Q)
    k = col(1)
    kt = put_heads(ko_ref, k)
    if k_transposed:
        kt = kt.astype(BF16)
        for j in range(m // KEY_TILE):
            k_ref[0, j] = kt[:, j * KEY_TILE:(j + 1) * KEY_TILE]
    else:
        put(k_ref, k)
    v = col(2)
    put_heads(vo_ref, v)
    put(sb_ref, v, SB_V)
    ga = col(3)
    put(sb_ref, ga * _sigmoid(ga), SB_GATE)
    put(hg_ref, col(4), HG_Q)
    put(fh_ref, col(5))
    put(hg_ref, col(6), HG_I)
    put(hg_ref, col(7), HG_GATE)


def _project(x, w, *, nb, tm, k_transposed):
    B, T, _ = x.shape
    grid = (B // nb, T // tm)
    act = lambda dt: jax.ShapeDtypeStruct((B, T, GROUP_W), dt)
    act_spec = pl.BlockSpec((nb, tm, GROUP_W), lambda b, t: (b, t, 0))
    act3 = jax.ShapeDtypeStruct((B, T, 3 * GROUP_W), BF16)
    act3_spec = pl.BlockSpec((nb, tm, 3 * GROUP_W), lambda b, t: (b, t, 0))
    if k_transposed:
        assert nb == 1 and tm % KEY_TILE == 0
        k_shape = jax.ShapeDtypeStruct((B, T // KEY_TILE, GROUP_W, KEY_TILE), BF16)
        k_spec = pl.BlockSpec((1, tm // KEY_TILE, GROUP_W, KEY_TILE), lambda b, t: (b, t, 0, 0))
        kv_shape = jax.ShapeDtypeStruct((1, B, SB_HEADS, SB_HEAD_DIM, T), F32)
        kv_spec = pl.BlockSpec((1, 1, SB_HEADS, SB_HEAD_DIM, tm), lambda b, t: (0, b, 0, 0, t))
    else:
        k_shape, k_spec = act(BF16), act_spec
        kv_shape = jax.ShapeDtypeStruct((1, B, SB_HEADS, T, SB_HEAD_DIM), F32)
        kv_spec = pl.BlockSpec((1, nb, SB_HEADS, tm, SB_HEAD_DIM), lambda b, t: (0, b, 0, t, 0))
    outs = pl.pallas_call(
        functools.partial(_proj_kernel, nb=nb, tm=tm, k_transposed=k_transposed),
        grid=grid,
        in_specs=[pl.BlockSpec((nb, tm, D_MODEL), lambda b, t: (b, t, 0)),
                  pl.BlockSpec((D_MODEL, N_GROUPS * GROUP_W), lambda b, t: (0, 0),
                               pipeline_mode=pl.Buffered(1))],
        out_specs=[act3_spec, k_spec, act3_spec, act_spec, kv_spec, kv_spec],
        out_shape=[act3, k_shape, act3, act(F32), kv_shape, kv_shape],
        compiler_params=pltpu.CompilerParams(
            dimension_semantics=("arbitrary", "arbitrary"), vmem_limit_bytes=VMEM_LIMIT),
    )(x, w)
    if k_transposed:
        outs = list(outs[:4]) + [jnp.swapaxes(o, 3, 4) for o in outs[4:]]
    return outs


def _suffix_matrix(n):
    r = lax.broadcasted_iota(jnp.int32, (n, n), 0)
    c = lax.broadcasted_iota(jnp.int32, (n, n), 1)
    return jnp.where(r >= c, 1.0, 0.0).astype(BF16)


def _sb_suffix_sums(zs, suffix, causal):
    drops = []
    for z in zs:
        drop = jnp.maximum(z, 0.0) + jnp.log2(1.0 + jnp.exp2(-jnp.abs(z)))
        if causal is not None:
            drop = jnp.where(causal, drop, 0.0)
        drops.append(drop.astype(BF16))
    return [_dot(drop, suffix) for drop in drops]


def _sb_weight(z, incl, run, causal):
    w = jnp.exp2(z - incl if run is None else z - incl - run)
    if causal is not None:
        w = jnp.where(causal, w, 0.0)
    return w.astype(BF16)


def _sb_prompt_kernel(x_ref, win_ref, o_ref, ko_ref, vo_ref, sb_ref, kt_ref, z_ref, w_ref, run_ref,
                      acc_ref, state_ref, *, heads):
    tq = KEY_TILE
    nq = sb_ref.shape[0]
    s = pl.program_id(1)
    blk = jnp.minimum(s, nq - 1)
    suffix = _suffix_matrix(KEY_TILE)
    r = lax.broadcasted_iota(jnp.int32, (tq, KEY_TILE), 0)
    c = lax.broadcasted_iota(jnp.int32, (tq, KEY_TILE), 1)
    causal = c < r
    head_lanes = [slice(hh * SB_HEAD_DIM, (hh + 1) * SB_HEAD_DIM) for hh in range(heads)]

    def packed(slot, lanes):
        return slice(slot * GROUP_W + lanes.start, slot * GROUP_W + lanes.stop)

    x_block = x_ref[0].astype(BF16)

    def column_group(c):
        return _dot(x_block, win_ref[:, c * GROUP_W:(c + 1) * GROUP_W].astype(BF16))

    def put_heads_transposed(ref, val):
        val_t = val.T
        for h in range(SB_HEADS):
            ref[0, 0, h] = val_t[h * SB_HEAD_DIM:(h + 1) * SB_HEAD_DIM, :]
        return val_t

    def project_q():
        sb_ref[blk, :, SB_Q * GROUP_W:(SB_Q + 1) * GROUP_W] = (
            column_group(0) * (SB_HEAD_DIM ** -0.5 * LOG2E)).astype(BF16)

    def project_k():
        kt_ref[blk] = put_heads_transposed(ko_ref, column_group(1)).astype(BF16)

    def project_v():
        v = column_group(2)
        put_heads_transposed(vo_ref, v)
        sb_ref[blk, :, SB_V * GROUP_W:(SB_V + 1) * GROUP_W] = v.astype(BF16)

    def project_gate():
        ga = column_group(3)
        sb_ref[blk, :, SB_GATE * GROUP_W:(SB_GATE + 1) * GROUP_W] = (ga * _sigmoid(ga)).astype(BF16)

    def logits(bi, j, hh):
        return _dot(sb_ref[bi, :, packed(SB_Q, head_lanes[hh])], kt_ref[j, head_lanes[hh], :])

    def add_weighted_values(slot, j):
        for hh in range(heads):
            acc_ref[hh] += _dot(w_ref[slot, hh], sb_ref[j, :, packed(SB_V, head_lanes[hh])])

    def finalize(bi):
        o = jnp.concatenate([acc_ref[hh] for hh in range(heads)], axis=1)
        gate = sb_ref[bi, :, SB_GATE * GROUP_W:(SB_GATE + 1) * GROUP_W].astype(F32)
        o_ref[0, pl.ds(pl.multiple_of(bi * tq, tq), tq), :] = (o * gate).astype(o_ref.dtype)

    def step(cur, masked, next_bi, next_j, fill=(None, None)):
        mask = causal if masked else None
        zs = [z_ref[cur, hh] for hh in range(heads)]
        if fill[0] is not None:
            fill[0]()
        incls = _sb_suffix_sums(zs, suffix, mask)
        runs = [None if masked else run_ref[hh] for hh in range(heads)]
        for hh in range(heads):
            total = incls[hh][:, 0:1]
            run_ref[hh] = total if masked else runs[hh] + total
        if not masked:
            lowest = run_ref[0]
            for hh in range(1, heads):
                lowest = jnp.minimum(lowest, run_ref[hh])
            state_ref[2] = (jnp.min(lowest) < NEGLIGIBLE_LOG2).astype(jnp.int32)
        if fill[1] is not None:
            fill[1]()
        for hh in range(heads):
            z_ref[1 - cur, hh] = logits(next_bi, next_j, hh)
            w_ref[1 - cur, hh] = _sb_weight(zs[hh], incls[hh], runs[hh], mask)

    def refill(slot, bi, j):
        for hh in range(heads):
            z_ref[slot, hh] = logits(bi, j, hh)

    def per_slot(slot, cond, fn):
        for parity in range(2):
            pl.when(jnp.logical_and(cond, slot == parity))(functools.partial(fn, parity))

    @pl.when(s == 0)
    def _():
        project_q()
        project_k()
        project_v()
        project_gate()

    @pl.when(s == 1)
    def _():
        for hh in range(heads):
            z_ref[0, hh] = logits(0, 0, hh)
        w_ref[0] = jnp.zeros_like(w_ref[0])
        acc_ref[...] = jnp.zeros_like(acc_ref)
        state_ref[0] = 0
        state_ref[1] = 0
        state_ref[2] = 0

    def block(i):
        first_slot = state_ref[0]
        pending_tile = state_ref[1]
        nxt = jnp.minimum(i + 1, nq - 1)

        def enter_block(parity):
            add_weighted_values(parity, pending_tile)
            finalize(jnp.maximum(i - 1, 0))
            acc_ref[...] = jnp.zeros_like(acc_ref)

        def first_block(parity):
            enter_block(parity)
            project_q()
            project_k()
            project_v()
            project_gate()
            step(parity, True, nxt, nxt)

        def later_block(parity):
            enter_block(parity)
            step(parity, True, i, i - 1)
            add_weighted_values(1 - parity, i)
            project_q()
            project_k()
            project_v()
            project_gate()
            step(1 - parity, False, nxt, nxt)

        per_slot(first_slot, i == 0, first_block)
        per_slot(first_slot, i > 0, later_block)
        tiles = jnp.where(i > 0, 2, 1)
        slot = (first_slot + tiles) % 2
        last_tile = jnp.maximum(i - 1, 0)

        def weight_left():
            return state_ref[2] != 0

        go_on = jnp.logical_and(last_tile > 0, weight_left())
        per_slot(slot, go_on, lambda parity: refill(parity, i, last_tile - 1))

        def sweep(carry):
            n, _ = carry
            j = last_tile - 1 - n

            def tile(parity):
                add_weighted_values(parity, j + 1)
                same_block = j > 0
                step(parity, False, jnp.where(same_block, i, nxt),
                     jnp.where(same_block, j - 1, nxt))

            per_slot((slot + n) % 2, True, tile)
            return n + 1, jnp.logical_and(j > 0, weight_left())

        swept, _ = lax.while_loop(lambda carry: carry[1], sweep, (jnp.int32(0), go_on))
        slot = (slot + swept) % 2
        last_tile = last_tile - swept
        per_slot(slot, jnp.logical_and(swept > 0, last_tile > 0),
                 lambda parity: refill(parity, nxt, nxt))
        state_ref[0] = slot
        state_ref[1] = last_tile

    pl.when(s >= 1)(lambda: block(s - 1))

    @pl.when(s == nq)
    def _():
        last_slot, last_tile = state_ref[0], state_ref[1]
        per_slot(last_slot, True, lambda parity: add_weighted_values(parity, last_tile))
        finalize(nq - 1)


def _sb_prompt(x, w):
    B, T, _ = x.shape
    tq = KEY_TILE
    nq = T // tq
    last = nq - 1
    row_spec = pl.BlockSpec((1, T, GROUP_W), lambda b, s: (b, 0, 0))
    kv_shape = jax.ShapeDtypeStruct((1, B, SB_HEADS, SB_HEAD_DIM, T), F32)
    kv_spec = pl.BlockSpec((1, 1, SB_HEADS, SB_HEAD_DIM, tq),
                           lambda b, s: (0, b, 0, 0, jnp.minimum(s, last)))
    o, k_out, v_out = pl.pallas_call(
        functools.partial(_sb_prompt_kernel, heads=SB_HEADS),
        grid=(B, nq + 1),
        in_specs=[pl.BlockSpec((1, tq, D_MODEL), lambda b, s: (b, jnp.minimum(s, last), 0)),
                  pl.BlockSpec((D_MODEL, 4 * GROUP_W), lambda b, s: (0, 0),
                               pipeline_mode=pl.Buffered(1))],
        out_specs=[row_spec, kv_spec, kv_spec],
        out_shape=[jax.ShapeDtypeStruct((B, T, GROUP_W), BF16), kv_shape, kv_shape],
        scratch_shapes=[pltpu.VMEM((nq, tq, 3 * GROUP_W), BF16),
                        pltpu.VMEM((nq, GROUP_W, KEY_TILE), BF16),
                        pltpu.VMEM((2, SB_HEADS, tq, KEY_TILE), F32),
                        pltpu.VMEM((2, SB_HEADS, tq, KEY_TILE), BF16),
                        pltpu.VMEM((SB_HEADS, tq, 1), F32),
                        pltpu.VMEM((SB_HEADS, tq, SB_HEAD_DIM), F32),
                        pltpu.SMEM((3,), jnp.int32)],
        compiler_params=pltpu.CompilerParams(
            dimension_semantics=("arbitrary", "arbitrary"), vmem_limit_bytes=VMEM_LIMIT),
    )(x, w)
    return o, jnp.swapaxes(k_out, 3, 4), jnp.swapaxes(v_out, 3, 4)


def _sb_decode_kernel(sb_ref, kn_ref, ck_ref, cv_ref, o_ref):
    tq = sb_ref.shape[1]
    past = ck_ref.shape[4]
    suffix = _suffix_matrix(KEY_TILE)
    suffix_new = _suffix_matrix(tq)
    r = lax.broadcasted_iota(jnp.int32, (tq, tq), 0)
    c = lax.broadcasted_iota(jnp.int32, (tq, tq), 1)
    causal = c < r
    heads = range(SB_HEADS)
    head_lanes = [slice(hh * SB_HEAD_DIM, (hh + 1) * SB_HEAD_DIM) for hh in heads]
    group = lambda slot: sb_ref[0, :, slot * GROUP_W:(slot + 1) * GROUP_W]
    q_all, v_all = group(SB_Q), group(SB_V)
    qs = [q_all[:, lanes] for lanes in head_lanes]
    cache_keys = [slice(j * KEY_TILE, (j + 1) * KEY_TILE) for j in reversed(range(past // KEY_TILE))]
    zs = [[_dot_nt(q, kn_ref[0, :, lanes]) for q, lanes in zip(qs, head_lanes)]]
    zs += [[_dot(q, ck_ref[0, 0, hh, :, keys].astype(BF16)) for q, hh in zip(qs, heads)]
           for keys in cache_keys]
    masks = [causal] + [None] * len(cache_keys)
    incls = [_sb_suffix_sums(z, suffix_new if mask is not None else suffix, mask)
             for z, mask in zip(zs, masks)]
    runs = [None] * SB_HEADS
    ws = []
    for z, incl, mask in zip(zs, incls, masks):
        ws.append([_sb_weight(z[hh], incl[hh], runs[hh], mask) for hh in heads])
        runs = [incl[hh][:, 0:1] if runs[hh] is None else runs[hh] + incl[hh][:, 0:1]
                for hh in heads]
    accs = [_dot(w, v_all[:, lanes]) for w, lanes in zip(ws[0], head_lanes)]
    for w_tile, keys in zip(ws[1:], cache_keys):
        accs = [acc + _dot_nt(w, cv_ref[0, 0, hh, :, keys].astype(BF16))
                for w, hh, acc in zip(w_tile, heads, accs)]
    o = jnp.concatenate(accs, axis=1) * group(SB_GATE).astype(F32)
    o_ref[0] = o.astype(o_ref.dtype)


def _sb_decode(sb, kn, cache_k, cache_v, layer):
    B, T, _ = kn.shape
    past = cache_k.shape[3]
    assert past % KEY_TILE == 0
    row_spec = lambda width: pl.BlockSpec((1, T, width), lambda b: (b, 0, 0))
    cache_spec = pl.BlockSpec((1, 1, SB_HEADS, SB_HEAD_DIM, past), lambda b: (layer, b, 0, 0, 0))
    cache_k = jnp.swapaxes(cache_k, 3, 4)
    cache_v = jnp.swapaxes(cache_v, 3, 4)
    return pl.pallas_call(
        _sb_decode_kernel,
        grid=(B,),
        in_specs=[row_spec(3 * GROUP_W), row_spec(GROUP_W), cache_spec, cache_spec],
        out_specs=row_spec(GROUP_W),
        out_shape=jax.ShapeDtypeStruct((B, T, GROUP_W), BF16),
        compiler_params=pltpu.CompilerParams(
            dimension_semantics=("arbitrary",), vmem_limit_bytes=VMEM_LIMIT),
    )(sb, kn, cache_k, cache_v)


def _hgrn_tables(C):
    nl = int(math.log2(C))
    assert 1 << nl == C
    n_small = min(nl, int(math.log2(SUBLANES)))
    t = np.arange(C)[:, None]
    j = np.arange(C)[None, :]
    blocks = []
    for l in range(n_small):
        half = 1 << l
        mid = (t >> (l + 1) << (l + 1)) + half
        right = (t & half) != 0
        blocks.append(np.where(right, (j >= mid) & (j <= t), (j > t) & (j < mid)))
    blocks.append(j <= t)
    prefix = np.concatenate(blocks, axis=0).astype(np.float32)
    x = t ^ j
    msb = np.floor(np.log2(np.maximum(x, 1))).astype(np.int32)
    level_of = np.where(t > j, msb, np.where(t == j, nl, -1)).astype(np.int32)
    return jnp.asarray(prefix, BF16), jnp.asarray(level_of), nl, n_small


def _hgrn_out_kernel(*refs, C, sub, streams, nl, n_small, layer, has_s0, alpha):
    (hg_ref, fh_ref, pm_ref, lv_ref, lbl_ref, ng_ref, ma_ref, x_ref, wo_ref,
     lg_ref, lnb_ref) = refs[:11]
    if has_s0:
        s0_ref, y_ref, so_ref, st_ref, b_ref, sc_ref = refs[11:]
    else:
        y_ref, so_ref, st_ref, b_ref, sc_ref = refs[11:]
    t = pl.program_id(1)

    def packed(rows, slot):
        return hg_ref[0, rows, slot * GROUP_W:(slot + 1) * GROUP_W]

    group = sub if streams > 1 else min(sub, OUT_GROUP_CHUNKS)
    group_rows = [slice(g * group * C, (g + 1) * group * C) for g in range(sub // group)]
    w_attn = wo_ref[:GROUP_W, :].astype(BF16)
    w_hgrn = wo_ref[GROUP_W:, :].astype(BF16)
    out_attn = [_dot(ma_ref[0, rows, :], w_attn) for rows in group_rows]

    @pl.when(t == 0)
    def _():
        if has_s0:
            for s in range(streams):
                for h in range(HG_HEADS):
                    st_ref[s, h] = s0_ref[0, s, h].T
        else:
            st_ref[...] = jnp.zeros_like(st_ref)

    logits = lbl_ref[...]
    e = jnp.exp(logits - jnp.max(logits, axis=0, keepdims=True))
    lb = jnp.sum(e[:layer + 1], axis=0, keepdims=True) / jnp.sum(e, axis=0, keepdims=True)

    def head(a, h):
        return a[:, h * HG_HEAD_DIM:(h + 1) * HG_HEAD_DIM]

    def prepare(c):
        rows = slice(c * C, (c + 1) * C)
        f = lb + (1.0 - lb) * _sigmoid(fh_ref[0, rows, :])
        g = jnp.log2(f)
        kin = 1.0 - f
        qh = packed(rows, HG_Q).astype(F32)
        qs = qh * _sigmoid(qh)
        g_hi, g_lo = _split_bf16(g, 2)
        cum = _dot(pm_ref[n_small * C:, :], jnp.concatenate([g_hi, g_lo], axis=1))
        b = cum[:, :GROUP_W] + cum[:, GROUP_W:]
        b_ref[c] = b
        q_dec = (qs * jnp.exp2(b)).astype(BF16)
        return dict(rows=rows, qs=qs, kin=kin, g_hi=g_hi, q_dec=q_dec)

    def level_log(c, pre, l):
        if l < n_small:
            return pre[l * C:(l + 1) * C]
        half = 1 << l
        parts = []
        for lo in range(0, C, 2 * half):
            mid = lo + half
            last_left = b_ref[c, mid - 1:mid, :]
            parts.append(last_left - b_ref[c, lo:mid, :])
            parts.append(b_ref[c, mid:mid + half, :] - last_left)
        return jnp.concatenate(parts, axis=0)

    def operands(c, p, l):
        if l == nl:
            return p["qs_b"], p["kin_b"]
        e_l = jnp.exp2(level_log(c, p["pre"], l)).astype(BF16)
        return p["qs_b"] * e_l, p["kin_b"] * e_l

    def scores_by_levels():
        ops = [dict(pre=_dot(pm_ref[:n_small * C, :], p["g_hi"]),
                    qs_b=p["qs"].astype(BF16), kin_b=p["kin"].astype(BF16)) for p in preps]
        level_of = lv_ref[...]
        order = [(l, c) for l in [nl] + list(range(nl)) for c in chunks]
        scores = [[0.0] * HG_HEADS for _ in chunks]
        nxt = operands(order[0][1], ops[order[0][1]], order[0][0])
        for idx, (l, c) in enumerate(order):
            q_l, k_l = nxt
            if idx + 1 < len(order):
                l_n, c_n = order[idx + 1]
                nxt = operands(c_n, ops[c_n], l_n)
            hit = level_of == l
            for h in range(HG_HEADS):
                scores[c][h] = jnp.where(hit, _dot_nt(head(q_l, h), head(k_l, h)), scores[c][h])
        for c in chunks:
            for h in range(HG_HEADS):
                sc_ref[c, h] = scores[c][h]

    def scores_from_chunk_start():
        on_or_below = lv_ref[...] >= 0
        for c, p in enumerate(preps):
            k_grow = (p["kin"] * jnp.exp2(-b_ref[c])).astype(BF16)
            for h in range(HG_HEADS):
                sc_ref[c, h] = jnp.where(
                    on_or_below, _dot_nt(head(p["q_dec"], h), head(k_grow, h)), 0.0)

    def finish(c, p):
        rows = p["rows"]
        stream = c if streams > 1 else 0
        b = b_ref[c]
        b_last = b_ref[c, C - 1:C, :]
        k_out = (p["kin"] * jnp.exp2(b_last - b)).astype(BF16)
        chunk_decay = jnp.exp2(b_last)
        i_b = packed(rows, HG_I)
        gh = packed(rows, HG_GATE).astype(F32)
        gate = ng_ref[...] * (gh * _sigmoid(gh))
        outs = []
        for h in range(HG_HEADS):
            st = st_ref[stream, h]
            o = (_dot(sc_ref[c, h].astype(BF16), head(i_b, h))
                 + _dot_nt(head(p["q_dec"], h), st.astype(BF16)))
            st_ref[stream, h] = (st * head(chunk_decay, h)
                                 + _dot_tn(head(i_b, h), head(k_out, h)))
            outs.append(o * lax.rsqrt(jnp.mean(o * o, axis=-1, keepdims=True) + RMS_EPS))
        return (jnp.concatenate(outs, axis=1) * gate).astype(BF16)

    def project_out(g, mixed_hgrn):
        rows = group_rows[g]
        hid = alpha * x_ref[0, rows, :] + (out_attn[g] + _dot(mixed_hgrn, w_hgrn))
        mu = jnp.mean(hid, axis=-1, keepdims=True)
        cen = hid - mu
        var = jnp.mean(cen * cen, axis=-1, keepdims=True)
        y_ref[0, rows, :] = cen * lax.rsqrt(var + LN_EPS) * lg_ref[...] + lnb_ref[...]

    chunks = range(sub)
    preps = [prepare(c) for c in chunks]
    lowest = b_ref[0, C - 1:C, :]
    for c in range(1, sub):
        lowest = jnp.minimum(lowest, b_ref[c, C - 1:C, :])
    bounded = jnp.min(lowest) > -BOUNDED_DECAY_LOG2
    pl.when(bounded)(scores_from_chunk_start)
    pl.when(jnp.logical_not(bounded))(scores_by_levels)
    mixed = []
    for c in chunks:
        mixed.append(finish(c, preps[c]))
        if len(mixed) == group:
            project_out(c // group, mixed[0] if group == 1 else jnp.concatenate(mixed, axis=0))
            mixed = []

    @pl.when(t == pl.num_programs(1) - 1)
    def _():
        for s in range(streams):
            for h in range(HG_HEADS):
                so_ref[0, s, h] = st_ref[s, h].T


def _hgrn_out(hg, fh, lb_logits, norm_g, s0, layer, ma, x, w_out, ln_g, ln_b, *, C, sub, alpha,
              streams_per_step=1):
    n_streams, T, _ = fh.shape
    streams = streams_per_step
    if streams > 1:
        assert T == C and sub == streams and n_streams % streams == 0
        fold = lambda a: a.reshape(n_streams // streams, streams * T, a.shape[-1])
        hg, fh, ma, x = fold(hg), fold(fh), fold(ma), fold(x)
    B, T, _ = fh.shape
    prefix, level_of, nl, n_small = _hgrn_tables(C)
    has_s0 = s0 is not None
    rows = sub * C
    row_spec = pl.BlockSpec((1, rows, GROUP_W), lambda b, t: (b, t, 0))
    row3_spec = pl.BlockSpec((1, rows, 3 * GROUP_W), lambda b, t: (b, t, 0))
    wide_spec = pl.BlockSpec((1, rows, D_MODEL), lambda b, t: (b, t, 0))
    const = lambda shape: pl.BlockSpec(shape, lambda b, t: (0,) * len(shape))
    state_shape = (1, streams, HG_HEADS, HG_HEAD_DIM, HG_HEAD_DIM)
    in_specs = [row3_spec, row_spec, const(prefix.shape), const(level_of.shape),
                const(lb_logits.shape), const((1, GROUP_W)), row_spec, wide_spec,
                const(w_out.shape), const((1, D_MODEL)), const((1, D_MODEL))]
    args = [hg, fh, prefix, level_of, lb_logits.astype(F32),
            norm_g.reshape(1, GROUP_W).astype(F32), ma, x, w_out,
            ln_g.reshape(1, D_MODEL).astype(F32), ln_b.reshape(1, D_MODEL).astype(F32)]
    if has_s0:
        in_specs.append(pl.BlockSpec(state_shape, lambda b, t: (layer, b, 0, 0, 0)))
        args.append(s0)
    y, s_out = pl.pallas_call(
        functools.partial(_hgrn_out_kernel, C=C, sub=sub, streams=streams, nl=nl, n_small=n_small,
                          layer=layer, has_s0=has_s0, alpha=alpha),
        grid=(B, T // rows),
        in_specs=in_specs,
        out_specs=[wide_spec, pl.BlockSpec(state_shape, lambda b, t: (0, b, 0, 0, 0))],
        out_shape=[jax.ShapeDtypeStruct((B, T, D_MODEL), F32),
                   jax.ShapeDtypeStruct((1, n_streams, HG_HEADS, HG_HEAD_DIM, HG_HEAD_DIM), F32)],
        scratch_shapes=[pltpu.VMEM((streams, HG_HEADS, HG_HEAD_DIM, HG_HEAD_DIM), F32),
                        pltpu.VMEM((sub, C, GROUP_W), F32),
                        pltpu.VMEM((sub, HG_HEADS, C, C), F32)],
        compiler_params=pltpu.CompilerParams(
            dimension_semantics=("arbitrary", "arbitrary"), vmem_limit_bytes=VMEM_LIMIT),
    )(*args)
    return y.reshape(n_streams, -1, D_MODEL), s_out


PROMPT_ROWS = 512
PROMPT_CHUNK = 128
OUT_GROUP_CHUNKS = 4
PROMPT_CHUNKS_PER_STEP = 8


def _layer(x, cache_k, cache_v, state_s, layer, w_in, w_out, lb_logits, norm_g, ln_g, ln_b, alpha):
    B, T, _ = x.shape
    decode = cache_k is not None
    if decode:
        sb, k, hg, fh, k_out, v_out = _project(x, w_in, nb=B, tm=T, k_transposed=False)
        ma = _sb_decode(sb, k, cache_k, cache_v, layer)
        y, s_out = _hgrn_out(hg, fh, lb_logits, norm_g, state_s, layer, ma, x, w_out, ln_g, ln_b,
                             C=T, sub=B, alpha=alpha, streams_per_step=B)
    else:
        ma, k_out, v_out = _sb_prompt(x, w_in)
        hg, fh = _project_hgrn(x, w_in, tm=min(PROMPT_ROWS, T))
        chunk = min(PROMPT_CHUNK, T)
        y, s_out = _hgrn_out(hg, fh, lb_logits, norm_g, None, layer, ma, x, w_out, ln_g, ln_b,
                             C=chunk, sub=min(PROMPT_CHUNKS_PER_STEP, T // chunk), alpha=alpha)
    return y, k_out, v_out, s_out


def kernel(x_prompt, x_sample, cache_k, cache_v, state_s, w_in, w_out, lb_logits, hgrn_norm_g,
           ln_g, ln_b):
    depth = w_in.shape[0]
    alpha = (2 * depth) ** 0.25
    yp, ys = x_prompt, x_sample
    per_layer = []
    for l in range(depth):
        common = (l, w_in[l], w_out[l], lb_logits, hgrn_norm_g[l], ln_g[l], ln_b[l], alpha)
        yp, kp, vp, sp = _layer(yp, None, None, None, *common)
        ys, kn, vn, sn = _layer(ys, cache_k, cache_v, state_s, *common)
        per_layer.append((kp, vp, sp, kn, vn, sn))
    stack = lambda i: (per_layer[0][i] if depth == 1
                       else jnp.concatenate([p[i] for p in per_layer], axis=0))
    return (yp, ys, stack(0), stack(1), stack(2), stack(3), stack(4), stack(5))
```

```python
import functools
import math

import numpy as np
import jax
import jax.numpy as jnp
from jax import lax
from jax.experimental import pallas as pl
from jax.experimental.pallas import tpu as pltpu

F32 = jnp.float32
BF16 = jnp.bfloat16

D_MODEL = 1024
GROUP_W = 512
N_GROUPS = 8
SB_HEADS = 8
SB_HEAD_DIM = 64
HG_HEADS = 4
HG_HEAD_DIM = 128
LN_EPS = 1e-5
RMS_EPS = 1e-6

SUBLANES = 8
KEY_TILE = 256
SB_Q, SB_V, SB_GATE = 0, 1, 2
HG_Q, HG_I, HG_GATE = 0, 1, 2
BOUNDED_DECAY_LOG2 = 100.0
NEGLIGIBLE_LOG2 = 200.0
LOG2E = 1.4426950408889634
VMEM_LIMIT = 56 * 1024 * 1024


def _sigmoid(x):
    return 1.0 / (1.0 + jnp.exp2(x * (-LOG2E)))


def _dot(a, b):
    return jnp.dot(a, b, preferred_element_type=F32)


def _dot_nt(a, b):
    return lax.dot_general(a, b, (((1,), (1,)), ((), ())), preferred_element_type=F32)


def _dot_tn(a, b):
    return lax.dot_general(a, b, (((0,), (0,)), ((), ())), preferred_element_type=F32)


def _split_bf16(x, parts):
    out = []
    for _ in range(parts - 1):
        p = x.astype(BF16)
        out.append(p)
        x = x - p.astype(F32)
    out.append(x.astype(BF16))
    return out


def _proj_kernel(x_ref, w_ref, sb_ref, k_ref, hg_ref, fh_ref, ko_ref, vo_ref, *, nb, tm,
                 k_transposed):
    m = nb * tm
    x = x_ref[...].reshape(m, D_MODEL).astype(BF16)

    def col(c):
        return _dot(x, w_ref[:, c * GROUP_W:(c + 1) * GROUP_W].astype(BF16))

    def put(ref, val, slot=0):
        ref[:, :, slot * GROUP_W:(slot + 1) * GROUP_W] = (
            val.reshape(nb, tm, GROUP_W).astype(ref.dtype))

    def put_heads(ref, val):
        if k_transposed:
            val_t = val.T
            for h in range(SB_HEADS):
                ref[0, 0, h] = val_t[h * SB_HEAD_DIM:(h + 1) * SB_HEAD_DIM, :]
            return val_t
        for b in range(nb):
            for h in range(SB_HEADS):
                ref[0, b, h] = val[b * tm:(b + 1) * tm, h * SB_HEAD_DIM:(h + 1) * SB_HEAD_DIM]

    put(sb_ref, col(0) * (SB_HEAD_DIM ** -0.5 * LOG2E), SB_Q)
    k = col(1)
    kt = put_heads(ko_ref, k)
    if k_transposed:
        kt = kt.astype(BF16)
        for j in range(m // KEY_TILE):
            k_ref[0, j] = kt[:, j * KEY_TILE:(j + 1) * KEY_TILE]
    else:
        put(k_ref, k)
    v = col(2)
    put_heads(vo_ref, v)
    put(sb_ref, v, SB_V)
    ga = col(3)
    put(sb_ref, ga * _sigmoid(ga), SB_GATE)
    put(hg_ref, col(4), HG_Q)
    put(fh_ref, col(5))
    put(hg_ref, col(6), HG_I)
    put(hg_ref, col(7), HG_GATE)


def _project(x, w, *, nb, tm, k_transposed):
    B, T, _ = x.shape
    grid = (B // nb, T // tm)
    act = lambda dt: jax.ShapeDtypeStruct((B, T, GROUP_W), dt)
    act_spec = pl.BlockSpec((nb, tm, GROUP_W), lambda b, t: (b, t, 0))
    act3 = jax.ShapeDtypeStruct((B, T, 3 * GROUP_W), BF16)
    act3_spec = pl.BlockSpec((nb, tm, 3 * GROUP_W), lambda b, t: (b, t, 0))
    if k_transposed:
        assert nb == 1 and tm % KEY_TILE == 0
        k_shape = jax.ShapeDtypeStruct((B, T // KEY_TILE, GROUP_W, KEY_TILE), BF16)
        k_spec = pl.BlockSpec((1, tm // KEY_TILE, GROUP_W, KEY_TILE), lambda b, t: (b, t, 0, 0))
        kv_shape = jax.ShapeDtypeStruct((1, B, SB_HEADS, SB_HEAD_DIM, T), F32)
        kv_spec = pl.BlockSpec((1, 1, SB_HEADS, SB_HEAD_DIM, tm), lambda b, t: (0, b, 0, 0, t))
    else:
        k_shape, k_spec = act(BF16), act_spec
        kv_shape = jax.ShapeDtypeStruct((1, B, SB_HEADS, T, SB_HEAD_DIM), F32)
        kv_spec = pl.BlockSpec((1, nb, SB_HEADS, tm, SB_HEAD_DIM), lambda b, t: (0, b, 0, t, 0))
    outs = pl.pallas_call(
        functools.partial(_proj_kernel, nb=nb, tm=tm, k_transposed=k_transposed),
        grid=grid,
        in_specs=[pl.BlockSpec((nb, tm, D_MODEL), lambda b, t: (b, t, 0)),
                  pl.BlockSpec((D_MODEL, N_GROUPS * GROUP_W), lambda b, t: (0, 0),
                               pipeline_mode=pl.Buffered(1))],
        out_specs=[act3_spec, k_spec, act3_spec, act_spec, kv_spec, kv_spec],
        out_shape=[act3, k_shape, act3, act(F32), kv_shape, kv_shape],
        compiler_params=pltpu.CompilerParams(
            dimension_semantics=("arbitrary", "arbitrary"), vmem_limit_bytes=VMEM_LIMIT),
    )(x, w)
    if k_transposed:
        outs = list(outs[:4]) + [jnp.swapaxes(o, 3, 4) for o in outs[4:]]
    return outs


def _suffix_matrix(n):
    r = lax.broadcasted_iota(jnp.int32, (n, n), 0)
    c = lax.broadcasted_iota(jnp.int32, (n, n), 1)
    return jnp.where(r >= c, 1.0, 0.0).astype(BF16)


def _sb_suffix_sums(zs, suffix, causal):
    drops = []
    for z in zs:
        drop = jnp.maximum(z, 0.0) + jnp.log2(1.0 + jnp.exp2(-jnp.abs(z)))
        if causal is not None:
            drop = jnp.where(causal, drop, 0.0)
        drops.append(drop.astype(BF16))
    return [_dot(drop, suffix) for drop in drops]


def _sb_weight(z, incl, run, causal):
    w = jnp.exp2(z - incl if run is None else z - incl - run)
    if causal is not None:
        w = jnp.where(causal, w, 0.0)
    return w.astype(BF16)


def _sb_prompt_kernel(sb_ref, kt_ref, o_ref, z_ref, w_ref, run_ref, acc_ref, state_ref, *, heads):
    tq = KEY_TILE
    nq = sb_ref.shape[1]
    suffix = _suffix_matrix(KEY_TILE)
    r = lax.broadcasted_iota(jnp.int32, (tq, KEY_TILE), 0)
    c = lax.broadcasted_iota(jnp.int32, (tq, KEY_TILE), 1)
    causal = c < r
    head_lanes = [slice(hh * SB_HEAD_DIM, (hh + 1) * SB_HEAD_DIM) for hh in range(heads)]

    def packed(slot, lanes):
        return slice(slot * GROUP_W + lanes.start, slot * GROUP_W + lanes.stop)

    def logits(bi, j, hh):
        return _dot(sb_ref[0, bi, :, packed(SB_Q, head_lanes[hh])],
                    kt_ref[0, j, head_lanes[hh], :])

    def add_weighted_values(slot, j):
        for hh in range(heads):
            acc_ref[hh] += _dot(w_ref[slot, hh], sb_ref[0, j, :, packed(SB_V, head_lanes[hh])])

    def finalize(bi):
        o = jnp.concatenate([acc_ref[hh] for hh in range(heads)], axis=1)
        gate = sb_ref[0, bi, :, SB_GATE * GROUP_W:(SB_GATE + 1) * GROUP_W].astype(F32)
        o_ref[0, pl.ds(pl.multiple_of(bi * tq, tq), tq), :] = (o * gate).astype(o_ref.dtype)

    def step(cur, masked, next_bi, next_j):
        mask = causal if masked else None
        zs = [z_ref[cur, hh] for hh in range(heads)]
        incls = _sb_suffix_sums(zs, suffix, mask)
        runs = [None if masked else run_ref[hh] for hh in range(heads)]
        for hh in range(heads):
            total = incls[hh][:, 0:1]
            run_ref[hh] = total if masked else runs[hh] + total
        if not masked:
            lowest = run_ref[0]
            for hh in range(1, heads):
                lowest = jnp.minimum(lowest, run_ref[hh])
            state_ref[2] = (jnp.min(lowest) < NEGLIGIBLE_LOG2).astype(jnp.int32)
        for hh in range(heads):
            z_ref[1 - cur, hh] = logits(next_bi, next_j, hh)
            w_ref[1 - cur, hh] = _sb_weight(zs[hh], incls[hh], runs[hh], mask)

    def refill(slot, bi, j):
        for hh in range(heads):
            z_ref[slot, hh] = logits(bi, j, hh)

    def per_slot(slot, cond, fn):
        for parity in range(2):
            pl.when(jnp.logical_and(cond, slot == parity))(functools.partial(fn, parity))

    for hh in range(heads):
        z_ref[0, hh] = logits(0, 0, hh)
    w_ref[0] = jnp.zeros_like(w_ref[0])
    acc_ref[...] = jnp.zeros_like(acc_ref)
    state_ref[0] = 0
    state_ref[1] = 0
    state_ref[2] = 0

    def block(i, carry):
        first_slot = state_ref[0]
        pending_tile = state_ref[1]
        nxt = jnp.minimum(i + 1, nq - 1)

        def enter_block(parity):
            add_weighted_values(parity, pending_tile)
            finalize(jnp.maximum(i - 1, 0))
            acc_ref[...] = jnp.zeros_like(acc_ref)

        def first_block(parity):
            enter_block(parity)
            step(parity, True, nxt, nxt)

        def later_block(parity):
            enter_block(parity)
            step(parity, True, i, i - 1)
            add_weighted_values(1 - parity, i)
            step(1 - parity, False, nxt, nxt)

        per_slot(first_slot, i == 0, first_block)
        per_slot(first_slot, i > 0, later_block)
        tiles = jnp.where(i > 0, 2, 1)
        slot = (first_slot + tiles) % 2
        last_tile = jnp.maximum(i - 1, 0)

        def weight_left():
            return state_ref[2] != 0

        go_on = jnp.logical_and(last_tile > 0, weight_left())
        per_slot(slot, go_on, lambda parity: refill(parity, i, last_tile - 1))

        def sweep(carry):
            n, _ = carry
            j = last_tile - 1 - n

            def tile(parity):
                add_weighted_values(parity, j + 1)
                same_block = j > 0
                step(parity, False, jnp.where(same_block, i, nxt),
                     jnp.where(same_block, j - 1, nxt))

            per_slot((slot + n) % 2, True, tile)
            return n + 1, jnp.logical_and(j > 0, weight_left())

        swept, _ = lax.while_loop(lambda carry: carry[1], sweep, (jnp.int32(0), go_on))
        slot = (slot + swept) % 2
        last_tile = last_tile - swept
        per_slot(slot, jnp.logical_and(swept > 0, last_tile > 0),
                 lambda parity: refill(parity, nxt, nxt))
        state_ref[0] = slot
        state_ref[1] = last_tile
        return carry

    lax.fori_loop(0, nq, block, 0)
    last_slot, last_tile = state_ref[0], state_ref[1]
    per_slot(last_slot, True, lambda parity: add_weighted_values(parity, last_tile))
    finalize(nq - 1)


def _sb_prompt(sb, kt):
    B, T, _ = sb.shape
    tq = KEY_TILE
    nkt = T // KEY_TILE
    sb4 = sb.reshape(B, nkt, KEY_TILE, 3 * GROUP_W)
    row_spec = pl.BlockSpec((1, T, GROUP_W), lambda b: (b, 0, 0))
    tile_spec = lambda shape: pl.BlockSpec((1,) + shape, lambda b: (b, 0, 0, 0))
    return pl.pallas_call(
        functools.partial(_sb_prompt_kernel, heads=SB_HEADS),
        grid=(B,),
        in_specs=[tile_spec((nkt, KEY_TILE, 3 * GROUP_W)), tile_spec((nkt, GROUP_W, KEY_TILE))],
        out_specs=row_spec,
        out_shape=jax.ShapeDtypeStruct((B, T, GROUP_W), BF16),
        scratch_shapes=[pltpu.VMEM((2, SB_HEADS, tq, KEY_TILE), F32),
                        pltpu.VMEM((2, SB_HEADS, tq, KEY_TILE), BF16),
                        pltpu.VMEM((SB_HEADS, tq, 1), F32),
                        pltpu.VMEM((SB_HEADS, tq, SB_HEAD_DIM), F32),
                        pltpu.SMEM((3,), jnp.int32)],
        compiler_params=pltpu.CompilerParams(
            dimension_semantics=("arbitrary",), vmem_limit_bytes=VMEM_LIMIT),
    )(sb4, kt)


def _sb_decode_kernel(sb_ref, kn_ref, ck_hbm, cv_hbm, o_ref, knew_ref, vnew_ref, kold_ref, vold_ref,
                      run_ref, acc_ref, sem_new, sem_old, *, layer):
    b = pl.program_id(0)
    slot = b % 2
    tq = sb_ref.shape[1]
    past = ck_hbm.shape[4]
    older = past - KEY_TILE
    suffix = _suffix_matrix(KEY_TILE)
    suffix_new = _suffix_matrix(tq)
    r = lax.broadcasted_iota(jnp.int32, (tq, tq), 0)
    c = lax.broadcasted_iota(jnp.int32, (tq, tq), 1)
    causal = c < r
    heads = range(SB_HEADS)
    head_lanes = [slice(hh * SB_HEAD_DIM, (hh + 1) * SB_HEAD_DIM) for hh in heads]
    group = lambda slot: sb_ref[0, :, slot * GROUP_W:(slot + 1) * GROUP_W]
    q_all, v_all = group(SB_Q), group(SB_V)
    qs = [q_all[:, lanes] for lanes in head_lanes]

    def newest_tile(stream, into):
        return [pltpu.make_async_copy(hbm.at[layer, stream, :, :, pl.ds(older, KEY_TILE)],
                                      buf.at[into], sem_new.at[i, into])
                for i, (hbm, buf) in enumerate(((ck_hbm, knew_ref), (cv_hbm, vnew_ref)))]

    def older_tiles():
        return [pltpu.make_async_copy(hbm.at[layer, b, :, :, pl.ds(0, older)], buf, sem_old.at[i])
                for i, (hbm, buf) in enumerate(((ck_hbm, kold_ref), (cv_hbm, vold_ref)))]

    @pl.when(b == 0)
    def _():
        for copy in newest_tile(0, 0):
            copy.start()

    @pl.when(b + 1 < pl.num_programs(0))
    def _():
        for copy in newest_tile(b + 1, 1 - slot):
            copy.start()

    z_new = [_dot_nt(q, kn_ref[0, :, lanes]) for q, lanes in zip(qs, head_lanes)]
    incl_new = _sb_suffix_sums(z_new, suffix_new, causal)
    w_new = [_sb_weight(z_new[hh], incl_new[hh], None, causal) for hh in heads]
    runs = [incl_new[hh][:, 0:1] for hh in heads]
    accs = [_dot(w, v_all[:, lanes]) for w, lanes in zip(w_new, head_lanes)]

    for copy in newest_tile(b, slot):
        copy.wait()
    z = [_dot(q, knew_ref[slot, hh].astype(BF16)) for q, hh in zip(qs, heads)]
    incl = _sb_suffix_sums(z, suffix, None)
    w = [_sb_weight(z[hh], incl[hh], runs[hh], None) for hh in heads]
    runs = [runs[hh] + incl[hh][:, 0:1] for hh in heads]
    lowest = runs[0]
    for hh in heads:
        run_ref[hh] = runs[hh]
        acc_ref[hh] = accs[hh] + _dot_nt(w[hh], vnew_ref[slot, hh].astype(BF16))
        lowest = jnp.minimum(lowest, runs[hh])

    if older > 0:
        @pl.when(jnp.min(lowest) < NEGLIGIBLE_LOG2)
        def _():
            for copy in older_tiles():
                copy.start()
            for copy in older_tiles():
                copy.wait()
            cache_keys = [slice(j * KEY_TILE, (j + 1) * KEY_TILE)
                          for j in reversed(range(older // KEY_TILE))]
            zs = [[_dot(q, kold_ref[hh, :, keys].astype(BF16)) for q, hh in zip(qs, heads)]
                  for keys in cache_keys]
            incls = [_sb_suffix_sums(z_tile, suffix, None) for z_tile in zs]
            run = [run_ref[hh] for hh in heads]
            ws = []
            for z_tile, incl_tile in zip(zs, incls):
                ws.append([_sb_weight(z_tile[hh], incl_tile[hh], run[hh], None) for hh in heads])
                run = [run[hh] + incl_tile[hh][:, 0:1] for hh in heads]
            for hh in heads:
                acc = acc_ref[hh]
                for w_tile, keys in zip(ws, cache_keys):
                    acc = acc + _dot_nt(w_tile[hh], vold_ref[hh, :, keys].astype(BF16))
                acc_ref[hh] = acc

    o = jnp.concatenate([acc_ref[hh] for hh in heads], axis=1) * group(SB_GATE).astype(F32)
    o_ref[0] = o.astype(o_ref.dtype)


def _sb_decode(sb, kn, cache_k, cache_v, layer):
    B, T, _ = kn.shape
    past = cache_k.shape[3]
    assert past % KEY_TILE == 0
    older = past - KEY_TILE
    row_spec = lambda width: pl.BlockSpec((1, T, width), lambda b: (b, 0, 0))
    cache_spec = pl.BlockSpec(memory_space=pl.ANY)
    cache_k = jnp.swapaxes(cache_k, 3, 4)
    cache_v = jnp.swapaxes(cache_v, 3, 4)
    newest = pltpu.VMEM((2, SB_HEADS, SB_HEAD_DIM, KEY_TILE), F32)
    rest = pltpu.VMEM((SB_HEADS, SB_HEAD_DIM, max(older, KEY_TILE)), F32)
    return pl.pallas_call(
        functools.partial(_sb_decode_kernel, layer=layer),
        grid=(B,),
        in_specs=[row_spec(3 * GROUP_W), row_spec(GROUP_W), cache_spec, cache_spec],
        out_specs=row_spec(GROUP_W),
        out_shape=jax.ShapeDtypeStruct((B, T, GROUP_W), BF16),
        scratch_shapes=[newest, newest, rest, rest,
                        pltpu.VMEM((SB_HEADS, T, 1), F32),
                        pltpu.VMEM((SB_HEADS, T, SB_HEAD_DIM), F32),
                        pltpu.SemaphoreType.DMA((2, 2)), pltpu.SemaphoreType.DMA((2,))],
        compiler_params=pltpu.CompilerParams(
            dimension_semantics=("arbitrary",), vmem_limit_bytes=VMEM_LIMIT),
    )(sb, kn, cache_k, cache_v)


def _hgrn_tables(C):
    nl = int(math.log2(C))
    assert 1 << nl == C
    n_small = min(nl, int(math.log2(SUBLANES)))
    t = np.arange(C)[:, None]
    j = np.arange(C)[None, :]
    blocks = []
    for l in range(n_small):
        half = 1 << l
        mid = (t >> (l + 1) << (l + 1)) + half
        right = (t & half) != 0
        blocks.append(np.where(right, (j >= mid) & (j <= t), (j > t) & (j < mid)))
    blocks.append(j <= t)
    prefix = np.concatenate(blocks, axis=0).astype(np.float32)
    x = t ^ j
    msb = np.floor(np.log2(np.maximum(x, 1))).astype(np.int32)
    level_of = np.where(t > j, msb, np.where(t == j, nl, -1)).astype(np.int32)
    return jnp.asarray(prefix, BF16), jnp.asarray(level_of), nl, n_small


def _hgrn_out_kernel(*refs, C, sub, streams, nl, n_small, layer, has_s0, alpha):
    (hg_ref, fh_ref, pm_ref, lv_ref, lbl_ref, ng_ref, ma_ref, x_ref, wo_ref,
     lg_ref, lnb_ref) = refs[:11]
    if has_s0:
        s0_ref, y_ref, so_ref, st_ref, b_ref, sc_ref = refs[11:]
    else:
        y_ref, so_ref, st_ref, b_ref, sc_ref = refs[11:]
    t = pl.program_id(1)

    def packed(rows, slot):
        return hg_ref[0, rows, slot * GROUP_W:(slot + 1) * GROUP_W]

    group = sub if streams > 1 else min(sub, OUT_GROUP_CHUNKS)
    group_rows = [slice(g * group * C, (g + 1) * group * C) for g in range(sub // group)]
    w_attn = wo_ref[:GROUP_W, :].astype(BF16)
    w_hgrn = wo_ref[GROUP_W:, :].astype(BF16)
    out_attn = [_dot(ma_ref[0, rows, :], w_attn) for rows in group_rows]

    @pl.when(t == 0)
    def _():
        if has_s0:
            for s in range(streams):
                for h in range(HG_HEADS):
                    st_ref[s, h] = s0_ref[0, s, h].T
        else:
            st_ref[...] = jnp.zeros_like(st_ref)

    logits = lbl_ref[...]
    e = jnp.exp(logits - jnp.max(logits, axis=0, keepdims=True))
    lb = jnp.sum(e[:layer + 1], axis=0, keepdims=True) / jnp.sum(e, axis=0, keepdims=True)

    def head(a, h):
        return a[:, h * HG_HEAD_DIM:(h + 1) * HG_HEAD_DIM]

    def prepare(c):
        rows = slice(c * C, (c + 1) * C)
        f = lb + (1.0 - lb) * _sigmoid(fh_ref[0, rows, :])
        g = jnp.log2(f)
        kin = 1.0 - f
        qh = packed(rows, HG_Q).astype(F32)
        qs = qh * _sigmoid(qh)
        g_hi, g_lo = _split_bf16(g, 2)
        cum = _dot(pm_ref[n_small * C:, :], jnp.concatenate([g_hi, g_lo], axis=1))
        b = cum[:, :GROUP_W] + cum[:, GROUP_W:]
        b_ref[c] = b
        q_dec = (qs * jnp.exp2(b)).astype(BF16)
        return dict(rows=rows, qs=qs, kin=kin, g_hi=g_hi, q_dec=q_dec)

    def level_log(c, pre, l):
        if l < n_small:
            return pre[l * C:(l + 1) * C]
        half = 1 << l
        parts = []
        for lo in range(0, C, 2 * half):
            mid = lo + half
            last_left = b_ref[c, mid - 1:mid, :]
            parts.append(last_left - b_ref[c, lo:mid, :])
            parts.append(b_ref[c, mid:mid + half, :] - last_left)
        return jnp.concatenate(parts, axis=0)

    def operands(c, p, l):
        if l == nl:
            return p["qs_b"], p["kin_b"]
        e_l = jnp.exp2(level_log(c, p["pre"], l)).astype(BF16)
        return p["qs_b"] * e_l, p["kin_b"] * e_l

    def scores_by_levels():
        ops = [dict(pre=_dot(pm_ref[:n_small * C, :], p["g_hi"]),
                    qs_b=p["qs"].astype(BF16), kin_b=p["kin"].astype(BF16)) for p in preps]
        level_of = lv_ref[...]
        order = [(l, c) for l in [nl] + list(range(nl)) for c in chunks]
        scores = [[0.0] * HG_HEADS for _ in chunks]
        nxt = operands(order[0][1], ops[order[0][1]], order[0][0])
        for idx, (l, c) in enumerate(order):
            q_l, k_l = nxt
            if idx + 1 < len(order):
                l_n, c_n = order[idx + 1]
                nxt = operands(c_n, ops[c_n], l_n)
            hit = level_of == l
            for h in range(HG_HEADS):
                scores[c][h] = jnp.where(hit, _dot_nt(head(q_l, h), head(k_l, h)), scores[c][h])
        for c in chunks:
            for h in range(HG_HEADS):
                sc_ref[c, h] = scores[c][h]

    def scores_from_chunk_start():
        on_or_below = lv_ref[...] >= 0
        for c, p in enumerate(preps):
            k_grow = (p["kin"] * jnp.exp2(-b_ref[c])).astype(BF16)
            for h in range(HG_HEADS):
                sc_ref[c, h] = jnp.where(
                    on_or_below, _dot_nt(head(p["q_dec"], h), head(k_grow, h)), 0.0)

    def finish(c, p):
        rows = p["rows"]
        stream = c if streams > 1 else 0
        b = b_ref[c]
        b_last = b_ref[c, C - 1:C, :]
        k_out = (p["kin"] * jnp.exp2(b_last - b)).astype(BF16)
        chunk_decay = jnp.exp2(b_last)
        i_b = packed(rows, HG_I)
        gh = packed(rows, HG_GATE).astype(F32)
        gate = ng_ref[...] * (gh * _sigmoid(gh))
        outs = []
        for h in range(HG_HEADS):
            st = st_ref[stream, h]
            o = (_dot(sc_ref[c, h].astype(BF16), head(i_b, h))
                 + _dot_nt(head(p["q_dec"], h), st.astype(BF16)))
            st_ref[stream, h] = (st * head(chunk_decay, h)
                                 + _dot_tn(head(i_b, h), head(k_out, h)))
            outs.append(o * lax.rsqrt(jnp.mean(o * o, axis=-1, keepdims=True) + RMS_EPS))
        return (jnp.concatenate(outs, axis=1) * gate).astype(BF16)

    def project_out(g, mixed_hgrn):
        rows = group_rows[g]
        hid = alpha * x_ref[0, rows, :] + (out_attn[g] + _dot(mixed_hgrn, w_hgrn))
        mu = jnp.mean(hid, axis=-1, keepdims=True)
        cen = hid - mu
        var = jnp.mean(cen * cen, axis=-1, keepdims=True)
        y_ref[0, rows, :] = cen * lax.rsqrt(var + LN_EPS) * lg_ref[...] + lnb_ref[...]

    chunks = range(sub)
    preps = [prepare(c) for c in chunks]
    lowest = b_ref[0, C - 1:C, :]
    for c in range(1, sub):
        lowest = jnp.minimum(lowest, b_ref[c, C - 1:C, :])
    bounded = jnp.min(lowest) > -BOUNDED_DECAY_LOG2
    pl.when(bounded)(scores_from_chunk_start)
    pl.when(jnp.logical_not(bounded))(scores_by_levels)
    mixed = []
    for c in chunks:
        mixed.append(finish(c, preps[c]))
        if len(mixed) == group:
            project_out(c // group, mixed[0] if group == 1 else jnp.concatenate(mixed, axis=0))
            mixed = []

    @pl.when(t == pl.num_programs(1) - 1)
    def _():
        for s in range(streams):
            for h in range(HG_HEADS):
                so_ref[0, s, h] = st_ref[s, h].T


def _hgrn_out(hg, fh, lb_logits, norm_g, s0, layer, ma, x, w_out, ln_g, ln_b, *, C, sub, alpha,
              streams_per_step=1):
    n_streams, T, _ = fh.shape
    streams = streams_per_step
    if streams > 1:
        assert T == C and sub == streams and n_streams % streams == 0
        fold = lambda a: a.reshape(n_streams // streams, streams * T, a.shape[-1])
        hg, fh, ma, x = fold(hg), fold(fh), fold(ma), fold(x)
    B, T, _ = fh.shape
    prefix, level_of, nl, n_small = _hgrn_tables(C)
    has_s0 = s0 is not None
    rows = sub * C
    row_spec = pl.BlockSpec((1, rows, GROUP_W), lambda b, t: (b, t, 0))
    row3_spec = pl.BlockSpec((1, rows, 3 * GROUP_W), lambda b, t: (b, t, 0))
    wide_spec = pl.BlockSpec((1, rows, D_MODEL), lambda b, t: (b, t, 0))
    const = lambda shape: pl.BlockSpec(shape, lambda b, t: (0,) * len(shape))
    state_shape = (1, streams, HG_HEADS, HG_HEAD_DIM, HG_HEAD_DIM)
    in_specs = [row3_spec, row_spec, const(prefix.shape), const(level_of.shape),
                const(lb_logits.shape), const((1, GROUP_W)), row_spec, wide_spec,
                const(w_out.shape), const((1, D_MODEL)), const((1, D_MODEL))]
    args = [hg, fh, prefix, level_of, lb_logits.astype(F32),
            norm_g.reshape(1, GROUP_W).astype(F32), ma, x, w_out,
            ln_g.reshape(1, D_MODEL).astype(F32), ln_b.reshape(1, D_MODEL).astype(F32)]
    if has_s0:
        in_specs.append(pl.BlockSpec(state_shape, lambda b, t: (layer, b, 0, 0, 0)))
        args.append(s0)
    y, s_out = pl.pallas_call(
        functools.partial(_hgrn_out_kernel, C=C, sub=sub, streams=streams, nl=nl, n_small=n_small,
                          layer=layer, has_s0=has_s0, alpha=alpha),
        grid=(B, T // rows),
        in_specs=in_specs,
        out_specs=[wide_spec, pl.BlockSpec(state_shape, lambda b, t: (0, b, 0, 0, 0))],
        out_shape=[jax.ShapeDtypeStruct((B, T, D_MODEL), F32),
                   jax.ShapeDtypeStruct((1, n_streams, HG_HEADS, HG_HEAD_DIM, HG_HEAD_DIM), F32)],
        scratch_shapes=[pltpu.VMEM((streams, HG_HEADS, HG_HEAD_DIM, HG_HEAD_DIM), F32),
                        pltpu.VMEM((sub, C, GROUP_W), F32),
                        pltpu.VMEM((sub, HG_HEADS, C, C), F32)],
        compiler_params=pltpu.CompilerParams(
            dimension_semantics=("arbitrary", "arbitrary"), vmem_limit_bytes=VMEM_LIMIT),
    )(*args)
    return y.reshape(n_streams, -1, D_MODEL), s_out


PROMPT_ROWS = 512
PROMPT_CHUNK = 128
OUT_GROUP_CHUNKS = 4
PROMPT_CHUNKS_PER_STEP = 8


def _layer(x, cache_k, cache_v, state_s, layer, w_in, w_out, lb_logits, norm_g, ln_g, ln_b, alpha):
    B, T, _ = x.shape
    decode = cache_k is not None
    if decode:
        sb, k, hg, fh, k_out, v_out = _project(x, w_in, nb=B, tm=T, k_transposed=False)
        ma = _sb_decode(sb, k, cache_k, cache_v, layer)
        y, s_out = _hgrn_out(hg, fh, lb_logits, norm_g, state_s, layer, ma, x, w_out, ln_g, ln_b,
                             C=T, sub=B, alpha=alpha, streams_per_step=B)
    else:
        sb, kt, hg, fh, k_out, v_out = _project(
            x, w_in, nb=1, tm=min(PROMPT_ROWS, T), k_transposed=True)
        ma = _sb_prompt(sb, kt)
        chunk = min(PROMPT_CHUNK, T)
        y, s_out = _hgrn_out(hg, fh, lb_logits, norm_g, None, layer, ma, x, w_out, ln_g, ln_b,
                             C=chunk, sub=min(PROMPT_CHUNKS_PER_STEP, T // chunk), alpha=alpha)
    return y, k_out, v_out, s_out


def kernel(x_prompt, x_sample, cache_k, cache_v, state_s, w_in, w_out, lb_logits, hgrn_norm_g,
           ln_g, ln_b):
    depth = w_in.shape[0]
    alpha = (2 * depth) ** 0.25
    yp, ys = x_prompt, x_sample
    per_layer = []
    for l in range(depth):
        common = (l, w_in[l], w_out[l], lb_logits, hgrn_norm_g[l], ln_g[l], ln_b[l], alpha)
        yp, kp, vp, sp = _layer(yp, None, None, None, *common)
        ys, kn, vn, sn = _layer(ys, cache_k, cache_v, state_s, *common)
        per_layer.append((kp, vp, sp, kn, vn, sn))
    stack = lambda i: (per_layer[0][i] if depth == 1
                       else jnp.concatenate([p[i] for p in per_layer], axis=0))
    return (yp, ys, stack(0), stack(1), stack(2), stack(3), stack(4), stack(5))
```

```python
import functools
import math

import numpy as np
import jax
import jax.numpy as jnp
from jax import lax
from jax.experimental import pallas as pl
from jax.experimental.pallas import tpu as pltpu

F32 = jnp.float32
BF16 = jnp.bfloat16

D_MODEL = 1024
GROUP_W = 512
N_GROUPS = 8
SB_HEADS = 8
SB_HEAD_DIM = 64
HG_HEADS = 4
HG_HEAD_DIM = 128
LN_EPS = 1e-5
RMS_EPS = 1e-6

SUBLANES = 8
KEY_TILE = 256
SB_Q, SB_V, SB_GATE = 0, 1, 2
HG_Q, HG_I, HG_GATE = 0, 1, 2
BOUNDED_DECAY_LOG2 = 100.0
NEGLIGIBLE_LOG2 = 200.0
LOG2E = 1.4426950408889634
VMEM_LIMIT = 56 * 1024 * 1024


def _sigmoid(x):
    return 1.0 / (1.0 + jnp.exp2(x * (-LOG2E)))


def _dot(a, b):
    return jnp.dot(a, b, preferred_element_type=F32)


def _dot_nt(a, b):
    return lax.dot_general(a, b, (((1,), (1,)), ((), ())), preferred_element_type=F32)


def _dot_tn(a, b):
    return lax.dot_general(a, b, (((0,), (0,)), ((), ())), preferred_element_type=F32)


def _split_bf16(x, parts):
    out = []
    for _ in range(parts - 1):
        p = x.astype(BF16)
        out.append(p)
        x = x - p.astype(F32)
    out.append(x.astype(BF16))
    return out


def _proj_kernel(x_ref, w_ref, sb_ref, k_ref, hg_ref, fh_ref, ko_ref, vo_ref, *, nb, tm,
                 k_transposed):
    m = nb * tm
    x = x_ref[...].reshape(m, D_MODEL).astype(BF16)

    def col(c):
        return _dot(x, w_ref[:, c * GROUP_W:(c + 1) * GROUP_W].astype(BF16))

    def put(ref, val, slot=0):
        ref[:, :, slot * GROUP_W:(slot + 1) * GROUP_W] = (
            val.reshape(nb, tm, GROUP_W).astype(ref.dtype))

    def put_heads(ref, val):
        if k_transposed:
            val_t = val.T
            for h in range(SB_HEADS):
                ref[0, 0, h] = val_t[h * SB_HEAD_DIM:(h + 1) * SB_HEAD_DIM, :]
            return val_t
        for b in range(nb):
            for h in range(SB_HEADS):
                ref[0, b, h] = val[b * tm:(b + 1) * tm, h * SB_HEAD_DIM:(h + 1) * SB_HEAD_DIM]

    put(sb_ref, col(0) * (SB_HEAD_DIM ** -0.5 * LOG2E), SB_Q)
    k = col(1)
    kt = put_heads(ko_ref, k)
    if k_transposed:
        kt = kt.astype(BF16)
        for j in range(m // KEY_TILE):
            k_ref[0, j] = kt[:, j * KEY_TILE:(j + 1) * KEY_TILE]
    else:
        put(k_ref, k)
    v = col(2)
    put_heads(vo_ref, v)
    put(sb_ref, v, SB_V)
    ga = col(3)
    put(sb_ref, ga * _sigmoid(ga), SB_GATE)
    put(hg_ref, col(4), HG_Q)
    put(fh_ref, col(5))
    put(hg_ref, col(6), HG_I)
    put(hg_ref, col(7), HG_GATE)


def _project(x, w, *, nb, tm, k_transposed):
    B, T, _ = x.shape
    grid = (B // nb, T // tm)
    act = lambda dt: jax.ShapeDtypeStruct((B, T, GROUP_W), dt)
    act_spec = pl.BlockSpec((nb, tm, GROUP_W), lambda b, t: (b, t, 0))
    act3 = jax.ShapeDtypeStruct((B, T, 3 * GROUP_W), BF16)
    act3_spec = pl.BlockSpec((nb, tm, 3 * GROUP_W), lambda b, t: (b, t, 0))
    if k_transposed:
        assert nb == 1 and tm % KEY_TILE == 0
        k_shape = jax.ShapeDtypeStruct((B, T // KEY_TILE, GROUP_W, KEY_TILE), BF16)
        k_spec = pl.BlockSpec((1, tm // KEY_TILE, GROUP_W, KEY_TILE), lambda b, t: (b, t, 0, 0))
        kv_shape = jax.ShapeDtypeStruct((1, B, SB_HEADS, SB_HEAD_DIM, T), F32)
        kv_spec = pl.BlockSpec((1, 1, SB_HEADS, SB_HEAD_DIM, tm), lambda b, t: (0, b, 0, 0, t))
    else:
        k_shape, k_spec = act(BF16), act_spec
        kv_shape = jax.ShapeDtypeStruct((1, B, SB_HEADS, T, SB_HEAD_DIM), F32)
        kv_spec = pl.BlockSpec((1, nb, SB_HEADS, tm, SB_HEAD_DIM), lambda b, t: (0, b, 0, t, 0))
    outs = pl.pallas_call(
        functools.partial(_proj_kernel, nb=nb, tm=tm, k_transposed=k_transposed),
        grid=grid,
        in_specs=[pl.BlockSpec((nb, tm, D_MODEL), lambda b, t: (b, t, 0)),
                  pl.BlockSpec((D_MODEL, N_GROUPS * GROUP_W), lambda b, t: (0, 0),
                               pipeline_mode=pl.Buffered(1))],
        out_specs=[act3_spec, k_spec, act3_spec, act_spec, kv_spec, kv_spec],
        out_shape=[act3, k_shape, act3, act(F32), kv_shape, kv_shape],
        compiler_params=pltpu.CompilerParams(
            dimension_semantics=("arbitrary", "arbitrary"), vmem_limit_bytes=VMEM_LIMIT),
    )(x, w)
    if k_transposed:
        outs = list(outs[:4]) + [jnp.swapaxes(o, 3, 4) for o in outs[4:]]
    return outs


def _suffix_matrix(n):
    r = lax.broadcasted_iota(jnp.int32, (n, n), 0)
    c = lax.broadcasted_iota(jnp.int32, (n, n), 1)
    return jnp.where(r >= c, 1.0, 0.0).astype(BF16)


def _sb_suffix_sums(zs, suffix, causal):
    drops = []
    for z in zs:
        drop = jnp.maximum(z, 0.0) + jnp.log2(1.0 + jnp.exp2(-jnp.abs(z)))
        if causal is not None:
            drop = jnp.where(causal, drop, 0.0)
        drops.append(drop.astype(BF16))
    return [_dot(drop, suffix) for drop in drops]


def _sb_weight(z, incl, run, causal):
    w = jnp.exp2(z - incl if run is None else z - incl - run)
    if causal is not None:
        w = jnp.where(causal, w, 0.0)
    return w.astype(BF16)


def _sb_prompt_kernel(sb_ref, kt_ref, o_ref, z_ref, w_ref, run_ref, acc_ref, state_ref, *, heads):
    tq = KEY_TILE
    nq = sb_ref.shape[1]
    suffix = _suffix_matrix(KEY_TILE)
    r = lax.broadcasted_iota(jnp.int32, (tq, KEY_TILE), 0)
    c = lax.broadcasted_iota(jnp.int32, (tq, KEY_TILE), 1)
    causal = c < r
    head_lanes = [slice(hh * SB_HEAD_DIM, (hh + 1) * SB_HEAD_DIM) for hh in range(heads)]

    def packed(slot, lanes):
        return slice(slot * GROUP_W + lanes.start, slot * GROUP_W + lanes.stop)

    def logits(bi, j, hh):
        return _dot(sb_ref[0, bi, :, packed(SB_Q, head_lanes[hh])],
                    kt_ref[0, j, head_lanes[hh], :])

    def add_weighted_values(slot, j):
        for hh in range(heads):
            acc_ref[hh] += _dot(w_ref[slot, hh], sb_ref[0, j, :, packed(SB_V, head_lanes[hh])])

    def finalize(bi):
        o = jnp.concatenate([acc_ref[hh] for hh in range(heads)], axis=1)
        gate = sb_ref[0, bi, :, SB_GATE * GROUP_W:(SB_GATE + 1) * GROUP_W].astype(F32)
        o_ref[0, pl.ds(pl.multiple_of(bi * tq, tq), tq), :] = (o * gate).astype(o_ref.dtype)

    def step(cur, masked, next_bi, next_j):
        mask = causal if masked else None
        zs = [z_ref[cur, hh] for hh in range(heads)]
        incls = _sb_suffix_sums(zs, suffix, mask)
        runs = [None if masked else run_ref[hh] for hh in range(heads)]
        for hh in range(heads):
            total = incls[hh][:, 0:1]
            run_ref[hh] = total if masked else runs[hh] + total
        if not masked:
            lowest = run_ref[0]
            for hh in range(1, heads):
                lowest = jnp.minimum(lowest, run_ref[hh])
            state_ref[2] = (jnp.min(lowest) < NEGLIGIBLE_LOG2).astype(jnp.int32)
        for hh in range(heads):
            z_ref[1 - cur, hh] = logits(next_bi, next_j, hh)
            w_ref[1 - cur, hh] = _sb_weight(zs[hh], incls[hh], runs[hh], mask)

    def refill(slot, bi, j):
        for hh in range(heads):
            z_ref[slot, hh] = logits(bi, j, hh)

    def per_slot(slot, cond, fn):
        for parity in range(2):
            pl.when(jnp.logical_and(cond, slot == parity))(functools.partial(fn, parity))

    for hh in range(heads):
        z_ref[0, hh] = logits(0, 0, hh)
    w_ref[0] = jnp.zeros_like(w_ref[0])
    acc_ref[...] = jnp.zeros_like(acc_ref)
    state_ref[0] = 0
    state_ref[1] = 0
    state_ref[2] = 0

    def block(i, carry):
        first_slot = state_ref[0]
        pending_tile = state_ref[1]
        nxt = jnp.minimum(i + 1, nq - 1)

        def enter_block(parity):
            add_weighted_values(parity, pending_tile)
            finalize(jnp.maximum(i - 1, 0))
            acc_ref[...] = jnp.zeros_like(acc_ref)

        def first_block(parity):
            enter_block(parity)
            step(parity, True, nxt, nxt)

        def later_block(parity):
            enter_block(parity)
            step(parity, True, i, i - 1)
            add_weighted_values(1 - parity, i)
            step(1 - parity, False, nxt, nxt)

        per_slot(first_slot, i == 0, first_block)
        per_slot(first_slot, i > 0, later_block)
        tiles = jnp.where(i > 0, 2, 1)
        slot = (first_slot + tiles) % 2
        last_tile = jnp.maximum(i - 1, 0)

        def weight_left():
            return state_ref[2] != 0

        go_on = jnp.logical_and(last_tile > 0, weight_left())
        per_slot(slot, go_on, lambda parity: refill(parity, i, last_tile - 1))

        def sweep(carry):
            n, _ = carry
            j = last_tile - 1 - n

            def tile(parity):
                add_weighted_values(parity, j + 1)
                same_block = j > 0
                step(parity, False, jnp.where(same_block, i, nxt),
                     jnp.where(same_block, j - 1, nxt))

            per_slot((slot + n) % 2, True, tile)
            return n + 1, jnp.logical_and(j > 0, weight_left())

        swept, _ = lax.while_loop(lambda carry: carry[1], sweep, (jnp.int32(0), go_on))
        slot = (slot + swept) % 2
        last_tile = last_tile - swept
        per_slot(slot, jnp.logical_and(swept > 0, last_tile > 0),
                 lambda parity: refill(parity, nxt, nxt))
        state_ref[0] = slot
        state_ref[1] = last_tile
        return carry

    lax.fori_loop(0, nq, block, 0)
    last_slot, last_tile = state_ref[0], state_ref[1]
    per_slot(last_slot, True, lambda parity: add_weighted_values(parity, last_tile))
    finalize(nq - 1)


def _sb_prompt(sb, kt):
    B, T, _ = sb.shape
    tq = KEY_TILE
    nkt = T // KEY_TILE
    sb4 = sb.reshape(B, nkt, KEY_TILE, 3 * GROUP_W)
    row_spec = pl.BlockSpec((1, T, GROUP_W), lambda b: (b, 0, 0))
    tile_spec = lambda shape: pl.BlockSpec((1,) + shape, lambda b: (b, 0, 0, 0))
    return pl.pallas_call(
        functools.partial(_sb_prompt_kernel, heads=SB_HEADS),
        grid=(B,),
        in_specs=[tile_spec((nkt, KEY_TILE, 3 * GROUP_W)), tile_spec((nkt, GROUP_W, KEY_TILE))],
        out_specs=row_spec,
        out_shape=jax.ShapeDtypeStruct((B, T, GROUP_W), BF16),
        scratch_shapes=[pltpu.VMEM((2, SB_HEADS, tq, KEY_TILE), F32),
                        pltpu.VMEM((2, SB_HEADS, tq, KEY_TILE), BF16),
                        pltpu.VMEM((SB_HEADS, tq, 1), F32),
                        pltpu.VMEM((SB_HEADS, tq, SB_HEAD_DIM), F32),
                        pltpu.SMEM((3,), jnp.int32)],
        compiler_params=pltpu.CompilerParams(
            dimension_semantics=("arbitrary",), vmem_limit_bytes=VMEM_LIMIT),
    )(sb4, kt)


DECODE_RING = 4


def _sb_decode_kernel(sb_ref, kn_ref, ck_hbm, cv_hbm, o_ref, knew_ref, vnew_ref, kold_ref, vold_ref,
                      run_ref, acc_ref, sem_new, sem_old, *, layer):
    b = pl.program_id(0)
    slots = knew_ref.shape[0]
    ahead = slots - 1
    slot = b % slots
    tq = sb_ref.shape[1]
    past = ck_hbm.shape[4]
    older = past - KEY_TILE
    suffix = _suffix_matrix(KEY_TILE)
    suffix_new = _suffix_matrix(tq)
    r = lax.broadcasted_iota(jnp.int32, (tq, tq), 0)
    c = lax.broadcasted_iota(jnp.int32, (tq, tq), 1)
    causal = c < r
    heads = range(SB_HEADS)
    head_lanes = [slice(hh * SB_HEAD_DIM, (hh + 1) * SB_HEAD_DIM) for hh in heads]
    group = lambda slot: sb_ref[0, :, slot * GROUP_W:(slot + 1) * GROUP_W]
    q_all, v_all = group(SB_Q), group(SB_V)
    qs = [q_all[:, lanes] for lanes in head_lanes]

    def newest_tile(stream, into):
        return [pltpu.make_async_copy(hbm.at[layer, stream, :, :, pl.ds(older, KEY_TILE)],
                                      buf.at[into], sem_new.at[i, into])
                for i, (hbm, buf) in enumerate(((ck_hbm, knew_ref), (cv_hbm, vnew_ref)))]

    def older_tiles():
        return [pltpu.make_async_copy(hbm.at[layer, b, :, :, pl.ds(0, older)], buf, sem_old.at[i])
                for i, (hbm, buf) in enumerate(((ck_hbm, kold_ref), (cv_hbm, vold_ref)))]

    def start_newest(stream, into):
        for copy in newest_tile(stream, into):
            copy.start()

    for s in range(ahead):
        pl.when(jnp.logical_and(b == 0, s < pl.num_programs(0)))(
            functools.partial(start_newest, s, s))
    pl.when(b + ahead < pl.num_programs(0))(
        functools.partial(start_newest, b + ahead, (b + ahead) % slots))

    z_new = [_dot_nt(q, kn_ref[0, :, lanes]) for q, lanes in zip(qs, head_lanes)]
    incl_new = _sb_suffix_sums(z_new, suffix_new, causal)
    w_new = [_sb_weight(z_new[hh], incl_new[hh], None, causal) for hh in heads]
    runs = [incl_new[hh][:, 0:1] for hh in heads]
    accs = [_dot(w, v_all[:, lanes]) for w, lanes in zip(w_new, head_lanes)]

    for copy in newest_tile(b, slot):
        copy.wait()
    z = [_dot(q, knew_ref[slot, hh].astype(BF16)) for q, hh in zip(qs, heads)]
    incl = _sb_suffix_sums(z, suffix, None)
    w = [_sb_weight(z[hh], incl[hh], runs[hh], None) for hh in heads]
    runs = [runs[hh] + incl[hh][:, 0:1] for hh in heads]
    lowest = runs[0]
    for hh in heads:
        run_ref[hh] = runs[hh]
        acc_ref[hh] = accs[hh] + _dot_nt(w[hh], vnew_ref[slot, hh].astype(BF16))
        lowest = jnp.minimum(lowest, runs[hh])

    if older > 0:
        @pl.when(jnp.min(lowest) < NEGLIGIBLE_LOG2)
        def _():
            for copy in older_tiles():
                copy.start()
            for copy in older_tiles():
                copy.wait()
            cache_keys = [slice(j * KEY_TILE, (j + 1) * KEY_TILE)
                          for j in reversed(range(older // KEY_TILE))]
            zs = [[_dot(q, kold_ref[hh, :, keys].astype(BF16)) for q, hh in zip(qs, heads)]
                  for keys in cache_keys]
            incls = [_sb_suffix_sums(z_tile, suffix, None) for z_tile in zs]
            run = [run_ref[hh] for hh in heads]
            ws = []
            for z_tile, incl_tile in zip(zs, incls):
                ws.append([_sb_weight(z_tile[hh], incl_tile[hh], run[hh], None) for hh in heads])
                run = [run[hh] + incl_tile[hh][:, 0:1] for hh in heads]
            for hh in heads:
                acc = acc_ref[hh]
                for w_tile, keys in zip(ws, cache_keys):
                    acc = acc + _dot_nt(w_tile[hh], vold_ref[hh, :, keys].astype(BF16))
                acc_ref[hh] = acc

    o = jnp.concatenate([acc_ref[hh] for hh in heads], axis=1) * group(SB_GATE).astype(F32)
    o_ref[0] = o.astype(o_ref.dtype)


def _sb_decode(sb, kn, cache_k, cache_v, layer):
    B, T, _ = kn.shape
    past = cache_k.shape[3]
    assert past % KEY_TILE == 0
    older = past - KEY_TILE
    row_spec = lambda width: pl.BlockSpec((1, T, width), lambda b: (b, 0, 0))
    cache_spec = pl.BlockSpec(memory_space=pl.ANY)
    cache_k = jnp.swapaxes(cache_k, 3, 4)
    cache_v = jnp.swapaxes(cache_v, 3, 4)
    newest = pltpu.VMEM((DECODE_RING, SB_HEADS, SB_HEAD_DIM, KEY_TILE), F32)
    rest = pltpu.VMEM((SB_HEADS, SB_HEAD_DIM, max(older, KEY_TILE)), F32)
    return pl.pallas_call(
        functools.partial(_sb_decode_kernel, layer=layer),
        grid=(B,),
        in_specs=[row_spec(3 * GROUP_W), row_spec(GROUP_W), cache_spec, cache_spec],
        out_specs=row_spec(GROUP_W),
        out_shape=jax.ShapeDtypeStruct((B, T, GROUP_W), BF16),
        scratch_shapes=[newest, newest, rest, rest,
                        pltpu.VMEM((SB_HEADS, T, 1), F32),
                        pltpu.VMEM((SB_HEADS, T, SB_HEAD_DIM), F32),
                        pltpu.SemaphoreType.DMA((2, DECODE_RING)), pltpu.SemaphoreType.DMA((2,))],
        compiler_params=pltpu.CompilerParams(
            dimension_semantics=("arbitrary",), vmem_limit_bytes=VMEM_LIMIT),
    )(sb, kn, cache_k, cache_v)


def _hgrn_tables(C):
    nl = int(math.log2(C))
    assert 1 << nl == C
    n_small = min(nl, int(math.log2(SUBLANES)))
    t = np.arange(C)[:, None]
    j = np.arange(C)[None, :]
    blocks = []
    for l in range(n_small):
        half = 1 << l
        mid = (t >> (l + 1) << (l + 1)) + half
        right = (t & half) != 0
        blocks.append(np.where(right, (j >= mid) & (j <= t), (j > t) & (j < mid)))
    blocks.append(j <= t)
    prefix = np.concatenate(blocks, axis=0).astype(np.float32)
    x = t ^ j
    msb = np.floor(np.log2(np.maximum(x, 1))).astype(np.int32)
    level_of = np.where(t > j, msb, np.where(t == j, nl, -1)).astype(np.int32)
    return jnp.asarray(prefix, BF16), jnp.asarray(level_of), nl, n_small


def _hgrn_out_kernel(*refs, C, sub, streams, nl, n_small, layer, has_s0, alpha):
    (hg_ref, fh_ref, pm_ref, lv_ref, lbl_ref, ng_ref, ma_ref, x_ref, wo_ref,
     lg_ref, lnb_ref) = refs[:11]
    if has_s0:
        s0_ref, y_ref, so_ref, st_ref, b_ref, sc_ref = refs[11:]
    else:
        y_ref, so_ref, st_ref, b_ref, sc_ref = refs[11:]
    t = pl.program_id(1)

    def packed(rows, slot):
        return hg_ref[0, rows, slot * GROUP_W:(slot + 1) * GROUP_W]

    group = sub if streams > 1 else min(sub, OUT_GROUP_CHUNKS)
    group_rows = [slice(g * group * C, (g + 1) * group * C) for g in range(sub // group)]
    w_attn = wo_ref[:GROUP_W, :].astype(BF16)
    w_hgrn = wo_ref[GROUP_W:, :].astype(BF16)
    out_attn = [_dot(ma_ref[0, rows, :], w_attn) for rows in group_rows]

    @pl.when(t == 0)
    def _():
        if has_s0:
            for s in range(streams):
                for h in range(HG_HEADS):
                    st_ref[s, h] = s0_ref[0, s, h].T
        else:
            st_ref[...] = jnp.zeros_like(st_ref)

    logits = lbl_ref[...]
    e = jnp.exp(logits - jnp.max(logits, axis=0, keepdims=True))
    lb = jnp.sum(e[:layer + 1], axis=0, keepdims=True) / jnp.sum(e, axis=0, keepdims=True)

    def head(a, h):
        return a[:, h * HG_HEAD_DIM:(h + 1) * HG_HEAD_DIM]

    def prepare(c):
        rows = slice(c * C, (c + 1) * C)
        f = lb + (1.0 - lb) * _sigmoid(fh_ref[0, rows, :])
        g = jnp.log2(f)
        kin = 1.0 - f
        qh = packed(rows, HG_Q).astype(F32)
        qs = qh * _sigmoid(qh)
        g_hi, g_lo = _split_bf16(g, 2)
        cum = _dot(pm_ref[n_small * C:, :], jnp.concatenate([g_hi, g_lo], axis=1))
        b = cum[:, :GROUP_W] + cum[:, GROUP_W:]
        b_ref[c] = b
        q_dec = (qs * jnp.exp2(b)).astype(BF16)
        return dict(rows=rows, qs=qs, kin=kin, g_hi=g_hi, q_dec=q_dec)

    def level_log(c, pre, l):
        if l < n_small:
            return pre[l * C:(l + 1) * C]
        half = 1 << l
        parts = []
        for lo in range(0, C, 2 * half):
            mid = lo + half
            last_left = b_ref[c, mid - 1:mid, :]
            parts.append(last_left - b_ref[c, lo:mid, :])
            parts.append(b_ref[c, mid:mid + half, :] - last_left)
        return jnp.concatenate(parts, axis=0)

    def operands(c, p, l):
        if l == nl:
            return p["qs_b"], p["kin_b"]
        e_l = jnp.exp2(level_log(c, p["pre"], l)).astype(BF16)
        return p["qs_b"] * e_l, p["kin_b"] * e_l

    def scores_by_levels():
        ops = [dict(pre=_dot(pm_ref[:n_small * C, :], p["g_hi"]),
                    qs_b=p["qs"].astype(BF16), kin_b=p["kin"].astype(BF16)) for p in preps]
        level_of = lv_ref[...]
        order = [(l, c) for l in [nl] + list(range(nl)) for c in chunks]
        scores = [[0.0] * HG_HEADS for _ in chunks]
        nxt = operands(order[0][1], ops[order[0][1]], order[0][0])
        for idx, (l, c) in enumerate(order):
            q_l, k_l = nxt
            if idx + 1 < len(order):
                l_n, c_n = order[idx + 1]
                nxt = operands(c_n, ops[c_n], l_n)
            hit = level_of == l
            for h in range(HG_HEADS):
                scores[c][h] = jnp.where(hit, _dot_nt(head(q_l, h), head(k_l, h)), scores[c][h])
        for c in chunks:
            for h in range(HG_HEADS):
                sc_ref[c, h] = scores[c][h]

    def scores_from_chunk_start():
        on_or_below = lv_ref[...] >= 0
        for c, p in enumerate(preps):
            k_grow = (p["kin"] * jnp.exp2(-b_ref[c])).astype(BF16)
            for h in range(HG_HEADS):
                sc_ref[c, h] = jnp.where(
                    on_or_below, _dot_nt(head(p["q_dec"], h), head(k_grow, h)), 0.0)

    def finish(c, p):
        rows = p["rows"]
        stream = c if streams > 1 else 0
        b = b_ref[c]
        b_last = b_ref[c, C - 1:C, :]
        k_out = (p["kin"] * jnp.exp2(b_last - b)).astype(BF16)
        chunk_decay = jnp.exp2(b_last)
        i_b = packed(rows, HG_I)
        gh = packed(rows, HG_GATE).astype(F32)
        gate = ng_ref[...] * (gh * _sigmoid(gh))
        outs = []
        for h in range(HG_HEADS):
            st = st_ref[stream, h]
            o = (_dot(sc_ref[c, h].astype(BF16), head(i_b, h))
                 + _dot_nt(head(p["q_dec"], h), st.astype(BF16)))
            st_ref[stream, h] = (st * head(chunk_decay, h)
                                 + _dot_tn(head(i_b, h), head(k_out, h)))
            outs.append(o * lax.rsqrt(jnp.mean(o * o, axis=-1, keepdims=True) + RMS_EPS))
        return (jnp.concatenate(outs, axis=1) * gate).astype(BF16)

    def project_out(g, mixed_hgrn):
        rows = group_rows[g]
        hid = alpha * x_ref[0, rows, :] + (out_attn[g] + _dot(mixed_hgrn, w_hgrn))
        mu = jnp.mean(hid, axis=-1, keepdims=True)
        cen = hid - mu
        var = jnp.mean(cen * cen, axis=-1, keepdims=True)
        y_ref[0, rows, :] = cen * lax.rsqrt(var + LN_EPS) * lg_ref[...] + lnb_ref[...]

    chunks = range(sub)
    preps = [prepare(c) for c in chunks]
    lowest = b_ref[0, C - 1:C, :]
    for c in range(1, sub):
        lowest = jnp.minimum(lowest, b_ref[c, C - 1:C, :])
    bounded = jnp.min(lowest) > -BOUNDED_DECAY_LOG2
    pl.when(bounded)(scores_from_chunk_start)
    pl.when(jnp.logical_not(bounded))(scores_by_levels)
    mixed = []
    for c in chunks:
        mixed.append(finish(c, preps[c]))
        if len(mixed) == group:
            project_out(c // group, mixed[0] if group == 1 else jnp.concatenate(mixed, axis=0))
            mixed = []

    @pl.when(t == pl.num_programs(1) - 1)
    def _():
        for s in range(streams):
            for h in range(HG_HEADS):
                so_ref[0, s, h] = st_ref[s, h].T


def _hgrn_out(hg, fh, lb_logits, norm_g, s0, layer, ma, x, w_out, ln_g, ln_b, *, C, sub, alpha,
              streams_per_step=1):
    n_streams, T, _ = fh.shape
    streams = streams_per_step
    if streams > 1:
        assert T == C and sub == streams and n_streams % streams == 0
        fold = lambda a: a.reshape(n_streams // streams, streams * T, a.shape[-1])
        hg, fh, ma, x = fold(hg), fold(fh), fold(ma), fold(x)
    B, T, _ = fh.shape
    prefix, level_of, nl, n_small = _hgrn_tables(C)
    has_s0 = s0 is not None
    rows = sub * C
    row_spec = pl.BlockSpec((1, rows, GROUP_W), lambda b, t: (b, t, 0))
    row3_spec = pl.BlockSpec((1, rows, 3 * GROUP_W), lambda b, t: (b, t, 0))
    wide_spec = pl.BlockSpec((1, rows, D_MODEL), lambda b, t: (b, t, 0))
    const = lambda shape: pl.BlockSpec(shape, lambda b, t: (0,) * len(shape))
    state_shape = (1, streams, HG_HEADS, HG_HEAD_DIM, HG_HEAD_DIM)
    in_specs = [row3_spec, row_spec, const(prefix.shape), const(level_of.shape),
                const(lb_logits.shape), const((1, GROUP_W)), row_spec, wide_spec,
                const(w_out.shape), const((1, D_MODEL)), const((1, D_MODEL))]
    args = [hg, fh, prefix, level_of, lb_logits.astype(F32),
            norm_g.reshape(1, GROUP_W).astype(F32), ma, x, w_out,
            ln_g.reshape(1, D_MODEL).astype(F32), ln_b.reshape(1, D_MODEL).astype(F32)]
    if has_s0:
        in_specs.append(pl.BlockSpec(state_shape, lambda b, t: (layer, b, 0, 0, 0)))
        args.append(s0)
    y, s_out = pl.pallas_call(
        functools.partial(_hgrn_out_kernel, C=C, sub=sub, streams=streams, nl=nl, n_small=n_small,
                          layer=layer, has_s0=has_s0, alpha=alpha),
        grid=(B, T // rows),
        in_specs=in_specs,
        out_specs=[wide_spec, pl.BlockSpec(state_shape, lambda b, t: (0, b, 0, 0, 0))],
        out_shape=[jax.ShapeDtypeStruct((B, T, D_MODEL), F32),
                   jax.ShapeDtypeStruct((1, n_streams, HG_HEADS, HG_HEAD_DIM, HG_HEAD_DIM), F32)],
        scratch_shapes=[pltpu.VMEM((streams, HG_HEADS, HG_HEAD_DIM, HG_HEAD_DIM), F32),
                        pltpu.VMEM((sub, C, GROUP_W), F32),
                        pltpu.VMEM((sub, HG_HEADS, C, C), F32)],
        compiler_params=pltpu.CompilerParams(
            dimension_semantics=("arbitrary", "arbitrary"), vmem_limit_bytes=VMEM_LIMIT),
    )(*args)
    return y.reshape(n_streams, -1, D_MODEL), s_out


PROMPT_ROWS = 512
PROMPT_CHUNK = 128
OUT_GROUP_CHUNKS = 4
PROMPT_CHUNKS_PER_STEP = 8


def _layer(x, cache_k, cache_v, state_s, layer, w_in, w_out, lb_logits, norm_g, ln_g, ln_b, alpha):
    B, T, _ = x.shape
    decode = cache_k is not None
    if decode:
        sb, k, hg, fh, k_out, v_out = _project(x, w_in, nb=B, tm=T, k_transposed=False)
        ma = _sb_decode(sb, k, cache_k, cache_v, layer)
        y, s_out = _hgrn_out(hg, fh, lb_logits, norm_g, state_s, layer, ma, x, w_out, ln_g, ln_b,
                             C=T, sub=B, alpha=alpha, streams_per_step=B)
    else:
        sb, kt, hg, fh, k_out, v_out = _project(
            x, w_in, nb=1, tm=min(PROMPT_ROWS, T), k_transposed=True)
        ma = _sb_prompt(sb, kt)
        chunk = min(PROMPT_CHUNK, T)
        y, s_out = _hgrn_out(hg, fh, lb_logits, norm_g, None, layer, ma, x, w_out, ln_g, ln_b,
                             C=chunk, sub=min(PROMPT_CHUNKS_PER_STEP, T // chunk), alpha=alpha)
    return y, k_out, v_out, s_out


def kernel(x_prompt, x_sample, cache_k, cache_v, state_s, w_in, w_out, lb_logits, hgrn_norm_g,
           ln_g, ln_b):
    depth = w_in.shape[0]
    alpha = (2 * depth) ** 0.25
    yp, ys = x_prompt, x_sample
    per_layer = []
    for l in range(depth):
        common = (l, w_in[l], w_out[l], lb_logits, hgrn_norm_g[l], ln_g[l], ln_b[l], alpha)
        yp, kp, vp, sp = _layer(yp, None, None, None, *common)
        ys, kn, vn, sn = _layer(ys, cache_k, cache_v, state_s, *common)
        per_layer.append((kp, vp, sp, kn, vn, sn))
    stack = lambda i: (per_layer[0][i] if depth == 1
                       else jnp.concatenate([p[i] for p in per_layer], axis=0))
    return (yp, ys, stack(0), stack(1), stack(2), stack(3), stack(4), stack(5))
```
